```python
import jax
import jax.numpy as jnp
from jax import lax
import numpy as np

D_MODEL = 1024
BATCH = 16
SEQ = 2048
DEPTH = 2

GRID_W = 64
CTX_LEN = 256
ML_HEADS = 4
ML_DH = D_MODEL // 16
ML_W = ML_HEADS * ML_DH
ML_CONV = 3
HG_HEADS = 4
HG_DK = D_MODEL // 16
HG_DV = D_MODEL // 16
HG_WK = HG_HEADS * HG_DK
HG_W = HG_HEADS * HG_DV
MLA_HEADS = 8
MLA_NOPE = D_MODEL // 16
MLA_ROPE = D_MODEL // 32
MLA_DQK = MLA_NOPE + MLA_ROPE
MLA_DV = D_MODEL // 16
MLA_W = MLA_HEADS * MLA_DV
MLA_Q_RANK = D_MODEL // 4
MLA_KV_RANK = D_MODEL // 8
MIX_W = ML_W + HG_W + MLA_W
N_EXPERTS = 16
EC_CAPACITY = 2
D_EXPERT = D_MODEL
CHUNK = 64
Q_BLOCK = 128
ROPE_THETA = 10000.0
EPS = 1e-6
IN_SPLITS = (2 * ML_W, ML_W, ML_W, 4 * ML_HEADS,
             HG_WK, HG_W, HG_W, 2 * HG_WK,
             MLA_Q_RANK, MLA_KV_RANK, MLA_ROPE)
D_IN = sum(IN_SPLITS)

kernel_name = 'hybrid_mlstm_hgrn2_mla_ecmoe_dit'


def _rms(x, g):
    xf = x.astype(jnp.float32)
    y = xf * lax.rsqrt(jnp.mean(xf * xf, axis=-1, keepdims=True) + EPS)
    return (y * g.astype(jnp.float32)).astype(x.dtype)


def _heads(a, n_heads):
    b, n, _ = a.shape
    return a.reshape(b, n, n_heads, -1).transpose(0, 2, 1, 3)


def _merge_heads(a):
    b, h, n, d = a.shape
    return a.transpose(0, 2, 1, 3).reshape(b, n, h * d)


def _head_norm_merge(a, g):
    b, h, n, d = a.shape
    return _rms(a.transpose(0, 2, 1, 3), g.reshape(h, d)).reshape(b, n, h * d)


def _dwconv(a, w, bias):
    ch = a.shape[-1]
    y = lax.conv_general_dilated(a, w[:, None, :].astype(a.dtype), window_strides=(1,), padding='SAME',
                                 dimension_numbers=('NWC', 'WIO', 'NWC'), feature_group_count=ch)
    return y + bias.astype(a.dtype)


def _split_proj(h, w_in, b_in):
    points = np.cumsum(IN_SPLITS)[:-1].tolist()
    return jnp.split(h @ w_in + b_in, points, axis=-1)


def _rope_tables(pos):
    half = MLA_ROPE // 2
    inv = ROPE_THETA ** (-jnp.arange(0, half, 2, dtype=jnp.float32) / half)
    ang = pos.astype(jnp.float32)[:, None] * inv[None, :]
    ang = jnp.concatenate([ang, ang], axis=-1)
    return jnp.cos(ang), jnp.sin(ang)


def _rotate(x, cos, sin):
    x1, x2 = jnp.split(x, 2, axis=-1)
    return (x * cos + jnp.concatenate([-x2, x1], axis=-1) * sin).astype(x.dtype)


def _axial_rope(x, rope):
    cos_r, sin_r, cos_c, sin_c = rope
    x_nope, x_row, x_col = jnp.split(x, [MLA_NOPE, MLA_NOPE + MLA_ROPE // 2], axis=-1)
    return jnp.concatenate([x_nope, _rotate(x_row, cos_r, sin_r), _rotate(x_col, cos_c, sin_c)], axis=-1)


def _to_chunks(a):
    b, h, n = a.shape[:3]
    return jnp.moveaxis(a.reshape((b, h, n // CHUNK, CHUNK) + a.shape[3:]), 2, 0)


def _from_chunks(y):
    nc, b, h, l = y.shape[:4]
    return jnp.moveaxis(y, 0, 2).reshape((b, h, nc * l) + y.shape[4:])


def _chunked_scan(out_fn, update_fn, inputs, state0):
    def step(state, chunk):
        return update_fn(state, chunk), out_fn(state, chunk)
    state, ys = lax.scan(step, state0, tuple(_to_chunks(a) for a in inputs))
    return _from_chunks(ys), state


def _mlstm_out(state, chunk):
    c_mat, n_vec, m = state
    q, k, v, log_f, ig = chunk
    L = log_f.shape[-1]
    seen = jnp.tril(jnp.ones((L, L), dtype=bool))
    b = jnp.cumsum(log_f, axis=-1)
    d = jnp.where(seen, b[..., :, None] - b[..., None, :] + ig[..., None, :], -jnp.inf)
    inter = b + m[..., None]
    m_t = jnp.maximum(inter, d.max(axis=-1))
    w_tok = jnp.exp(d - m_t[..., None])
    w_st = jnp.exp(inter - m_t)
    s = jnp.einsum('bhtd,bhsd->bhts', q, k) * w_tok
    num = jnp.einsum('bhts,bhsv->bhtv', s, v) + w_st[..., None] * jnp.einsum('bhtk,bhkv->bhtv', q, c_mat)
    den = s.sum(axis=-1) + w_st * jnp.einsum('bhtk,bhk->bht', q, n_vec)
    return num / jnp.maximum(jnp.abs(den), jnp.exp(-m_t))[..., None]


def _mlstm_update(state, chunk):
    c_mat, n_vec, m = state
    _, k, v, log_f, ig = chunk
    b = jnp.cumsum(log_f, axis=-1)
    b_end = b[..., -1]
    w_log = b_end[..., None] - b + ig
    m_new = jnp.maximum(b_end + m, w_log.max(axis=-1))
    w_tok = jnp.exp(w_log - m_new[..., None])
    w_st = jnp.exp(b_end + m - m_new)
    c_new = w_st[..., None, None] * c_mat + jnp.einsum('bhs,bhsk,bhsv->bhkv', w_tok, k, v)
    n_new = w_st[..., None] * n_vec + jnp.einsum('bhs,bhsk->bhk', w_tok, k)
    return (c_new, n_new, m_new)


def _hgrn_out(s_mat, chunk):
    q, k, v, log_f = chunk
    L = log_f.shape[-2]
    seen = jnp.tril(jnp.ones((L, L), dtype=bool))[:, :, None]
    b = jnp.cumsum(log_f, axis=-2)
    dec = jnp.exp(jnp.where(seen, b[..., :, None, :] - b[..., None, :, :], -jnp.inf))
    a = jnp.einsum('bhtk,bhsk,bhtsk->bhts', q, k, dec)
    return jnp.einsum('bhts,bhsv->bhtv', a, v) + jnp.einsum('bhtk,bhkv->bhtv', q * jnp.exp(b), s_mat)


def _hgrn_update(s_mat, chunk):
    _, k, v, log_f = chunk
    b = jnp.cumsum(log_f, axis=-2)
    b_end = b[..., -1:, :]
    return jnp.exp(b_end[..., 0, :])[..., None] * s_mat + jnp.einsum('bhsk,bhsv->bhkv', k * jnp.exp(b_end - b), v)


def _flip_time(inputs):
    return tuple(jnp.flip(a, axis=2) for a in inputs)


def _bidir_recurrence(out_fn, update_fn, ctx_dirs, lat_dirs, state0, need_ctx):
    y_ctx, y_lat = None, None
    for d in range(2):
        c_in, l_in = ctx_dirs[d], lat_dirs[d]
        if d == 1:
            c_in, l_in = _flip_time(c_in), _flip_time(l_in)
        if need_ctx:
            yc, state = _chunked_scan(out_fn, update_fn, c_in, state0)
            yc = jnp.flip(yc, axis=2) if d == 1 else yc
            y_ctx = yc if y_ctx is None else y_ctx + yc
        else:
            state = update_fn(state0, c_in)
        yl, _ = _chunked_scan(out_fn, update_fn, l_in, state)
        yl = jnp.flip(yl, axis=2) if d == 1 else yl
        y_lat = yl if y_lat is None else y_lat + yl
    return y_ctx, y_lat


def _mlstm_dirs(qk, v, gates):
    f32 = jnp.float32
    q, k = jnp.split(qk, 2, axis=-1)
    q = _heads(q, ML_HEADS).astype(f32)
    k = _heads(k, ML_HEADS).astype(f32) * (ML_DH ** -0.5)
    v = _heads(v, ML_HEADS).astype(f32)
    b, n, _ = gates.shape
    g = jnp.moveaxis(gates.astype(f32).reshape(b, n, 2, 2, ML_HEADS), 1, -1)
    return tuple((q, k, v, jax.nn.log_sigmoid(g[:, d, 1]), g[:, d, 0]) for d in range(2))


def _hgrn_dirs(q, i, f, lb):
    f32 = jnp.float32
    q = _heads(jax.nn.silu(q), HG_HEADS).astype(f32)
    v = _heads(i, HG_HEADS).astype(f32)
    b, n, _ = f.shape
    f = f.astype(f32).reshape(b, n, 2, HG_WK)
    dirs = []
    for d in range(2):
        pre, low = f[:, :, d], lb[d]
        log_f = jnp.logaddexp(jnp.log(low), jnp.log1p(-low) + jax.nn.log_sigmoid(pre))
        k = (1.0 - low) * jax.nn.sigmoid(-pre)
        dirs.append((q, _heads(k, HG_HEADS), v, _heads(log_f, HG_HEADS)))
    return tuple(dirs)


def _mla_q(cq, p, rope):
    b, n, _ = cq.shape
    q = (_rms(cq, p['mla_q_norm_g']) @ p['mla_w_uq']).reshape(b, n, MLA_HEADS, MLA_DQK)
    q = _rms(q, p['mla_q_qk_g']).transpose(0, 2, 1, 3)
    return q if rope is None else _axial_rope(q, rope)


def _mla_kv(ckv, kr, p, rope):
    b, n, _ = ckv.shape
    kv = (_rms(ckv, p['mla_kv_norm_g']) @ p['mla_w_ukv']).reshape(b, n, MLA_HEADS, MLA_NOPE + MLA_DV)
    k_nope, v = jnp.split(kv, [MLA_NOPE], axis=-1)
    k_rope = jnp.broadcast_to(kr[:, :, None, :], (b, n, MLA_HEADS, MLA_ROPE))
    k = _rms(jnp.concatenate([k_nope, k_rope], axis=-1), p['mla_k_qk_g']).transpose(0, 2, 1, 3)
    k = k if rope is None else _axial_rope(k, rope)
    return k, v.transpose(0, 2, 1, 3)


def _attend(q, k, v):
    s = jnp.einsum('bhqd,bhkd->bhqk', q, k).astype(jnp.float32) * (MLA_DQK ** -0.5)
    pr = jax.nn.softmax(s, axis=-1).astype(v.dtype)
    return jnp.einsum('bhqk,bhkd->bhqd', pr, v)


def _block_attend(q, k, v):
    b, h, n, d = q.shape
    qb = jnp.moveaxis(q.reshape(b, h, n // Q_BLOCK, Q_BLOCK, d), 2, 0)
    o = lax.map(lambda qq: _attend(qq, k, v), qb)
    return jnp.moveaxis(o, 0, 2).reshape(b, h, n, -1)


def _token_mixers(hc, hl, rope, lb, p, need_ctx):
    f32 = jnp.float32
    pc = _split_proj(hc, p['w_in'], p['b_in'])
    pl = _split_proj(hl, p['w_in'], p['b_in'])
    bsz = hl.shape[0]

    def ml_dirs(parts):
        qk = jax.nn.silu(_dwconv(parts[0], p['ml_conv_w'], p['ml_conv_b']))
        return _mlstm_dirs(qk, parts[1], parts[3])
    ml_state0 = (jnp.zeros((bsz, ML_HEADS, ML_DH, ML_DH), f32),
                 jnp.zeros((bsz, ML_HEADS, ML_DH), f32),
                 jnp.zeros((bsz, ML_HEADS), f32))
    ml_c, ml_l = _bidir_recurrence(_mlstm_out, _mlstm_update, ml_dirs(pc), ml_dirs(pl), ml_state0, need_ctx)

    hg_state0 = jnp.zeros((bsz, HG_HEADS, HG_DK, HG_DV), f32)
    hg_c, hg_l = _bidir_recurrence(_hgrn_out, _hgrn_update, _hgrn_dirs(pc[4], pc[5], pc[7], lb),
                                   _hgrn_dirs(pl[4], pl[5], pl[7], lb), hg_state0, need_ctx)

    k_c, v_c = _mla_kv(pc[9], pc[10], p, None)
    k_l, v_l = _mla_kv(pl[9], pl[10], p, rope)
    q_l = _mla_q(pl[8], p, rope)
    at_l = _block_attend(q_l, jnp.concatenate([k_c, k_l], axis=2), jnp.concatenate([v_c, v_l], axis=2))

    def merge(parts, ml, hg, at, dtype):
        y = jnp.concatenate([
            _head_norm_merge(ml, p['ml_norm_g']) * jax.nn.sigmoid(parts[2].astype(f32)),
            _head_norm_merge(hg, p['hg_norm_g']) * jax.nn.silu(parts[6].astype(f32)),
            _merge_heads(at).astype(f32)], axis=-1).astype(dtype)
        return y @ p['w_out']

    y_lat = merge(pl, ml_l, hg_l, at_l, hl.dtype)
    if not need_ctx:
        return None, y_lat
    at_c = _attend(_mla_q(pc[8], p, None), k_c, v_c)
    return merge(pc, ml_c, hg_c, at_c, hc.dtype), y_lat


def _expert_choice_ffn(h, router_w, w_gate, w_up, w_down):
    b, n, _ = h.shape
    cap = EC_CAPACITY * n // N_EXPERTS
    aff = jax.nn.softmax((h @ router_w).astype(jnp.float32), axis=-1)
    gate, idx = lax.top_k(jnp.swapaxes(aff, 1, 2), cap)
    bidx = jnp.arange(b)[:, None, None]
    xs = h[bidx, idx]
    hid = jax.nn.silu(jnp.einsum('becd,edf->becf', xs, w_gate)) * jnp.einsum('becd,edf->becf', xs, w_up)
    out = jnp.einsum('becf,efd->becd', hid, w_down) * gate[..., None].astype(h.dtype)
    return jnp.zeros_like(h).at[bidx, idx].add(out)


def setup_inputs(seed: int = 0) -> dict:
    key = jax.random.key(seed)
    ks = jax.random.split(key, 26)
    nrm = jax.random.normal
    f32 = jnp.float32
    D, L = D_MODEL, DEPTH
    fb = np.zeros((D_IN,), np.float32)
    off = 4 * ML_W
    fb[off + ML_HEADS: off + 2 * ML_HEADS] = np.linspace(3.0, 6.0, ML_HEADS)
    fb[off + 3 * ML_HEADS: off + 4 * ML_HEADS] = np.linspace(3.0, 6.0, ML_HEADS)
    return {
        'x': nrm(ks[0], (BATCH, SEQ, D), f32),
        'c': nrm(ks[1], (BATCH, D), f32),
        'ctx': nrm(ks[2], (BATCH, CTX_LEN, D), f32),
        'c_ctx': nrm(ks[3], (D,), f32),
        'ada_w': nrm(ks[4], (L, D, 6 * D), f32) * (0.5 * D ** -0.5),
        'ada_b': nrm(ks[5], (L, 6 * D), f32) * 0.02,
        'norm1_g': 1.0 + 0.05 * nrm(ks[6], (L, D), f32),
        'norm2_g': 1.0 + 0.05 * nrm(ks[7], (L, D), f32),
        'w_in': nrm(ks[8], (L, D, D_IN), f32) * D ** -0.5,
        'b_in': nrm(ks[9], (L, D_IN), f32) * 0.02 + jnp.asarray(fb),
        'ml_conv_w': nrm(ks[10], (L, ML_CONV, 2 * ML_W), f32) * ML_CONV ** -0.5,
        'ml_conv_b': nrm(ks[11], (L, 2 * ML_W), f32) * 0.02,
        'ml_norm_g': 1.0 + 0.05 * nrm(ks[12], (L, ML_W), f32),
        'hg_lb_logits': nrm(ks[13], (L, 2, HG_WK), f32) * 0.1,
        'hg_norm_g': 1.0 + 0.05 * nrm(ks[14], (L, HG_W), f32),
        'mla_q_norm_g': 1.0 + 0.05 * nrm(ks[15], (L, MLA_Q_RANK), f32),
        'mla_w_uq': nrm(ks[16], (L, MLA_Q_RANK, MLA_HEADS * MLA_DQK), f32) * MLA_Q_RANK ** -0.5,
        'mla_kv_norm_g': 1.0 + 0.05 * nrm(ks[17], (L, MLA_KV_RANK), f32),
        'mla_w_ukv': nrm(ks[18], (L, MLA_KV_RANK, MLA_HEADS * (MLA_NOPE + MLA_DV)), f32) * MLA_KV_RANK ** -0.5,
        'mla_q_qk_g': 1.0 + 0.05 * nrm(ks[19], (L, MLA_DQK), f32),
        'mla_k_qk_g': 1.0 + 0.05 * nrm(ks[20], (L, MLA_DQK), f32),
        'w_out': nrm(ks[21], (L, MIX_W, D), f32) * MIX_W ** -0.5,
        'router_w': nrm(ks[22], (L, D, N_EXPERTS), f32) * D ** -0.5,
        'ex_w_gate': nrm(ks[23], (L, N_EXPERTS, D, D_EXPERT), f32) * D ** -0.5,
        'ex_w_up': nrm(ks[24], (L, N_EXPERTS, D, D_EXPERT), f32) * D ** -0.5,
        'ex_w_down': nrm(ks[25], (L, N_EXPERTS, D_EXPERT, D), f32) * D_EXPERT ** -0.5,
    }


def reference(x, c, ctx, c_ctx, ada_w, ada_b, norm1_g, norm2_g, w_in, b_in, ml_conv_w, ml_conv_b,
              ml_norm_g, hg_lb_logits, hg_norm_g, mla_q_norm_g, mla_w_uq, mla_kv_norm_g, mla_w_ukv,
              mla_q_qk_g, mla_k_qk_g, w_out, router_w, ex_w_gate, ex_w_up, ex_w_down):
    n_lat = x.shape[1]
    rows = n_lat // GRID_W
    row_pos = jnp.repeat(jnp.arange(rows), GRID_W)
    col_pos = jnp.broadcast_to(jnp.arange(GRID_W), (rows, GRID_W)).reshape(-1)
    cos_r, sin_r = _rope_tables(row_pos)
    cos_c, sin_c = _rope_tables(col_pos)
    rope = (cos_r, sin_r, cos_c, sin_c)

    lb_all = jnp.cumsum(jax.nn.softmax(hg_lb_logits.astype(jnp.float32), axis=0), axis=0)
    lb_all = lb_all - lb_all[:1]

    for layer in range(DEPTH):
        need_ctx = layer < DEPTH - 1
        p = {'w_in': w_in[layer], 'b_in': b_in[layer], 'ml_conv_w': ml_conv_w[layer],
             'ml_conv_b': ml_conv_b[layer], 'ml_norm_g': ml_norm_g[layer], 'hg_norm_g': hg_norm_g[layer],
             'mla_q_norm_g': mla_q_norm_g[layer], 'mla_w_uq': mla_w_uq[layer],
             'mla_kv_norm_g': mla_kv_norm_g[layer], 'mla_w_ukv': mla_w_ukv[layer],
             'mla_q_qk_g': mla_q_qk_g[layer], 'mla_k_qk_g': mla_k_qk_g[layer], 'w_out': w_out[layer]}
        mod = jnp.split(jax.nn.silu(c) @ ada_w[layer] + ada_b[layer], 6, axis=-1)
        sh1, sc1, g1, sh2, sc2, g2 = [m[:, None, :] for m in mod]
        csh1, csc1, cg1, csh2, csc2, cg2 = jnp.split(jax.nn.silu(c_ctx) @ ada_w[layer] + ada_b[layer], 6, axis=-1)

        hl = _rms(x, norm1_g[layer]) * (1.0 + sc1) + sh1
        hc = _rms(ctx, norm1_g[layer]) * (1.0 + csc1) + csh1
        y_ctx, y_lat = _token_mixers(hc, hl, rope, lb_all[layer], p, need_ctx)
        x = x + g1 * y_lat
        x = x + g2 * _expert_choice_ffn(_rms(x, norm2_g[layer]) * (1.0 + sc2) + sh2, router_w[layer],
                                        ex_w_gate[layer], ex_w_up[layer], ex_w_down[layer])
        if need_ctx:
            ctx = ctx + cg1 * y_ctx
            ctx = ctx + cg2 * _expert_choice_ffn(_rms(ctx, norm2_g[layer]) * (1.0 + csc2) + csh2, router_w[layer],
                                                 ex_w_gate[layer], ex_w_up[layer], ex_w_down[layer])
    return x
```

```python
import functools

import numpy as np
import jax
import jax.numpy as jnp
from jax import lax
from jax.experimental import pallas as pl
from jax.experimental.pallas import tpu as pltpu

F32 = jnp.float32
BF16 = jnp.bfloat16
HIGHEST = lax.Precision.HIGHEST

D_MODEL = 1024
GRID_W = 64
ML_HEADS = 4
ML_DH = 64
ML_W = 256
HG_HEADS = 4
HG_W = 256
MLA_HEADS = 8
MLA_NOPE = 64
MLA_ROPE = 32
MLA_DQK = 96
MLA_DV = 64
MLA_Q_RANK = 256
MLA_KV_RANK = 128
N_EXPERTS = 16
EC_CAPACITY = 2
CHUNK = 64
ROPE_THETA = 10000.0
EPS = 1e-6
HEAD_PAD = 128
LANES = 128
N_LEVELS = 6
N_MM_LEVELS = 3
SLAB = ML_HEADS * CHUNK
VMEM_LIMIT = 52 * 1024 * 1024

_O_QK, _O_V, _O_OG, _O_GATES, _O_HQ, _O_HI, _O_HGATE, _O_HF, _O_CQ, _O_CKV, _O_KR, _O_END = (
    0, 512, 768, 1024, 1040, 1296, 1552, 1808, 2320, 2576, 2704, 2736)


def _nt(a, b, **kw):
    return lax.dot_general(a, b, (((1,), (1,)), ((), ())), preferred_element_type=F32, **kw)


def _tn(a, b, **kw):
    return lax.dot_general(a, b, (((0,), (0,)), ((), ())), preferred_element_type=F32, **kw)


def _mm(a, b, **kw):
    return jnp.dot(a, b, preferred_element_type=F32, **kw)


def _split(x):
    hi = x.astype(BF16)
    return hi, (x - hi.astype(F32)).astype(BF16)


def _silu(x):
    return x * jax.nn.sigmoid(x)


def _log_sigmoid(x):
    return jnp.minimum(x, 0.0) - jnp.log1p(jnp.exp(-jnp.abs(x)))


def _rms_rows(x, g):
    return x * lax.rsqrt(jnp.mean(x * x, axis=-1, keepdims=True) + EPS) * g


def _params(sem=None):
    return pltpu.CompilerParams(dimension_semantics=sem, vmem_limit_bytes=VMEM_LIMIT)


def _const_spec(a):
    return pl.BlockSpec(a.shape, lambda *_: (0,) * a.ndim)


def _mod_kernel(c_ref, w_ref, b_ref, o_ref):
    s = _silu(c_ref[...]).astype(BF16)
    o_ref[0] = _mm(s, w_ref[0].astype(BF16)) + b_ref[0]


def _modulation(cc, ada_w, ada_b):
    n_layers, d, n6 = ada_w.shape
    rows = cc.shape[0]
    tn = 1024
    return pl.pallas_call(
        _mod_kernel,
        grid=(n_layers, n6 // tn),
        in_specs=[pl.BlockSpec((rows, d), lambda l, j: (0, 0)),
                  pl.BlockSpec((1, d, tn), lambda l, j: (l, 0, j)),
                  pl.BlockSpec((1, 1, tn), lambda l, j: (l, 0, j))],
        out_specs=pl.BlockSpec((1, rows, tn), lambda l, j: (l, 0, j)),
        out_shape=jax.ShapeDtypeStruct((n_layers, rows, n6), F32),
        compiler_params=_params(("arbitrary", "arbitrary")),
        name="modulation",
    )(cc, ada_w, ada_b.reshape(n_layers, 1, n6))


def _in_kernel(x_ref, mod_ref, g_ref, w_ref, b_ref, wgt_ref, bgt_ref,
               oml_ref, ohg_ref, ogate_ref, omla_ref, ogt_ref):
    x = x_ref[0]
    sh = mod_ref[0, 0, 0:1, :]
    sc = mod_ref[0, 0, 1:2, :]
    h = (_rms_rows(x, g_ref[...]) * (1.0 + sc) + sh).astype(BF16)
    oml_ref[0] = _mm(h, w_ref[:, 0:768]) + b_ref[:, 0:768]
    ohg_ref[0] = _mm(h, w_ref[:, 768:1792]) + b_ref[:, 768:1792]
    ogate_ref[0] = _mm(h, w_ref[:, 1792:2304]) + b_ref[:, 1792:2304]
    omla_ref[0] = _mm(h, w_ref[:, 2304:2816]) + b_ref[:, 2304:2816]
    ogt_ref[0] = _nt(wgt_ref[...], h) + bgt_ref[...]


def _in_proj(xs, modtok, g1, w_in, b_in, tm, nct):
    bsz, t, d = xs.shape
    zeros = lambda n: jnp.zeros((d, n), F32)
    w = jnp.concatenate([
        w_in[:, _O_QK:_O_OG], w_in[:, _O_HQ:_O_HGATE], w_in[:, _O_HF:_O_CQ],
        w_in[:, _O_OG:_O_GATES], w_in[:, _O_HGATE:_O_HF], w_in[:, _O_CQ:_O_KR],
        w_in[:, _O_GATES:_O_HQ], zeros(48), w_in[:, _O_KR:_O_END], zeros(32)], axis=1).astype(BF16)
    zb = lambda n: jnp.zeros((n,), F32)
    b = jnp.concatenate([
        b_in[_O_QK:_O_OG], b_in[_O_HQ:_O_HGATE], b_in[_O_HF:_O_CQ],
        b_in[_O_OG:_O_GATES], b_in[_O_HGATE:_O_HF], b_in[_O_CQ:_O_KR],
        b_in[_O_GATES:_O_HQ], zb(48), b_in[_O_KR:_O_END], zb(32)])[None, :]
    wgt = w_in[:, _O_GATES:_O_HQ].T.astype(BF16)
    bgt = b_in[_O_GATES:_O_HQ][:, None]
    nw = w.shape[1]
    tok = lambda width: pl.BlockSpec((1, tm, width), lambda i, j: (i, j, 0))
    full = lambda shape: pl.BlockSpec(shape, lambda i, j: (0,) * len(shape))
    return pl.pallas_call(
        _in_kernel,
        grid=(bsz, t // tm),
        in_specs=[tok(d),
                  pl.BlockSpec((1, 1, 6, d), lambda i, j: (i, jnp.where(j >= nct, 1, 0), 0, 0)),
                  full((1, d)), full((d, nw)), full((1, nw)), full((16, d)), full((16, 1))],
        out_specs=[tok(768), tok(1024), tok(512), tok(512),
                   pl.BlockSpec((1, 16, tm), lambda i, j: (i, 0, j))],
        out_shape=[jax.ShapeDtypeStruct((bsz, t, 768), F32),
                   jax.ShapeDtypeStruct((bsz, t, 1024), F32),
                   jax.ShapeDtypeStruct((bsz, t, 512), F32),
                   jax.ShapeDtypeStruct((bsz, t, 512), F32),
                   jax.ShapeDtypeStruct((bsz, 16, t), F32)],
        compiler_params=_params(("arbitrary", "arbitrary")),
        name="in_proj",
    )(xs, modtok, g1[None, :], w, b, wgt, bgt)


def _dir1_chunk(i, n_ctx_chunks, n_chunks):
    return jnp.where(i < n_ctx_chunks, n_ctx_chunks - 1 - i, n_chunks - 1 - (i - n_ctx_chunks))


def _head_tables():
    same = (np.arange(SLAB)[:, None] // CHUNK == np.arange(SLAB)[None, :] // CHUNK).astype(np.float32)
    return jnp.asarray(same, BF16), jnp.asarray(same, F32)


def _stack_heads(x_bf, same_bf):
    return jnp.concatenate([x_bf] * ML_HEADS, axis=0) * same_bf


def _mm_split(a, b):
    hi, lo = _split(a)
    return _mm(hi, b) + _mm(lo, b)


def _mlstm_tables():
    t = np.arange(CHUNK)
    lower = (t[:, None] >= t[None, :]).astype(np.float32)
    tri = np.stack([lower, lower.T])
    neg = np.stack([np.tile(np.where(m > 0, 0.0, -np.inf), (1, ML_HEADS)) for m in (lower, lower.T)]).astype(np.float32)
    spread = np.zeros((2, LANES, ML_W), np.float32)
    for d in range(2):
        for h in range(ML_HEADS):
            spread[d, d * 8 + 4 + h, h * ML_DH:(h + 1) * ML_DH] = 1.0
    gather = spread.transpose(0, 2, 1)
    return (jnp.asarray(tri, BF16), jnp.asarray(neg), jnp.asarray(spread, BF16), jnp.asarray(gather, BF16),
            jnp.asarray(gather))


def _mlstm_kernel(ml_ref, gc_ref, gt_ref, cw_ref, cb_ref, tri_ref, neg_ref, spread_ref, gather_ref, gatherf_ref,
                  sameb_ref, samef_ref, o_ref, qk_s, c_s, n_s, m_s, *, t, ctx_len, tm):
    n_chunks = t // CHUNK
    n_ctx_chunks = ctx_len // CHUNK
    cw = cw_ref[...]
    cb = cb_ref[...]
    rid = lax.broadcasted_iota(jnp.int32, (tm, 2 * ML_W), 0)
    for j in range(t // tm):
        r0 = j * tm
        cur = ml_ref[0, r0:r0 + tm, 0:2 * ML_W]
        up = pltpu.roll(cur, 1, 0)
        if r0 in (0, ctx_len):
            up = jnp.where(rid == 0, 0.0, up)
        else:
            up = jnp.where(rid == 0, ml_ref[0, r0 - 1:r0, 0:2 * ML_W], up)
        dn = pltpu.roll(cur, tm - 1, 0)
        if r0 + tm in (ctx_len, t):
            dn = jnp.where(rid == tm - 1, 0.0, dn)
        else:
            dn = jnp.where(rid == tm - 1, ml_ref[0, r0 + tm:r0 + tm + 1, 0:2 * ML_W], dn)
        y = cw[0:1, :] * up + cw[1:2, :] * cur + cw[2:3, :] * dn + cb
        qk_s[r0:r0 + tm, :] = _silu(y)

    o_ref[...] = jnp.zeros_like(o_ref)
    c_s[...] = jnp.zeros_like(c_s)
    n_s[...] = jnp.zeros_like(n_s)
    m_s[...] = jnp.zeros_like(m_s)
    ones_v = jnp.ones((CHUNK, LANES), BF16)
    gate_lane = lax.broadcasted_iota(jnp.int32, (CHUNK, LANES), 1) < 16
    tok = lax.broadcasted_iota(jnp.int32, (CHUNK, LANES), 0)

    def chunk_dir(d, c):
        r0 = pl.multiple_of(c * CHUNK, CHUNK)
        rows = pl.ds(r0, CHUNK)
        qk = qk_s[rows, :]
        qb = qk[:, 0:ML_W].astype(BF16)
        k = qk[:, ML_W:2 * ML_W] * (ML_DH ** -0.5)
        kb = k.astype(BF16)
        vb = ml_ref[0, rows, 2 * ML_W:3 * ML_W].astype(BF16)
        gc = jnp.where(gate_lane, gc_ref[0, rows, :], 0.0)
        gt = gt_ref[0, c]
        spread = spread_ref[d]
        fcols = [d * 8 + 4 + h for h in range(ML_HEADS)]
        icols = [d * 8 + h for h in range(ML_HEADS)]
        last = CHUNK - 1 if d == 0 else 0
        hi, lo = _split(_log_sigmoid(gc))
        b_c = _mm(tri_ref[d], hi) + _mm(tri_ref[d], lo)
        u = pltpu.roll(gc, 4, 1) - b_c
        cmax = u
        for step in (1, 2, 4, 8, 16, 32):
            if d == 0:
                moved = jnp.where(tok >= step, pltpu.roll(cmax, step, 0), -jnp.inf)
            else:
                moved = jnp.where(tok < CHUNK - step, pltpu.roll(cmax, CHUNK - step, 0), -jnp.inf)
            cmax = jnp.maximum(cmax, moved)
        m = m_s[d][0:1, :]
        inter = b_c + m
        m_t = jnp.maximum(inter, b_c + cmax)
        w_st = jnp.exp(inter - m_t)
        b_r = _mm_split(_log_sigmoid(gt), tri_ref[1 - d])
        u_row = jnp.concatenate([gt[i:i + 1, :] - b_r[f:f + 1, :] for i, f in zip(icols, fcols)], axis=1)
        w_tok = jnp.exp(_mm_split(b_c - m_t, spread) + u_row + neg_ref[d])
        s = _nt(qb, _stack_heads(kb, sameb_ref[...])) * w_tok
        intra = _mm(s.astype(BF16), _stack_heads(vb, sameb_ref[...]))
        den = _mm_split(s, gather_ref[d]) + w_st * _mm(qb, n_s[d].astype(BF16))
        r = 1.0 / jnp.maximum(jnp.abs(den), jnp.exp(-m_t))
        c_mat = c_s[d]
        out = intra * _mm_split(r, spread) + _mm(qb, c_mat.astype(BF16)) * _mm_split(w_st * r, spread)
        o_ref[0, rows, :] += out
        bend = b_c[last:last + 1, :]
        m_new = jnp.maximum(bend + m, bend + cmax[last:last + 1, :])
        kw = (k * _mm_split(jnp.exp(bend + u - m_new), spread)).astype(BF16)
        w_old = jnp.broadcast_to(jnp.exp(bend + m - m_new), (8, LANES))
        c_s[d] = _mm_split(w_old, spread)[0:1, :] * c_mat + _tn(kw, vb) * samef_ref[...]
        n_s[d] = w_old[0:1, :] * n_s[d] + _tn(kw, ones_v) * gatherf_ref[d]
        m_s[d] = jnp.broadcast_to(m_new, (8, LANES))

    def body(i, carry):
        chunk_dir(0, i)
        chunk_dir(1, _dir1_chunk(i, n_ctx_chunks, n_chunks))
        return carry

    lax.fori_loop(0, n_chunks, body, 0)


def _mlstm(oml, omla, gates_t, conv_w, conv_b, ctx_len, tm):
    bsz, t, _ = oml.shape
    n_chunks = t // CHUNK
    kern = functools.partial(_mlstm_kernel, t=t, ctx_len=ctx_len, tm=tm)
    consts = _mlstm_tables() + _head_tables()
    return pl.pallas_call(
        kern,
        grid=(bsz,),
        in_specs=[pl.BlockSpec((1, t, 768), lambda i: (i, 0, 0)),
                  pl.BlockSpec((1, t, LANES), lambda i: (i, 0, 3)),
                  pl.BlockSpec((1, n_chunks, 16, CHUNK), lambda i: (i, 0, 0, 0)),
                  pl.BlockSpec((3, 2 * ML_W), lambda i: (0, 0)),
                  pl.BlockSpec((1, 2 * ML_W), lambda i: (0, 0))] + [_const_spec(a) for a in consts],
        out_specs=pl.BlockSpec((1, t, ML_W), lambda i: (i, 0, 0)),
        out_shape=jax.ShapeDtypeStruct((bsz, t, ML_W), F32),
        scratch_shapes=[pltpu.VMEM((t, 2 * ML_W), F32),
                        pltpu.VMEM((2, ML_W, ML_W), F32),
                        pltpu.VMEM((2, ML_W, LANES), F32),
                        pltpu.VMEM((2, 8, LANES), F32)],
        compiler_params=_params(("arbitrary",)),
        name="mlstm",
    )(oml, omla, gates_t, conv_w, conv_b[None, :], *consts)


def _hgrn_tables():
    sums = np.zeros((2, (N_MM_LEVELS + 1) * CHUNK, CHUNK), np.float32)
    pairs = np.zeros((2, N_LEVELS + 1, CHUNK, CHUNK), np.float32)
    sign = np.zeros((2, N_LEVELS - N_MM_LEVELS, CHUNK, 1), np.float32)
    for d in range(2):
        p = (lambda a: a) if d == 0 else (lambda a: CHUNK - 1 - a)
        for t in range(CHUNK):
            pairs[d, 0, p(t), p(t)] = 1.0
            for s in range(t + 1):
                sums[d, p(t), p(s)] = 1.0
        for l in range(N_LEVELS):
            w = 1 << l
            for t in range(CHUNK):
                ref = t - (t % (2 * w)) + w - 1
                if l < N_MM_LEVELS:
                    lo, hi = (ref + 1, t) if t > ref else (t + 1, ref)
                    for s in range(lo, hi + 1):
                        sums[d, (l + 1) * CHUNK + p(t), p(s)] = 1.0
                else:
                    sign[d, l - N_MM_LEVELS, p(t), 0] = 1.0 if t > ref else -1.0
                if t > ref:
                    for s in range(ref - w + 1, ref + 1):
                        pairs[d, l + 1, p(t), p(s)] = 1.0
    pairs = np.tile(pairs, (1, 1, 1, HG_HEADS))
    sign = np.broadcast_to(sign, sign.shape[:3] + (HG_W,))
    return jnp.asarray(sums, BF16), jnp.asarray(sign), jnp.asarray(pairs)


def _hgrn_ref_rows(cum, d, l):
    w = 1 << l
    parts = []
    for base in range(0, CHUNK, 2 * w):
        r = base + w - 1 if d == 0 else base + w
        parts.append(jnp.broadcast_to(cum[r:r + 1, :], (2 * w, cum.shape[1])))
    return parts[0] if len(parts) == 1 else jnp.concatenate(parts, axis=0)


def _hgrn_kernel(hg_ref, lbl_ref, sum_ref, sign_ref, pair_ref, sameb_ref, samef_ref, o_ref, st_s,
                 *, t, ctx_len, layer):
    n_chunks = t // CHUNK
    n_ctx_chunks = ctx_len // CHUNK
    w_all = HG_W

    if layer > 0:
        logits = lbl_ref[...]
        n_layers = logits.shape[0]
        mx = logits[0]
        for l in range(1, n_layers):
            mx = jnp.maximum(mx, logits[l])
        ex = [jnp.exp(logits[l] - mx) for l in range(n_layers)]
        tot = ex[0]
        for l in range(1, n_layers):
            tot = tot + ex[l]
        low_all = ex[1] / tot
        for l in range(2, layer + 1):
            low_all = low_all + ex[l] / tot
        log_low = jnp.log(low_all)
        log_1m_low = jnp.log1p(-low_all)

    o_ref[...] = jnp.zeros_like(o_ref)
    st_s[...] = jnp.zeros_like(st_s)

    def chunk_dir(d, c):
        r0 = pl.multiple_of(c * CHUNK, CHUNK)
        rows = pl.ds(r0, CHUNK)
        q = _silu(hg_ref[0, rows, 0:w_all])
        v = hg_ref[0, rows, w_all:2 * w_all]
        pre = hg_ref[0, rows, (2 + d) * w_all:(3 + d) * w_all]
        ls = _log_sigmoid(pre)
        if layer == 0:
            log_f = ls
            key = jnp.exp(ls - pre)
        else:
            a = log_low[d:d + 1, :]
            b = log_1m_low[d:d + 1, :] + ls
            log_f = jnp.maximum(a, b) + jnp.log1p(jnp.exp(-jnp.abs(a - b)))
            key = (1.0 - low_all[d:d + 1, :]) * jnp.exp(ls - pre)
        hi, lo = _split(log_f)
        xb = _mm(sum_ref[d], hi) + _mm(sum_ref[d], lo)
        cum = xb[0:CHUNK]
        last = CHUNK - 1 if d == 0 else 0
        bend = cum[last:last + 1, :]
        kb = key.astype(BF16)
        vb = v.astype(BF16)
        same_b = sameb_ref[...]
        amat = _nt(q.astype(BF16), _stack_heads(kb, same_b)) * pair_ref[d, 0]
        for l in range(N_LEVELS):
            if l < N_MM_LEVELS:
                expo = xb[(l + 1) * CHUNK:(l + 2) * CHUNK]
            else:
                expo = (cum - _hgrn_ref_rows(cum, d, l)) * sign_ref[d, l - N_MM_LEVELS]
            fac = jnp.exp(expo)
            qt = (q * fac).astype(BF16)
            kt = (key * fac).astype(BF16)
            amat = amat + _nt(qt, _stack_heads(kt, same_b)) * pair_ref[d, l + 1]
        intra = _mm(amat.astype(BF16), _stack_heads(vb, same_b))
        st = st_s[d]
        inter = _nt((q * jnp.exp(cum)).astype(BF16), st.astype(BF16))
        o_ref[0, rows, :] += intra + inter
        kdec = (key * jnp.exp(bend - cum)).astype(BF16)
        st_s[d] = jnp.exp(bend) * st + _tn(vb, kdec) * samef_ref[...]

    def body(i, carry):
        chunk_dir(0, i)
        chunk_dir(1, _dir1_chunk(i, n_ctx_chunks, n_chunks))
        return carry

    lax.fori_loop(0, n_chunks, body, 0)


def _hgrn(ohg, lb_logits, ctx_len, layer):
    bsz, t, _ = ohg.shape
    kern = functools.partial(_hgrn_kernel, t=t, ctx_len=ctx_len, layer=layer)
    consts = (lb_logits,) + _hgrn_tables() + _head_tables()
    return pl.pallas_call(
        kern,
        grid=(bsz,),
        in_specs=[pl.BlockSpec((1, t, 4 * HG_W), lambda i: (i, 0, 0))] + [_const_spec(a) for a in consts],
        out_specs=pl.BlockSpec((1, t, HG_W), lambda i: (i, 0, 0)),
        out_shape=jax.ShapeDtypeStruct((bsz, t, HG_W), F32),
        scratch_shapes=[pltpu.VMEM((2, HG_W, HG_W), F32)],
        compiler_params=_params(("arbitrary",)),
        name="hgrn",
    )(ohg, *consts)


def _mla_prep_kernel(x_ref, gq_ref, gkv_ref, wq_ref, wk_ref, wv_ref, gqq_ref, gkk_ref,
                     cos_ref, sa_ref, sb_ref, q_ref, k_ref, v_ref):
    x = x_ref[0]
    cq = _rms_rows(x[:, 0:MLA_Q_RANK], gq_ref[...]).astype(BF16)
    ckv = _rms_rows(x[:, MLA_Q_RANK:MLA_Q_RANK + MLA_KV_RANK], gkv_ref[...]).astype(BF16)
    misc = x[:, MLA_Q_RANK + MLA_KV_RANK:]
    lane = lax.broadcasted_iota(jnp.int32, misc.shape, 1)
    k_rope = jnp.where((lane >= MLA_NOPE) & (lane < MLA_DQK), misc, 0.0)
    q_raw = _mm(cq, wq_ref[...])
    k_raw = _mm(ckv, wk_ref[...])
    v_ref[0] = _mm(ckv, wv_ref[...]).astype(BF16)
    cos, sa, sb = cos_ref[...], sa_ref[...], sb_ref[...]

    def norm_rope(xh, g):
        ms = jnp.sum(xh * xh, axis=-1, keepdims=True) * (1.0 / MLA_DQK)
        xn = xh * lax.rsqrt(ms + EPS) * g
        return xn * cos + pltpu.roll(xn, HEAD_PAD - 8, 1) * sa + pltpu.roll(xn, 8, 1) * sb

    for h in range(MLA_HEADS):
        sl = slice(h * HEAD_PAD, (h + 1) * HEAD_PAD)
        q_ref[0, :, sl] = (norm_rope(q_raw[:, sl], gqq_ref[...]) * (MLA_DQK ** -0.5)).astype(BF16)
        k_ref[0, :, sl] = norm_rope(k_raw[:, sl] + k_rope, gkk_ref[...]).astype(BF16)


def _mla_prep(omla, gq, gkv, w_uq, w_ukv, g_qq, g_kk, cos, sa, sb, tm):
    bsz, t, _ = omla.shape
    hw = MLA_HEADS * HEAD_PAD
    wq = w_uq.reshape(MLA_Q_RANK, MLA_HEADS, MLA_DQK)
    wq = jnp.pad(wq, ((0, 0), (0, 0), (0, HEAD_PAD - MLA_DQK))).reshape(MLA_Q_RANK, hw).astype(BF16)
    wkv = w_ukv.reshape(MLA_KV_RANK, MLA_HEADS, MLA_NOPE + MLA_DV)
    wk = jnp.pad(wkv[:, :, :MLA_NOPE], ((0, 0), (0, 0), (0, HEAD_PAD - MLA_NOPE))).reshape(MLA_KV_RANK, hw).astype(BF16)
    wv = wkv[:, :, MLA_NOPE:].reshape(MLA_KV_RANK, MLA_HEADS * MLA_DV).astype(BF16)
    padg = lambda g: jnp.pad(g, (0, HEAD_PAD - MLA_DQK))[None, :]
    full = lambda shape: pl.BlockSpec(shape, lambda i, j: (0,) * len(shape))
    rope_spec = pl.BlockSpec((tm, HEAD_PAD), lambda i, j: (j, 0))
    return pl.pallas_call(
        _mla_prep_kernel,
        grid=(bsz, t // tm),
        in_specs=[pl.BlockSpec((1, tm, 512), lambda i, j: (i, j, 0)),
                  full((1, MLA_Q_RANK)), full((1, MLA_KV_RANK)),
                  full((MLA_Q_RANK, hw)), full((MLA_KV_RANK, hw)), full((MLA_KV_RANK, MLA_HEADS * MLA_DV)),
                  full((1, HEAD_PAD)), full((1, HEAD_PAD)), rope_spec, rope_spec, rope_spec],
        out_specs=[pl.BlockSpec((1, tm, hw), lambda i, j: (i, j, 0)),
                   pl.BlockSpec((1, tm, hw), lambda i, j: (i, j, 0)),
                   pl.BlockSpec((1, tm, MLA_HEADS * MLA_DV), lambda i, j: (i, j, 0))],
        out_shape=[jax.ShapeDtypeStruct((bsz, t, hw), BF16),
                   jax.ShapeDtypeStruct((bsz, t, hw), BF16),
                   jax.ShapeDtypeStruct((bsz, t, MLA_HEADS * MLA_DV), BF16)],
        compiler_params=_params(("arbitrary", "arbitrary")),
        name="mla_prep",
    )(omla, gq[None, :], gkv[None, :], wq, wk, wv, padg(g_qq), padg(g_kk), cos, sa, sb)


def _attn_kernel(q_ref, k_ref, v_ref, o_ref, *, ctx_len, nct, j0):
    is_ctx = pl.program_id(2) + j0 < nct

    def attend(n_keys):
        for hh in range(2):
            q = q_ref[0, :, hh * HEAD_PAD:(hh + 1) * HEAD_PAD]
            k = k_ref[0, 0:n_keys, hh * HEAD_PAD:(hh + 1) * HEAD_PAD]
            v = v_ref[0, 0:n_keys, hh * MLA_DV:(hh + 1) * MLA_DV]
            s = _nt(q, k)
            p = jnp.exp(s - jnp.max(s, axis=-1, keepdims=True))
            o = _mm(p.astype(BF16), v) / jnp.sum(p, axis=-1, keepdims=True)
            o_ref[0, :, hh * MLA_DV:(hh + 1) * MLA_DV] = o

    @pl.when(is_ctx)
    def _():
        attend(ctx_len)

    @pl.when(jnp.logical_not(is_ctx))
    def _():
        attend(k_ref.shape[1])


def _attention(q, k, v, ctx_len, tm, j0):
    bsz, t, _ = q.shape
    nct = ctx_len // tm
    nq = t // tm - j0
    kern = functools.partial(_attn_kernel, ctx_len=ctx_len, nct=nct, j0=j0)
    return pl.pallas_call(
        kern,
        grid=(bsz, MLA_HEADS // 2, nq),
        in_specs=[pl.BlockSpec((1, tm, 2 * HEAD_PAD), lambda i, h, j: (i, j + j0, h)),
                  pl.BlockSpec((1, t, 2 * HEAD_PAD), lambda i, h, j: (i, 0, h)),
                  pl.BlockSpec((1, t, 2 * MLA_DV), lambda i, h, j: (i, 0, h))],
        out_specs=pl.BlockSpec((1, tm, 2 * MLA_DV), lambda i, h, j: (i, j, h)),
        out_shape=jax.ShapeDtypeStruct((bsz, nq * tm, MLA_HEADS * MLA_DV), F32),
        compiler_params=_params(("arbitrary", "arbitrary", "arbitrary")),
        name="attention",
    )(q, k, v)


def _merge_kernel(ml_ref, hg_ref, at_ref, gate_ref, x_ref, mod_ref, gml_ref, ghg_ref, wo_ref,
                  g2_ref, rwt_ref, same_ref, x1_ref, h2_ref, aff_ref):
    head_mean = same_ref[...] * (1.0 / ML_DH)

    def head_norm(a, g):
        hi, lo = _split(a * a)
        ms = _mm(hi, head_mean) + _mm(lo, head_mean)
        return a * lax.rsqrt(ms + EPS) * g

    gates = gate_ref[0]
    y_ml = head_norm(ml_ref[0], gml_ref[...]) * jax.nn.sigmoid(gates[:, 0:ML_W])
    y_hg = head_norm(hg_ref[0], ghg_ref[...]) * _silu(gates[:, ML_W:ML_W + HG_W])
    y = (_mm(y_ml.astype(BF16), wo_ref[0:ML_W, :])
         + _mm(y_hg.astype(BF16), wo_ref[ML_W:ML_W + HG_W, :])
         + _mm(at_ref[0].astype(BF16), wo_ref[ML_W + HG_W:, :]))
    x1 = x_ref[0] + mod_ref[0, 0, 2:3, :] * y
    x1_ref[0] = x1
    h2 = _rms_rows(x1, g2_ref[...]) * (1.0 + mod_ref[0, 0, 4:5, :]) + mod_ref[0, 0, 3:4, :]
    h2_ref[0] = h2.astype(BF16)
    logits = _nt(rwt_ref[...], h2, precision=HIGHEST)
    e = jnp.exp(logits - jnp.max(logits, axis=0, keepdims=True))
    aff_ref[0] = e / jnp.sum(e, axis=0, keepdims=True)


def _merge(ml, hg, at, ogate, xs, modtok, g_ml, g_hg, w_out, g2, router_w, tm, j0, n_tiles, at_j0):
    bsz, _, d = xs.shape
    n = n_tiles * tm
    sel = 0 if j0 == 0 and n_tiles * tm < xs.shape[1] else 1
    tok = lambda width, off: pl.BlockSpec((1, tm, width), lambda i, j: (i, j + off, 0))
    full = lambda shape: pl.BlockSpec(shape, lambda i, j: (0,) * len(shape))
    same_b, _ = _head_tables()
    return pl.pallas_call(
        _merge_kernel,
        grid=(bsz, n_tiles),
        in_specs=[tok(ML_W, j0), tok(HG_W, j0), tok(512, j0 - at_j0), tok(512, j0), tok(d, j0),
                  pl.BlockSpec((1, 1, 6, d), lambda i, j: (i, sel, 0, 0)),
                  full((1, ML_W)), full((1, HG_W)), full((d, d)), full((1, d)), full((N_EXPERTS, d)),
                  _const_spec(same_b)],
        out_specs=[tok(d, 0), tok(d, 0), pl.BlockSpec((1, N_EXPERTS, tm), lambda i, j: (i, 0, j))],
        out_shape=[jax.ShapeDtypeStruct((bsz, n, d), F32),
                   jax.ShapeDtypeStruct((bsz, n, d), BF16),
                   jax.ShapeDtypeStruct((bsz, N_EXPERTS, n), F32)],
        compiler_params=_params(("arbitrary", "arbitrary")),
        name="merge",
    )(ml, hg, at, ogate, xs, modtok, g_ml[None, :], g_hg[None, :], w_out.astype(BF16), g2[None, :],
      router_w.T, same_b)


def _route_kernel(aff_ref, rank_ref, *, cap, n):
    aff = aff_ref[0]

    def search(i, thr):
        cand = thr | (jnp.int32(1) << (30 - i))
        cnt = jnp.sum(jnp.where(aff >= pltpu.bitcast(cand, F32), 1, 0), axis=-1, keepdims=True)
        return jnp.where(cnt >= cap, cand, thr)

    thr = lax.fori_loop(0, 31, search, jnp.zeros((N_EXPERTS, 1), jnp.int32))
    above = aff >= pltpu.bitcast(thr + 1, F32)
    tied = jnp.logical_and(aff >= pltpu.bitcast(thr, F32), jnp.logical_not(above))
    need = cap - jnp.sum(jnp.where(above, 1, 0), axis=-1, keepdims=True)

    tl = min(n, 256)
    r = lax.broadcasted_iota(jnp.int32, (tl, tl), 0)
    c = lax.broadcasted_iota(jnp.int32, (tl, tl), 1)
    before = jnp.where(r < c, 1.0, 0.0).astype(BF16)

    def excl_cumsum(mask):
        parts, carry = [], jnp.zeros((N_EXPERTS, 1), F32)
        for j in range(n // tl):
            m = jnp.where(mask[:, j * tl:(j + 1) * tl], 1.0, 0.0)
            parts.append(_mm(m.astype(BF16), before) + carry)
            carry = carry + jnp.sum(m, axis=-1, keepdims=True)
        return jnp.concatenate(parts, axis=-1).astype(jnp.int32)

    keep = jnp.logical_or(above, jnp.logical_and(tied, excl_cumsum(tied) < need))
    rank_ref[0] = jnp.where(keep, excl_cumsum(keep), -1)


def _route(aff_t, cap):
    bsz, e, n = aff_t.shape
    kern = functools.partial(_route_kernel, cap=cap, n=n)
    return pl.pallas_call(
        kern,
        grid=(bsz,),
        in_specs=[pl.BlockSpec((1, e, n), lambda i: (i, 0, 0))],
        out_specs=pl.BlockSpec((1, e, n), lambda i: (i, 0, 0)),
        out_shape=jax.ShapeDtypeStruct((bsz, e, n), jnp.int32),
        compiler_params=_params(("arbitrary",)),
        name="route",
    )(aff_t)


def _ffn_kernel(rank_ref, aff_ref, h_ref, x1_ref, g2_ref, wg_ref, wu_ref, wd_ref, o_ref, *, cap, n, bb):
    e = pl.program_id(1)
    slot = lax.broadcasted_iota(jnp.int32, (cap, n), 0)
    picks, gates, xs = [], [], []
    for b in range(bb):
        chosen = rank_ref[b, 0] == slot
        pick = jnp.where(chosen, 1.0, 0.0).astype(BF16)
        picks.append(pick)
        gates.append(jnp.sum(jnp.where(chosen, aff_ref[b, 0], 0.0), axis=-1, keepdims=True))
        xs.append(_mm(pick, h_ref[b]).astype(BF16))
    xs = jnp.concatenate(xs, axis=0)
    hid = (_silu(_mm(xs, wg_ref[0])) * _mm(xs, wu_ref[0])).astype(BF16)
    out = _mm(hid, wd_ref[0])

    @pl.when(e == 0)
    def _():
        o_ref[...] = jnp.zeros_like(o_ref)

    d = out.shape[1]
    for b in range(bb):
        out_b = (out[b * cap:(b + 1) * cap] * gates[b]).astype(BF16)
        for j in range(d // 256):
            o_ref[b, :, j * 256:(j + 1) * 256] += _tn(picks[b], out_b[:, j * 256:(j + 1) * 256])

    @pl.when(e == pl.num_programs(1) - 1)
    def _():
        o_ref[...] = x1_ref[...] + g2_ref[...] * o_ref[...]


def _ffn(rank, aff_t, h2, x1, g2mod, wg, wu, wd, cap, bb):
    bsz, n, d = h2.shape
    n_exp = wg.shape[0]
    kern = functools.partial(_ffn_kernel, cap=cap, n=n, bb=bb)
    once = dict(pipeline_mode=pl.Buffered(1))
    return pl.pallas_call(
        kern,
        grid=(bsz // bb, n_exp),
        in_specs=[pl.BlockSpec((bb, 1, 1, n), lambda i, e: (i, e, 0, 0)),
                  pl.BlockSpec((bb, 1, 1, n), lambda i, e: (i, e, 0, 0)),
                  pl.BlockSpec((bb, n, d), lambda i, e: (i, 0, 0), **once),
                  pl.BlockSpec((bb, n, d), lambda i, e: (i, 0, 0), **once),
                  pl.BlockSpec((bb, 1, d), lambda i, e: (i, 0, 0)),
                  pl.BlockSpec((1, d, d), lambda i, e: (e, 0, 0)),
                  pl.BlockSpec((1, d, d), lambda i, e: (e, 0, 0)),
                  pl.BlockSpec((1, d, d), lambda i, e: (e, 0, 0))],
        out_specs=pl.BlockSpec((bb, n, d), lambda i, e: (i, 0, 0)),
        out_shape=jax.ShapeDtypeStruct((bsz, n, d), F32),
        compiler_params=_params(("arbitrary", "arbitrary")),
        name="expert_ffn",
    )(rank.reshape(bsz, n_exp, 1, n), aff_t.reshape(bsz, n_exp, 1, n), h2, x1, g2mod, wg, wu, wd)


def _rope_tables(ctx_len, seq):
    half = MLA_ROPE // 2
    inv = ROPE_THETA ** (-jnp.arange(0, half, 2, dtype=F32) / half)
    rows = seq // GRID_W
    row_pos = jnp.repeat(jnp.arange(rows), GRID_W).astype(F32)
    col_pos = jnp.broadcast_to(jnp.arange(GRID_W), (rows, GRID_W)).reshape(-1).astype(F32)

    def cs(pos):
        ang = pos[:, None] * inv[None, :]
        ang = jnp.concatenate([ang, ang], axis=-1)
        return jnp.cos(ang), jnp.sin(ang)

    cos_r, sin_r = cs(row_pos)
    cos_c, sin_c = cs(col_pos)
    first = jnp.arange(half) < half // 2
    zeros = lambda n: jnp.zeros((seq, n), F32)
    ones = lambda n: jnp.ones((seq, n), F32)
    tail = HEAD_PAD - MLA_DQK
    cos = jnp.concatenate([ones(MLA_NOPE), cos_r, cos_c, ones(tail)], axis=-1)
    sa = jnp.concatenate([zeros(MLA_NOPE), jnp.where(first, -sin_r, 0.0), jnp.where(first, -sin_c, 0.0),
                          zeros(tail)], axis=-1)
    sb = jnp.concatenate([zeros(MLA_NOPE), jnp.where(first, 0.0, sin_r), jnp.where(first, 0.0, sin_c),
                          zeros(tail)], axis=-1)
    ident = lambda a, fill: jnp.concatenate([jnp.full((ctx_len, HEAD_PAD), fill, F32), a], axis=0)
    return ident(cos, 1.0), ident(sa, 0.0), ident(sb, 0.0)


def kernel(x, c, ctx, c_ctx, ada_w, ada_b, norm1_g, norm2_g, w_in, b_in, ml_conv_w, ml_conv_b, ml_norm_g, hg_lb_logits, hg_norm_g, mla_q_norm_g, mla_w_uq, mla_kv_norm_g, mla_w_ukv, mla_q_qk_g, mla_k_qk_g, w_out, router_w, ex_w_gate, ex_w_up, ex_w_down):
    bsz, seq, d = x.shape
    ctx_len = ctx.shape[1]
    depth = ada_w.shape[0]
    t = ctx_len + seq
    tm = 256 if ctx_len % 256 == 0 else 128
    assert ctx_len % tm == 0 and seq % tm == 0 and ctx_len % CHUNK == 0 and seq % GRID_W == 0
    nct = ctx_len // tm
    n_chunks = t // CHUNK

    rows = -(-(bsz + 1) // 8) * 8
    cc = jnp.concatenate([c, c_ctx[None, :], jnp.zeros((rows - bsz - 1, d), F32)], axis=0)
    mod = _modulation(cc, ada_w, ada_b)
    cos, sa, sb = _rope_tables(ctx_len, seq)
    bb_ctx = max(bb for bb in (8, 4, 2, 1) if bsz % bb == 0)

    xs = jnp.concatenate([ctx, x], axis=1)
    lat = None
    for layer in range(depth):
        need_ctx = layer < depth - 1
        m = mod[layer]
        modtok = jnp.stack([jnp.broadcast_to(m[bsz], (bsz, 6 * d)), m[:bsz]], axis=1).reshape(bsz, 2, 6, d)
        oml, ohg, ogate, omla, ogt = _in_proj(xs, modtok, norm1_g[layer], w_in[layer], b_in[layer], tm, nct)
        gates_t = ogt.reshape(bsz, 16, n_chunks, CHUNK).transpose(0, 2, 1, 3)
        ml = _mlstm(oml, omla, gates_t, ml_conv_w[layer], ml_conv_b[layer], ctx_len, tm)
        hg = _hgrn(ohg, hg_lb_logits, ctx_len, layer)
        q, k, v = _mla_prep(omla, mla_q_norm_g[layer], mla_kv_norm_g[layer], mla_w_uq[layer], mla_w_ukv[layer],
                            mla_q_qk_g[layer], mla_k_qk_g[layer], cos, sa, sb, tm)
        at_j0 = 0 if need_ctx else nct
        at = _attention(q, k, v, ctx_len, tm, at_j0)
        wg, wu, wd = (w[layer].astype(BF16) for w in (ex_w_gate, ex_w_up, ex_w_down))
        g2mod = modtok[:, :, 5:6, :]

        def post(j0, n_tiles, g2, bb):
            x1, h2, aff = _merge(ml, hg, at, ogate, xs, modtok, ml_norm_g[layer], hg_norm_g[layer], w_out[layer],
                                 norm2_g[layer], router_w[layer], tm, j0, n_tiles, at_j0)
            n = n_tiles * tm
            cap = EC_CAPACITY * n // N_EXPERTS
            rank = _route(aff, cap)
            return _ffn(rank, aff, h2, x1, g2, wg, wu, wd, cap, bb)

        lat = post(nct, seq // tm, g2mod[:, 1], 1)
        if need_ctx:
            xs = jnp.concatenate([post(0, nct, g2mod[:, 0], bb_ctx), lat], axis=1)
    return lat
```

```python
import functools

import numpy as np
import jax
import jax.numpy as jnp
from jax import lax
from jax.experimental import pallas as pl
from jax.experimental.pallas import tpu as pltpu

F32 = jnp.float32
BF16 = jnp.bfloat16
HIGHEST = lax.Precision.HIGHEST

D_MODEL = 1024
GRID_W = 64
ML_HEADS = 4
ML_DH = 64
ML_W = 256
HG_HEADS = 4
HG_W = 256
MLA_HEADS = 8
MLA_NOPE = 64
MLA_ROPE = 32
MLA_DQK = 96
MLA_DV = 64
MLA_Q_RANK = 256
MLA_KV_RANK = 128
N_EXPERTS = 16
EC_CAPACITY = 2
CHUNK = 64
ROPE_THETA = 10000.0
EPS = 1e-6
HEAD_PAD = 128
LANES = 128
N_LEVELS = 6
N_MM_LEVELS = 3
SLAB = ML_HEADS * CHUNK
VMEM_LIMIT = 52 * 1024 * 1024

_O_QK, _O_V, _O_OG, _O_GATES, _O_HQ, _O_HI, _O_HGATE, _O_HF, _O_CQ, _O_CKV, _O_KR, _O_END = (
    0, 512, 768, 1024, 1040, 1296, 1552, 1808, 2320, 2576, 2704, 2736)


def _nt(a, b, **kw):
    return lax.dot_general(a, b, (((1,), (1,)), ((), ())), preferred_element_type=F32, **kw)


def _tn(a, b, **kw):
    return lax.dot_general(a, b, (((0,), (0,)), ((), ())), preferred_element_type=F32, **kw)


def _mm(a, b, **kw):
    return jnp.dot(a, b, preferred_element_type=F32, **kw)


def _split(x):
    hi = x.astype(BF16)
    return hi, (x - hi.astype(F32)).astype(BF16)


def _silu(x):
    return x * jax.nn.sigmoid(x)


def _log_sigmoid(x):
    return jnp.minimum(x, 0.0) - jnp.log1p(jnp.exp(-jnp.abs(x)))


def _rms_rows(x, g):
    return x * lax.rsqrt(jnp.mean(x * x, axis=-1, keepdims=True) + EPS) * g


def _params(sem=None):
    return pltpu.CompilerParams(dimension_semantics=sem, vmem_limit_bytes=VMEM_LIMIT)


def _const_spec(a):
    return pl.BlockSpec(a.shape, lambda *_: (0,) * a.ndim)


def _mod_kernel(c_ref, w_ref, b_ref, o_ref):
    s = _silu(c_ref[...]).astype(BF16)
    o_ref[0] = _mm(s, w_ref[0].astype(BF16)) + b_ref[0]


def _modulation(cc, ada_w, ada_b):
    n_layers, d, n6 = ada_w.shape
    rows = cc.shape[0]
    tn = 1024
    return pl.pallas_call(
        _mod_kernel,
        grid=(n_layers, n6 // tn),
        in_specs=[pl.BlockSpec((rows, d), lambda l, j: (0, 0)),
                  pl.BlockSpec((1, d, tn), lambda l, j: (l, 0, j)),
                  pl.BlockSpec((1, 1, tn), lambda l, j: (l, 0, j))],
        out_specs=pl.BlockSpec((1, rows, tn), lambda l, j: (l, 0, j)),
        out_shape=jax.ShapeDtypeStruct((n_layers, rows, n6), F32),
        compiler_params=_params(("arbitrary", "arbitrary")),
        name="modulation",
    )(cc, ada_w, ada_b.reshape(n_layers, 1, n6))


def _in_kernel(x_ref, mod_ref, g_ref, w_ref, b_ref, wgt_ref, bgt_ref,
               oml_ref, ohg_ref, ogate_ref, omla_ref, ogt_ref):
    x = x_ref[0]
    sh = mod_ref[0, 0, 0:1, :]
    sc = mod_ref[0, 0, 1:2, :]
    h = (_rms_rows(x, g_ref[...]) * (1.0 + sc) + sh).astype(BF16)
    oml_ref[0] = _mm(h, w_ref[:, 0:768]) + b_ref[:, 0:768]
    ohg_ref[0] = _mm(h, w_ref[:, 768:1792]) + b_ref[:, 768:1792]
    ogate_ref[0] = _mm(h, w_ref[:, 1792:2304]) + b_ref[:, 1792:2304]
    omla_ref[0] = _mm(h, w_ref[:, 2304:2816]) + b_ref[:, 2304:2816]
    ogt_ref[0] = _nt(wgt_ref[...], h) + bgt_ref[...]


def _in_proj(xs, modtok, g1, w_in, b_in, tm, nct):
    bsz, t, d = xs.shape
    zeros = lambda n: jnp.zeros((d, n), F32)
    w = jnp.concatenate([
        w_in[:, _O_QK:_O_OG], w_in[:, _O_HQ:_O_HGATE], w_in[:, _O_HF:_O_CQ],
        w_in[:, _O_OG:_O_GATES], w_in[:, _O_HGATE:_O_HF], w_in[:, _O_CQ:_O_KR],
        w_in[:, _O_GATES:_O_HQ], zeros(48), w_in[:, _O_KR:_O_END], zeros(32)], axis=1).astype(BF16)
    zb = lambda n: jnp.zeros((n,), F32)
    b = jnp.concatenate([
        b_in[_O_QK:_O_OG], b_in[_O_HQ:_O_HGATE], b_in[_O_HF:_O_CQ],
        b_in[_O_OG:_O_GATES], b_in[_O_HGATE:_O_HF], b_in[_O_CQ:_O_KR],
        b_in[_O_GATES:_O_HQ], zb(48), b_in[_O_KR:_O_END], zb(32)])[None, :]
    wgt = w_in[:, _O_GATES:_O_HQ].T.astype(BF16)
    bgt = b_in[_O_GATES:_O_HQ][:, None]
    nw = w.shape[1]
    tok = lambda width: pl.BlockSpec((1, tm, width), lambda i, j: (i, j, 0))
    full = lambda shape: pl.BlockSpec(shape, lambda i, j: (0,) * len(shape))
    return pl.pallas_call(
        _in_kernel,
        grid=(bsz, t // tm),
        in_specs=[tok(d),
                  pl.BlockSpec((1, 1, 6, d), lambda i, j: (i, jnp.where(j >= nct, 1, 0), 0, 0)),
                  full((1, d)), full((d, nw)), full((1, nw)), full((16, d)), full((16, 1))],
        out_specs=[tok(768), tok(1024), tok(512), tok(512),
                   pl.BlockSpec((1, 16, tm), lambda i, j: (i, 0, j))],
        out_shape=[jax.ShapeDtypeStruct((bsz, t, 768), F32),
                   jax.ShapeDtypeStruct((bsz, t, 1024), F32),
                   jax.ShapeDtypeStruct((bsz, t, 512), F32),
                   jax.ShapeDtypeStruct((bsz, t, 512), F32),
                   jax.ShapeDtypeStruct((bsz, 16, t), F32)],
        compiler_params=_params(("arbitrary", "arbitrary")),
        name="in_proj",
    )(xs, modtok, g1[None, :], w, b, wgt, bgt)


def _dir1_chunk(i, n_ctx_chunks, n_chunks):
    return jnp.where(i < n_ctx_chunks, n_ctx_chunks - 1 - i, n_chunks - 1 - (i - n_ctx_chunks))


def _head_tables():
    same = (np.arange(SLAB)[:, None] // CHUNK == np.arange(SLAB)[None, :] // CHUNK).astype(np.float32)
    return jnp.asarray(same, BF16), jnp.asarray(same, F32)


def _stack_heads(x_bf, same_bf):
    return jnp.concatenate([x_bf] * ML_HEADS, axis=0) * same_bf


class _Chain:
    def __init__(self, **kw):
        self.__dict__.update(kw)


def _samples_per_step(bsz):
    return 2 if bsz % 2 == 0 else 1


def _mm_split(a, b):
    hi, lo = _split(a)
    return _mm(hi, b) + _mm(lo, b)


def _mlstm_tables():
    t = np.arange(CHUNK)
    lower = (t[:, None] >= t[None, :]).astype(np.float32)
    tri = np.stack([lower, lower.T])
    neg = np.stack([np.tile(np.where(m > 0, 0.0, -np.inf), (1, ML_HEADS)) for m in (lower, lower.T)]).astype(np.float32)
    spread = np.zeros((2, LANES, ML_W), np.float32)
    for d in range(2):
        for h in range(ML_HEADS):
            spread[d, d * 8 + 4 + h, h * ML_DH:(h + 1) * ML_DH] = 1.0
    gather = spread.transpose(0, 2, 1)
    return jnp.asarray(tri, BF16), jnp.asarray(neg), jnp.asarray(spread, BF16), jnp.asarray(gather, BF16)


def _mlstm_kernel(ml_ref, gc_ref, gt_ref, cw_ref, cb_ref, tri_ref, neg_ref, spread_ref, gather_ref,
                  sameb_ref, samef_ref, o_ref, qk_s, c_s, n_s, m_s, *, t, ctx_len, tm):
    n_chunks = t // CHUNK
    n_ctx_chunks = ctx_len // CHUNK
    n_samples = ml_ref.shape[0]
    cw = cw_ref[...]
    cb = cb_ref[...]
    rid = lax.broadcasted_iota(jnp.int32, (tm, 2 * ML_W), 0)
    for b in range(n_samples):
        for j in range(t // tm):
            r0 = j * tm
            cur = ml_ref[b, r0:r0 + tm, 0:2 * ML_W]
            up = pltpu.roll(cur, 1, 0)
            if r0 in (0, ctx_len):
                up = jnp.where(rid == 0, 0.0, up)
            else:
                up = jnp.where(rid == 0, ml_ref[b, r0 - 1:r0, 0:2 * ML_W], up)
            dn = pltpu.roll(cur, tm - 1, 0)
            if r0 + tm in (ctx_len, t):
                dn = jnp.where(rid == tm - 1, 0.0, dn)
            else:
                dn = jnp.where(rid == tm - 1, ml_ref[b, r0 + tm:r0 + tm + 1, 0:2 * ML_W], dn)
            y = cw[0:1, :] * up + cw[1:2, :] * cur + cw[2:3, :] * dn + cb
            qk_s[b, r0:r0 + tm, :] = _silu(y)

    o_ref[...] = jnp.zeros_like(o_ref)
    c_s[...] = jnp.zeros_like(c_s)
    n_s[...] = jnp.zeros_like(n_s)
    m_s[...] = jnp.zeros_like(m_s)
    gate_lane = lax.broadcasted_iota(jnp.int32, (CHUNK, LANES), 1) < 16
    tok = lax.broadcasted_iota(jnp.int32, (CHUNK, LANES), 0)

    def load(ch):
        d, b = ch.d, ch.b
        ch.rows = pl.ds(pl.multiple_of(ch.c * CHUNK, CHUNK), CHUNK)
        qk = qk_s[b, ch.rows, :]
        ch.qb = qk[:, 0:ML_W].astype(BF16)
        ch.k = qk[:, ML_W:2 * ML_W] * (ML_DH ** -0.5)
        ch.vb = ml_ref[b, ch.rows, 2 * ML_W:3 * ML_W].astype(BF16)
        ch.gc = jnp.where(gate_lane, gc_ref[b, ch.rows, :], 0.0)
        ch.gt = gt_ref[b, ch.c]
        hi, lo = _split(_log_sigmoid(ch.gc))
        ch.b_c = _mm(tri_ref[d], hi) + _mm(tri_ref[d], lo)
        ch.b_r = _mm_split(_log_sigmoid(ch.gt), tri_ref[1 - d])
        ch.s_raw = _nt(ch.qb, _stack_heads(ch.k.astype(BF16), sameb_ref[...]))
        ch.qn = _mm((qk[:, 0:ML_W] * n_s[b, d][0:1, :]).astype(BF16), gather_ref[d])
        ch.c_mat = c_s[b, d]
        ch.qc = _mm(ch.qb, ch.c_mat.astype(BF16))

    def stabilise(ch):
        d = ch.d
        ch.u = pltpu.roll(ch.gc, 4, 1) - ch.b_c
        cmax = ch.u
        for step in (1, 2, 4, 8, 16, 32):
            if d == 0:
                moved = jnp.where(tok >= step, pltpu.roll(cmax, step, 0), -jnp.inf)
            else:
                moved = jnp.where(tok < CHUNK - step, pltpu.roll(cmax, CHUNK - step, 0), -jnp.inf)
            cmax = jnp.maximum(cmax, moved)
        ch.m = m_s[ch.b, d][0:1, :]
        inter = ch.b_c + ch.m
        ch.m_t = jnp.maximum(inter, ch.b_c + cmax)
        ch.w_st = jnp.exp(inter - ch.m_t)
        last = CHUNK - 1 if d == 0 else 0
        ch.bend = ch.b_c[last:last + 1, :]
        ch.m_new = jnp.maximum(ch.bend + ch.m, ch.bend + cmax[last:last + 1, :])
        fcols = [d * 8 + 4 + h for h in range(ML_HEADS)]
        icols = [d * 8 + h for h in range(ML_HEADS)]
        ch.u_row = jnp.concatenate([ch.gt[i:i + 1, :] - ch.b_r[f:f + 1, :] for i, f in zip(icols, fcols)], axis=1)
        ch.bm_wide = _mm_split(ch.b_c - ch.m_t, spread_ref[d])
        ch.wk_wide = _mm(jnp.exp(ch.bend + ch.u - ch.m_new).astype(BF16), spread_ref[d])
        ch.w_old = jnp.broadcast_to(jnp.exp(ch.bend + ch.m - ch.m_new), (8, LANES))
        ch.wold_wide = _mm_split(ch.w_old, spread_ref[d])

    def weigh(ch):
        d = ch.d
        s = ch.s_raw * jnp.exp(ch.bm_wide + ch.u_row + neg_ref[d])
        sb = s.astype(BF16)
        ch.intra = _mm(sb, _stack_heads(ch.vb, sameb_ref[...]))
        ch.rowsum = _mm(sb, gather_ref[d])
        kw = ch.k * ch.wk_wide
        ch.dc = _tn(kw.astype(BF16), ch.vb)
        ch.dn = jnp.sum(kw, axis=0, keepdims=True)

    def normalise(ch):
        d = ch.d
        den = ch.rowsum + ch.w_st * ch.qn
        r = 1.0 / jnp.maximum(jnp.abs(den), jnp.exp(-ch.m_t))
        ch.r_wide = _mm(r.astype(BF16), spread_ref[d])
        ch.wr_wide = _mm((ch.w_st * r).astype(BF16), spread_ref[d])

    def store(ch):
        b, d = ch.b, ch.d
        o_ref[b, ch.rows, :] += ch.intra * ch.r_wide + ch.qc * ch.wr_wide
        w_old = ch.wold_wide[0:1, :]
        c_s[b, d] = w_old * ch.c_mat + ch.dc * samef_ref[...]
        n_s[b, d] = jnp.broadcast_to(w_old * n_s[b, d][0:1, :] + ch.dn, (8, ML_W))
        m_s[b, d] = jnp.broadcast_to(ch.m_new, (8, LANES))

    def body(i, carry):
        chunk = (i, _dir1_chunk(i, n_ctx_chunks, n_chunks))
        chains = [_Chain(b=b, d=d, c=chunk[d]) for b in range(n_samples) for d in range(2)]
        for stage in (load, stabilise, weigh, normalise, store):
            for ch in chains:
                stage(ch)
        return carry

    lax.fori_loop(0, n_chunks, body, 0)


def _mlstm(oml, omla, gates_t, conv_w, conv_b, ctx_len, tm):
    bsz, t, _ = oml.shape
    n_chunks = t // CHUNK
    kern = functools.partial(_mlstm_kernel, t=t, ctx_len=ctx_len, tm=tm)
    consts = _mlstm_tables() + _head_tables()
    sps = _samples_per_step(bsz)
    once = dict(pipeline_mode=pl.Buffered(1))
    return pl.pallas_call(
        kern,
        grid=(bsz // sps,),
        in_specs=[pl.BlockSpec((sps, t, 768), lambda i: (i, 0, 0), **once),
                  pl.BlockSpec((sps, t, LANES), lambda i: (i, 0, 3), **once),
                  pl.BlockSpec((sps, n_chunks, 16, CHUNK), lambda i: (i, 0, 0, 0)),
                  pl.BlockSpec((3, 2 * ML_W), lambda i: (0, 0)),
                  pl.BlockSpec((1, 2 * ML_W), lambda i: (0, 0))] + [_const_spec(a) for a in consts],
        out_specs=pl.BlockSpec((sps, t, ML_W), lambda i: (i, 0, 0)),
        out_shape=jax.ShapeDtypeStruct((bsz, t, ML_W), F32),
        scratch_shapes=[pltpu.VMEM((sps, t, 2 * ML_W), F32),
                        pltpu.VMEM((sps, 2, ML_W, ML_W), F32),
                        pltpu.VMEM((sps, 2, 8, ML_W), F32),
                        pltpu.VMEM((sps, 2, 8, LANES), F32)],
        compiler_params=_params(("arbitrary",)),
        name="mlstm",
    )(oml, omla, gates_t, conv_w, conv_b[None, :], *consts)


def _hgrn_tables():
    sums = np.zeros((2, (N_MM_LEVELS + 1) * CHUNK, CHUNK), np.float32)
    pairs = np.zeros((2, N_LEVELS + 1, CHUNK, CHUNK), np.float32)
    sign = np.zeros((2, N_LEVELS - N_MM_LEVELS, CHUNK, 1), np.float32)
    for d in range(2):
        p = (lambda a: a) if d == 0 else (lambda a: CHUNK - 1 - a)
        for t in range(CHUNK):
            pairs[d, 0, p(t), p(t)] = 1.0
            for s in range(t + 1):
                sums[d, p(t), p(s)] = 1.0
        for l in range(N_LEVELS):
            w = 1 << l
            for t in range(CHUNK):
                ref = t - (t % (2 * w)) + w - 1
                if l < N_MM_LEVELS:
                    lo, hi = (ref + 1, t) if t > ref else (t + 1, ref)
                    for s in range(lo, hi + 1):
                        sums[d, (l + 1) * CHUNK + p(t), p(s)] = 1.0
                else:
                    sign[d, l - N_MM_LEVELS, p(t), 0] = 1.0 if t > ref else -1.0
                if t > ref:
                    for s in range(ref - w + 1, ref + 1):
                        pairs[d, l + 1, p(t), p(s)] = 1.0
    pairs = np.tile(pairs, (1, 1, 1, HG_HEADS))
    sign = np.broadcast_to(sign, sign.shape[:3] + (HG_W,))
    return jnp.asarray(sums, BF16), jnp.asarray(sign), jnp.asarray(pairs)


def _hgrn_ref_rows(cum, d, l):
    w = 1 << l
    parts = []
    for base in range(0, CHUNK, 2 * w):
        r = base + w - 1 if d == 0 else base + w
        parts.append(jnp.broadcast_to(cum[r:r + 1, :], (2 * w, cum.shape[1])))
    return parts[0] if len(parts) == 1 else jnp.concatenate(parts, axis=0)


def _hgrn_kernel(hg_ref, lbl_ref, sum_ref, sign_ref, pair_ref, sameb_ref, samef_ref, o_ref, st_s,
                 *, t, ctx_len, layer):
    n_chunks = t // CHUNK
    n_ctx_chunks = ctx_len // CHUNK
    w_all = HG_W

    if layer > 0:
        logits = lbl_ref[...]
        n_layers = logits.shape[0]
        mx = logits[0]
        for l in range(1, n_layers):
            mx = jnp.maximum(mx, logits[l])
        ex = [jnp.exp(logits[l] - mx) for l in range(n_layers)]
        tot = ex[0]
        for l in range(1, n_layers):
            tot = tot + ex[l]
        low_all = ex[1] / tot
        for l in range(2, layer + 1):
            low_all = low_all + ex[l] / tot
        log_low = jnp.log(low_all)
        log_1m_low = jnp.log1p(-low_all)

    o_ref[...] = jnp.zeros_like(o_ref)
    st_s[...] = jnp.zeros_like(st_s)

    def gates(ch):
        b, d = ch.b, ch.d
        ch.rows = pl.ds(pl.multiple_of(ch.c * CHUNK, CHUNK), CHUNK)
        ch.q = _silu(hg_ref[b, ch.rows, 0:w_all])
        ch.vb = hg_ref[b, ch.rows, w_all:2 * w_all].astype(BF16)
        pre = hg_ref[b, ch.rows, (2 + d) * w_all:(3 + d) * w_all]
        ls = _log_sigmoid(pre)
        if layer == 0:
            log_f = ls
            ch.key = jnp.exp(ls - pre)
        else:
            lo_, hi_ = log_low[d:d + 1, :], log_1m_low[d:d + 1, :] + ls
            log_f = jnp.maximum(lo_, hi_) + jnp.log1p(jnp.exp(-jnp.abs(lo_ - hi_)))
            ch.key = (1.0 - low_all[d:d + 1, :]) * jnp.exp(ls - pre)
        hi, lo = _split(log_f)
        ch.xb = _mm(sum_ref[d], hi) + _mm(sum_ref[d], lo)
        ch.st = st_s[b, d]
        ch.amat = _nt(ch.q.astype(BF16), _stack_heads(ch.key.astype(BF16), sameb_ref[...])) * pair_ref[d, 0]

    def levels(ch):
        d = ch.d
        cum = ch.xb[0:CHUNK]
        for l in range(N_LEVELS):
            if l < N_MM_LEVELS:
                expo = ch.xb[(l + 1) * CHUNK:(l + 2) * CHUNK]
            else:
                expo = (cum - _hgrn_ref_rows(cum, d, l)) * sign_ref[d, l - N_MM_LEVELS]
            fac = jnp.exp(expo)
            qt = (ch.q * fac).astype(BF16)
            kt = (ch.key * fac).astype(BF16)
            ch.amat = ch.amat + _nt(qt, _stack_heads(kt, sameb_ref[...])) * pair_ref[d, l + 1]
        last = CHUNK - 1 if d == 0 else 0
        bend = cum[last:last + 1, :]
        ch.inter = _nt((ch.q * jnp.exp(cum)).astype(BF16), ch.st.astype(BF16))
        ch.dst = _tn(ch.vb, (ch.key * jnp.exp(bend - cum)).astype(BF16))
        ch.decay = jnp.exp(bend)

    def readout(ch):
        ch.intra = _mm(ch.amat.astype(BF16), _stack_heads(ch.vb, sameb_ref[...]))

    def store(ch):
        o_ref[ch.b, ch.rows, :] += ch.intra + ch.inter
        st_s[ch.b, ch.d] = ch.decay * ch.st + ch.dst * samef_ref[...]

    def body(i, carry):
        chunk = (i, _dir1_chunk(i, n_ctx_chunks, n_chunks))
        chains = [_Chain(b=b, d=d, c=chunk[d]) for b in range(hg_ref.shape[0]) for d in range(2)]
        for stage in (gates, levels, readout, store):
            for ch in chains:
                stage(ch)
        return carry

    lax.fori_loop(0, n_chunks, body, 0)


def _hgrn(ohg, lb_logits, ctx_len, layer):
    bsz, t, _ = ohg.shape
    kern = functools.partial(_hgrn_kernel, t=t, ctx_len=ctx_len, layer=layer)
    consts = (lb_logits,) + _hgrn_tables() + _head_tables()
    sps = _samples_per_step(bsz)
    return pl.pallas_call(
        kern,
        grid=(bsz // sps,),
        in_specs=[pl.BlockSpec((sps, t, 4 * HG_W), lambda i: (i, 0, 0), pipeline_mode=pl.Buffered(1))]
        + [_const_spec(a) for a in consts],
        out_specs=pl.BlockSpec((sps, t, HG_W), lambda i: (i, 0, 0)),
        out_shape=jax.ShapeDtypeStruct((bsz, t, HG_W), F32),
        scratch_shapes=[pltpu.VMEM((sps, 2, HG_W, HG_W), F32)],
        compiler_params=_params(("arbitrary",)),
        name="hgrn",
    )(ohg, *consts)


def _mla_prep_kernel(x_ref, gq_ref, gkv_ref, wq_ref, wk_ref, wv_ref, gqq_ref, gkk_ref,
                     cos_ref, sa_ref, sb_ref, q_ref, k_ref, v_ref):
    x = x_ref[0]
    cq = _rms_rows(x[:, 0:MLA_Q_RANK], gq_ref[...]).astype(BF16)
    ckv = _rms_rows(x[:, MLA_Q_RANK:MLA_Q_RANK + MLA_KV_RANK], gkv_ref[...]).astype(BF16)
    misc = x[:, MLA_Q_RANK + MLA_KV_RANK:]
    lane = lax.broadcasted_iota(jnp.int32, misc.shape, 1)
    k_rope = jnp.where((lane >= MLA_NOPE) & (lane < MLA_DQK), misc, 0.0)
    q_raw = _mm(cq, wq_ref[...])
    k_raw = _mm(ckv, wk_ref[...])
    v_all = _mm(ckv, wv_ref[...]).astype(BF16)
    cos, sa, sb = cos_ref[...], sa_ref[...], sb_ref[...]

    def norm_rope(xh, g):
        ms = jnp.sum(xh * xh, axis=-1, keepdims=True) * (1.0 / MLA_DQK)
        xn = xh * lax.rsqrt(ms + EPS) * g
        return xn * cos + pltpu.roll(xn, HEAD_PAD - 8, 1) * sa + pltpu.roll(xn, 8, 1) * sb

    for h in range(MLA_HEADS):
        sl = slice(h * HEAD_PAD, (h + 1) * HEAD_PAD)
        q_ref[0, h] = (norm_rope(q_raw[:, sl], gqq_ref[...]) * (MLA_DQK ** -0.5)).astype(BF16)
        k_ref[0, h] = norm_rope(k_raw[:, sl] + k_rope, gkk_ref[...]).astype(BF16)
        v_ref[0, h] = v_all[:, h * MLA_DV:(h + 1) * MLA_DV]


def _mla_prep(omla, gq, gkv, w_uq, w_ukv, g_qq, g_kk, cos, sa, sb, tm):
    bsz, t, _ = omla.shape
    hw = MLA_HEADS * HEAD_PAD
    wq = w_uq.reshape(MLA_Q_RANK, MLA_HEADS, MLA_DQK)
    wq = jnp.pad(wq, ((0, 0), (0, 0), (0, HEAD_PAD - MLA_DQK))).reshape(MLA_Q_RANK, hw).astype(BF16)
    wkv = w_ukv.reshape(MLA_KV_RANK, MLA_HEADS, MLA_NOPE + MLA_DV)
    wk = jnp.pad(wkv[:, :, :MLA_NOPE], ((0, 0), (0, 0), (0, HEAD_PAD - MLA_NOPE))).reshape(MLA_KV_RANK, hw).astype(BF16)
    wv = wkv[:, :, MLA_NOPE:].reshape(MLA_KV_RANK, MLA_HEADS * MLA_DV).astype(BF16)
    padg = lambda g: jnp.pad(g, (0, HEAD_PAD - MLA_DQK))[None, :]
    full = lambda shape: pl.BlockSpec(shape, lambda i, j: (0,) * len(shape))
    rope_spec = pl.BlockSpec((tm, HEAD_PAD), lambda i, j: (j, 0))
    return pl.pallas_call(
        _mla_prep_kernel,
        grid=(bsz, t // tm),
        in_specs=[pl.BlockSpec((1, tm, 512), lambda i, j: (i, j, 0)),
                  full((1, MLA_Q_RANK)), full((1, MLA_KV_RANK)),
                  full((MLA_Q_RANK, hw)), full((MLA_KV_RANK, hw)), full((MLA_KV_RANK, MLA_HEADS * MLA_DV)),
                  full((1, HEAD_PAD)), full((1, HEAD_PAD)), rope_spec, rope_spec, rope_spec],
        out_specs=[pl.BlockSpec((1, MLA_HEADS, tm, HEAD_PAD), lambda i, j: (i, 0, j, 0)),
                   pl.BlockSpec((1, MLA_HEADS, tm, HEAD_PAD), lambda i, j: (i, 0, j, 0)),
                   pl.BlockSpec((1, MLA_HEADS, tm, MLA_DV), lambda i, j: (i, 0, j, 0))],
        out_shape=[jax.ShapeDtypeStruct((bsz, MLA_HEADS, t, HEAD_PAD), BF16),
                   jax.ShapeDtypeStruct((bsz, MLA_HEADS, t, HEAD_PAD), BF16),
                   jax.ShapeDtypeStruct((bsz, MLA_HEADS, t, MLA_DV), BF16)],
        compiler_params=_params(("arbitrary", "arbitrary")),
        name="mla_prep",
    )(omla, gq[None, :], gkv[None, :], wq, wk, wv, padg(g_qq), padg(g_kk), cos, sa, sb)


def _attn_kernel(q_ref, k_ref, v_ref, o_ref, *, ctx_len, nct, j0):
    is_ctx = pl.program_id(1) + j0 < nct

    def attend(n_keys):
        scores = lambda h: _nt(q_ref[0, h], k_ref[0, h, 0:n_keys, :])
        s_next = scores(0)
        for h in range(MLA_HEADS):
            s, s_next = s_next, (scores(h + 1) if h + 1 < MLA_HEADS else None)
            p = jnp.exp(s - jnp.max(s, axis=-1, keepdims=True))
            o = _mm(p.astype(BF16), v_ref[0, h, 0:n_keys, :]) / jnp.sum(p, axis=-1, keepdims=True)
            o_ref[0, :, h * MLA_DV:(h + 1) * MLA_DV] = o

    @pl.when(is_ctx)
    def _():
        attend(ctx_len)

    @pl.when(jnp.logical_not(is_ctx))
    def _():
        attend(k_ref.shape[2])


def _attention(q, k, v, ctx_len, tm, j0):
    bsz, _, t, _ = q.shape
    nct = ctx_len // tm
    nq = t // tm - j0
    kern = functools.partial(_attn_kernel, ctx_len=ctx_len, nct=nct, j0=j0)
    return pl.pallas_call(
        kern,
        grid=(bsz, nq),
        in_specs=[pl.BlockSpec((1, MLA_HEADS, tm, HEAD_PAD), lambda i, j: (i, 0, j + j0, 0)),
                  pl.BlockSpec((1, MLA_HEADS, t, HEAD_PAD), lambda i, j: (i, 0, 0, 0)),
                  pl.BlockSpec((1, MLA_HEADS, t, MLA_DV), lambda i, j: (i, 0, 0, 0))],
        out_specs=pl.BlockSpec((1, tm, MLA_HEADS * MLA_DV), lambda i, j: (i, j, 0)),
        out_shape=jax.ShapeDtypeStruct((bsz, nq * tm, MLA_HEADS * MLA_DV), F32),
        compiler_params=_params(("arbitrary", "arbitrary")),
        name="attention",
    )(q, k, v)


def _merge_kernel(ml_ref, hg_ref, at_ref, gate_ref, x_ref, mod_ref, gml_ref, ghg_ref, wo_ref,
                  g2_ref, rwt_ref, same_ref, x1_ref, h2_ref, aff_ref):
    head_mean = same_ref[...] * (1.0 / ML_DH)

    def head_norm(a, g):
        hi, lo = _split(a * a)
        ms = _mm(hi, head_mean) + _mm(lo, head_mean)
        return a * lax.rsqrt(ms + EPS) * g

    gates = gate_ref[0]
    y_ml = head_norm(ml_ref[0], gml_ref[...]) * jax.nn.sigmoid(gates[:, 0:ML_W])
    y_hg = head_norm(hg_ref[0], ghg_ref[...]) * _silu(gates[:, ML_W:ML_W + HG_W])
    y = (_mm(y_ml.astype(BF16), wo_ref[0:ML_W, :])
         + _mm(y_hg.astype(BF16), wo_ref[ML_W:ML_W + HG_W, :])
         + _mm(at_ref[0].astype(BF16), wo_ref[ML_W + HG_W:, :]))
    x1 = x_ref[0] + mod_ref[0, 0, 2:3, :] * y
    x1_ref[0] = x1
    h2 = _rms_rows(x1, g2_ref[...]) * (1.0 + mod_ref[0, 0, 4:5, :]) + mod_ref[0, 0, 3:4, :]
    h2_ref[0] = h2.astype(BF16)
    w_hi, w_lo = _split(rwt_ref[...])
    h_hi, h_lo = _split(h2)
    logits = _nt(w_hi, h_hi) + _nt(w_hi, h_lo) + _nt(w_lo, h_hi)
    e = jnp.exp(logits - jnp.max(logits, axis=0, keepdims=True))
    aff_ref[0] = e / jnp.sum(e, axis=0, keepdims=True)


def _merge(ml, hg, at, ogate, xs, modtok, g_ml, g_hg, w_out, g2, router_w, tm, j0, n_tiles, at_j0):
    bsz, _, d = xs.shape
    n = n_tiles * tm
    sel = 0 if j0 == 0 and n_tiles * tm < xs.shape[1] else 1
    tok = lambda width, off: pl.BlockSpec((1, tm, width), lambda i, j: (i, j + off, 0))
    full = lambda shape: pl.BlockSpec(shape, lambda i, j: (0,) * len(shape))
    same_b, _ = _head_tables()
    return pl.pallas_call(
        _merge_kernel,
        grid=(bsz, n_tiles),
        in_specs=[tok(ML_W, j0), tok(HG_W, j0), tok(512, j0 - at_j0), tok(512, j0), tok(d, j0),
                  pl.BlockSpec((1, 1, 6, d), lambda i, j: (i, sel, 0, 0)),
                  full((1, ML_W)), full((1, HG_W)), full((d, d)), full((1, d)), full((N_EXPERTS, d)),
                  _const_spec(same_b)],
        out_specs=[tok(d, 0), tok(d, 0), pl.BlockSpec((1, N_EXPERTS, tm), lambda i, j: (i, 0, j))],
        out_shape=[jax.ShapeDtypeStruct((bsz, n, d), F32),
                   jax.ShapeDtypeStruct((bsz, n, d), BF16),
                   jax.ShapeDtypeStruct((bsz, N_EXPERTS, n), F32)],
        compiler_params=_params(("arbitrary", "arbitrary")),
        name="merge",
    )(ml, hg, at, ogate, xs, modtok, g_ml[None, :], g_hg[None, :], w_out.astype(BF16), g2[None, :],
      router_w.T, same_b)


def _route_kernel(aff_ref, rank_ref, *, cap, n):
    aff = aff_ref[0]

    def search(i, thr):
        cand = thr | (jnp.int32(1) << (30 - i))
        cnt = jnp.sum(jnp.where(aff >= pltpu.bitcast(cand, F32), 1, 0), axis=-1, keepdims=True)
        return jnp.where(cnt >= cap, cand, thr)

    thr = lax.fori_loop(0, 31, search, jnp.zeros((N_EXPERTS, 1), jnp.int32))
    above = aff >= pltpu.bitcast(thr + 1, F32)
    tied = jnp.logical_and(aff >= pltpu.bitcast(thr, F32), jnp.logical_not(above))
    need = cap - jnp.sum(jnp.where(above, 1, 0), axis=-1, keepdims=True)

    tl = min(n, 256)
    r = lax.broadcasted_iota(jnp.int32, (tl, tl), 0)
    c = lax.broadcasted_iota(jnp.int32, (tl, tl), 1)
    before = jnp.where(r < c, 1.0, 0.0).astype(BF16)

    def excl_cumsum(mask):
        parts, carry = [], jnp.zeros((N_EXPERTS, 1), F32)
        for j in range(n // tl):
            m = jnp.where(mask[:, j * tl:(j + 1) * tl], 1.0, 0.0)
            parts.append(_mm(m.astype(BF16), before) + carry)
            carry = carry + jnp.sum(m, axis=-1, keepdims=True)
        return jnp.concatenate(parts, axis=-1).astype(jnp.int32)

    keep = jnp.logical_or(above, jnp.logical_and(tied, excl_cumsum(tied) < need))
    rank_ref[0] = jnp.where(keep, excl_cumsum(keep), -1)


def _route(aff_t, cap):
    bsz, e, n = aff_t.shape
    kern = functools.partial(_route_kernel, cap=cap, n=n)
    return pl.pallas_call(
        kern,
        grid=(bsz,),
        in_specs=[pl.BlockSpec((1, e, n), lambda i: (i, 0, 0))],
        out_specs=pl.BlockSpec((1, e, n), lambda i: (i, 0, 0)),
        out_shape=jax.ShapeDtypeStruct((bsz, e, n), jnp.int32),
        compiler_params=_params(("arbitrary",)),
        name="route",
    )(aff_t)


def _ffn_kernel(rank_ref, aff_ref, h_ref, x1_ref, g2_ref, wg_ref, wu_ref, wd_ref, o_ref, *, cap, n, bb):
    e = pl.program_id(1)
    slot = lax.broadcasted_iota(jnp.int32, (cap, n), 0)
    picks, gates, xs = [], [], []
    for b in range(bb):
        chosen = rank_ref[b, 0] == slot
        pick = jnp.where(chosen, 1.0, 0.0).astype(BF16)
        picks.append(pick)
        gates.append(jnp.sum(jnp.where(chosen, aff_ref[b, 0], 0.0), axis=-1, keepdims=True))
        xs.append(_mm(pick, h_ref[b]).astype(BF16))
    xs = jnp.concatenate(xs, axis=0)
    hid = (_silu(_mm(xs, wg_ref[0])) * _mm(xs, wu_ref[0])).astype(BF16)
    out = _mm(hid, wd_ref[0])

    @pl.when(e == 0)
    def _():
        o_ref[...] = jnp.zeros_like(o_ref)

    d = out.shape[1]
    for b in range(bb):
        out_b = (out[b * cap:(b + 1) * cap] * gates[b]).astype(BF16)
        for j in range(d // 256):
            o_ref[b, :, j * 256:(j + 1) * 256] += _tn(picks[b], out_b[:, j * 256:(j + 1) * 256])

    @pl.when(e == pl.num_programs(1) - 1)
    def _():
        o_ref[...] = x1_ref[...] + g2_ref[...] * o_ref[...]


def _ffn(rank, aff_t, h2, x1, g2mod, wg, wu, wd, cap, bb):
    bsz, n, d = h2.shape
    n_exp = wg.shape[0]
    kern = functools.partial(_ffn_kernel, cap=cap, n=n, bb=bb)
    once = dict(pipeline_mode=pl.Buffered(1))
    return pl.pallas_call(
        kern,
        grid=(bsz // bb, n_exp),
        in_specs=[pl.BlockSpec((bb, 1, 1, n), lambda i, e: (i, e, 0, 0)),
                  pl.BlockSpec((bb, 1, 1, n), lambda i, e: (i, e, 0, 0)),
                  pl.BlockSpec((bb, n, d), lambda i, e: (i, 0, 0), **once),
                  pl.BlockSpec((bb, n, d), lambda i, e: (i, 0, 0), **once),
                  pl.BlockSpec((bb, 1, d), lambda i, e: (i, 0, 0)),
                  pl.BlockSpec((1, d, d), lambda i, e: (e, 0, 0)),
                  pl.BlockSpec((1, d, d), lambda i, e: (e, 0, 0)),
                  pl.BlockSpec((1, d, d), lambda i, e: (e, 0, 0))],
        out_specs=pl.BlockSpec((bb, n, d), lambda i, e: (i, 0, 0)),
        out_shape=jax.ShapeDtypeStruct((bsz, n, d), F32),
        compiler_params=_params(("arbitrary", "arbitrary")),
        name="expert_ffn",
    )(rank.reshape(bsz, n_exp, 1, n), aff_t.reshape(bsz, n_exp, 1, n), h2, x1, g2mod, wg, wu, wd)


def _rope_tables(ctx_len, seq):
    half = MLA_ROPE // 2
    inv = ROPE_THETA ** (-jnp.arange(0, half, 2, dtype=F32) / half)
    rows = seq // GRID_W
    row_pos = jnp.repeat(jnp.arange(rows), GRID_W).astype(F32)
    col_pos = jnp.broadcast_to(jnp.arange(GRID_W), (rows, GRID_W)).reshape(-1).astype(F32)

    def cs(pos):
        ang = pos[:, None] * inv[None, :]
        ang = jnp.concatenate([ang, ang], axis=-1)
        return jnp.cos(ang), jnp.sin(ang)

    cos_r, sin_r = cs(row_pos)
    cos_c, sin_c = cs(col_pos)
    first = jnp.arange(half) < half // 2
    zeros = lambda n: jnp.zeros((seq, n), F32)
    ones = lambda n: jnp.ones((seq, n), F32)
    tail = HEAD_PAD - MLA_DQK
    cos = jnp.concatenate([ones(MLA_NOPE), cos_r, cos_c, ones(tail)], axis=-1)
    sa = jnp.concatenate([zeros(MLA_NOPE), jnp.where(first, -sin_r, 0.0), jnp.where(first, -sin_c, 0.0),
                          zeros(tail)], axis=-1)
    sb = jnp.concatenate([zeros(MLA_NOPE), jnp.where(first, 0.0, sin_r), jnp.where(first, 0.0, sin_c),
                          zeros(tail)], axis=-1)
    ident = lambda a, fill: jnp.concatenate([jnp.full((ctx_len, HEAD_PAD), fill, F32), a], axis=0)
    return ident(cos, 1.0), ident(sa, 0.0), ident(sb, 0.0)


def kernel(x, c, ctx, c_ctx, ada_w, ada_b, norm1_g, norm2_g, w_in, b_in, ml_conv_w, ml_conv_b, ml_norm_g, hg_lb_logits, hg_norm_g, mla_q_norm_g, mla_w_uq, mla_kv_norm_g, mla_w_ukv, mla_q_qk_g, mla_k_qk_g, w_out, router_w, ex_w_gate, ex_w_up, ex_w_down):
    bsz, seq, d = x.shape
    ctx_len = ctx.shape[1]
    depth = ada_w.shape[0]
    t = ctx_len + seq
    tm = 256 if ctx_len % 256 == 0 else 128
    assert ctx_len % tm == 0 and seq % tm == 0 and ctx_len % CHUNK == 0 and seq % GRID_W == 0
    nct = ctx_len // tm
    n_chunks = t // CHUNK

    rows = -(-(bsz + 1) // 8) * 8
    cc = jnp.concatenate([c, c_ctx[None, :], jnp.zeros((rows - bsz - 1, d), F32)], axis=0)
    mod = _modulation(cc, ada_w, ada_b)
    cos, sa, sb = _rope_tables(ctx_len, seq)
    bb_ctx = max(bb for bb in (8, 4, 2, 1) if bsz % bb == 0)

    xs = jnp.concatenate([ctx, x], axis=1)
    lat = None
    for layer in range(depth):
        need_ctx = layer < depth - 1
        m = mod[layer]
        modtok = jnp.stack([jnp.broadcast_to(m[bsz], (bsz, 6 * d)), m[:bsz]], axis=1).reshape(bsz, 2, 6, d)
        oml, ohg, ogate, omla, ogt = _in_proj(xs, modtok, norm1_g[layer], w_in[layer], b_in[layer], tm, nct)
        gates_t = ogt.reshape(bsz, 16, n_chunks, CHUNK).transpose(0, 2, 1, 3)
        ml = _mlstm(oml, omla, gates_t, ml_conv_w[layer], ml_conv_b[layer], ctx_len, tm)
        hg = _hgrn(ohg, hg_lb_logits, ctx_len, layer)
        q, k, v = _mla_prep(omla, mla_q_norm_g[layer], mla_kv_norm_g[layer], mla_w_uq[layer], mla_w_ukv[layer],
                            mla_q_qk_g[layer], mla_k_qk_g[layer], cos, sa, sb, tm)
        at_j0 = 0 if need_ctx else nct
        at = _attention(q, k, v, ctx_len, tm, at_j0)
        wg, wu, wd = (w[layer].astype(BF16) for w in (ex_w_gate, ex_w_up, ex_w_down))
        g2mod = modtok[:, :, 5:6, :]

        def post(j0, n_tiles, g2, bb):
            x1, h2, aff = _merge(ml, hg, at, ogate, xs, modtok, ml_norm_g[layer], hg_norm_g[layer], w_out[layer],
                                 norm2_g[layer], router_w[layer], tm, j0, n_tiles, at_j0)
            n = n_tiles * tm
            cap = EC_CAPACITY * n // N_EXPERTS
            rank = _route(aff, cap)
            return _ffn(rank, aff, h2, x1, g2, wg, wu, wd, cap, bb)

        lat = post(nct, seq // tm, g2mod[:, 1], 1)
        if need_ctx:
            xs = jnp.concatenate([post(0, nct, g2mod[:, 0], bb_ctx), lat], axis=1)
    return lat
```

```python
import functools

import numpy as np
import jax
import jax.numpy as jnp
from jax import lax
from jax.experimental import pallas as pl
from jax.experimental.pallas import tpu as pltpu

F32 = jnp.float32
BF16 = jnp.bfloat16
HIGHEST = lax.Precision.HIGHEST

D_MODEL = 1024
GRID_W = 64
ML_HEADS = 4
ML_DH = 64
ML_W = 256
HG_HEADS = 4
HG_W = 256
MLA_HEADS = 8
MLA_NOPE = 64
MLA_ROPE = 32
MLA_DQK = 96
MLA_DV = 64
MLA_Q_RANK = 256
MLA_KV_RANK = 128
N_EXPERTS = 16
EC_CAPACITY = 2
CHUNK = 64
ROPE_THETA = 10000.0
EPS = 1e-6
HEAD_PAD = 128
LANES = 128
N_LEVELS = 6
N_MM_LEVELS = 3
SLAB = ML_HEADS * CHUNK
VMEM_LIMIT = 52 * 1024 * 1024

_O_QK, _O_V, _O_OG, _O_GATES, _O_HQ, _O_HI, _O_HGATE, _O_HF, _O_CQ, _O_CKV, _O_KR, _O_END = (
    0, 512, 768, 1024, 1040, 1296, 1552, 1808, 2320, 2576, 2704, 2736)


def _nt(a, b, **kw):
    return lax.dot_general(a, b, (((1,), (1,)), ((), ())), preferred_element_type=F32, **kw)


def _tn(a, b, **kw):
    return lax.dot_general(a, b, (((0,), (0,)), ((), ())), preferred_element_type=F32, **kw)


def _mm(a, b, **kw):
    return jnp.dot(a, b, preferred_element_type=F32, **kw)


def _split(x):
    hi = x.astype(BF16)
    return hi, (x - hi.astype(F32)).astype(BF16)


def _silu(x):
    return x * jax.nn.sigmoid(x)


def _log_sigmoid(x):
    return jnp.minimum(x, 0.0) - jnp.log1p(jnp.exp(-jnp.abs(x)))


def _rms_rows(x, g):
    return x * lax.rsqrt(jnp.mean(x * x, axis=-1, keepdims=True) + EPS) * g


def _params(sem=None):
    return pltpu.CompilerParams(dimension_semantics=sem, vmem_limit_bytes=VMEM_LIMIT)


def _const_spec(a):
    return pl.BlockSpec(a.shape, lambda *_: (0,) * a.ndim)


def _mod_kernel(c_ref, w_ref, b_ref, o_ref):
    s = _silu(c_ref[...]).astype(BF16)
    o_ref[0] = _mm(s, w_ref[0].astype(BF16)) + b_ref[0]


def _modulation(cc, ada_w, ada_b):
    n_layers, d, n6 = ada_w.shape
    rows = cc.shape[0]
    tn = 1024
    return pl.pallas_call(
        _mod_kernel,
        grid=(n_layers, n6 // tn),
        in_specs=[pl.BlockSpec((rows, d), lambda l, j: (0, 0)),
                  pl.BlockSpec((1, d, tn), lambda l, j: (l, 0, j)),
                  pl.BlockSpec((1, 1, tn), lambda l, j: (l, 0, j))],
        out_specs=pl.BlockSpec((1, rows, tn), lambda l, j: (l, 0, j)),
        out_shape=jax.ShapeDtypeStruct((n_layers, rows, n6), F32),
        compiler_params=_params(("arbitrary", "arbitrary")),
        name="modulation",
    )(cc, ada_w, ada_b.reshape(n_layers, 1, n6))


def _in_kernel(x_ref, mod_ref, g_ref, w_ref, b_ref, wgt_ref, bgt_ref,
               oml_ref, ohg_ref, ogate_ref, omla_ref, ogt_ref):
    x = x_ref[0]
    sh = mod_ref[0, 0, 0:1, :]
    sc = mod_ref[0, 0, 1:2, :]
    h = (_rms_rows(x, g_ref[...]) * (1.0 + sc) + sh).astype(BF16)
    oml_ref[0] = _mm(h, w_ref[:, 0:768]) + b_ref[:, 0:768]
    ohg_ref[0] = _mm(h, w_ref[:, 768:1792]) + b_ref[:, 768:1792]
    ogate_ref[0] = _mm(h, w_ref[:, 1792:2304]) + b_ref[:, 1792:2304]
    omla_ref[0] = _mm(h, w_ref[:, 2304:2816]) + b_ref[:, 2304:2816]
    ogt_ref[0] = _nt(wgt_ref[...], h) + bgt_ref[...]


def _in_proj(xs, modtok, g1, w_in, b_in, tm, nct):
    bsz, t, d = xs.shape
    src, sign = _rotate_half_columns(MLA_ROPE)
    zeros = lambda n: jnp.zeros((d, n), F32)
    w = jnp.concatenate([
        w_in[:, _O_QK:_O_OG], w_in[:, _O_HQ:_O_HGATE], w_in[:, _O_HF:_O_CQ],
        w_in[:, _O_OG:_O_GATES], w_in[:, _O_HGATE:_O_HF], w_in[:, _O_CQ:_O_KR],
        w_in[:, _O_GATES:_O_HQ], zeros(48), w_in[:, _O_KR:_O_END],
        w_in[:, _O_KR:_O_END][:, src] * sign], axis=1).astype(BF16)
    zb = lambda n: jnp.zeros((n,), F32)
    b = jnp.concatenate([
        b_in[_O_QK:_O_OG], b_in[_O_HQ:_O_HGATE], b_in[_O_HF:_O_CQ],
        b_in[_O_OG:_O_GATES], b_in[_O_HGATE:_O_HF], b_in[_O_CQ:_O_KR],
        b_in[_O_GATES:_O_HQ], zb(48), b_in[_O_KR:_O_END], b_in[_O_KR:_O_END][src] * sign])[None, :]
    wgt = w_in[:, _O_GATES:_O_HQ].T.astype(BF16)
    bgt = b_in[_O_GATES:_O_HQ][:, None]
    nw = w.shape[1]
    tok = lambda width: pl.BlockSpec((1, tm, width), lambda i, j: (i, j, 0))
    full = lambda shape: pl.BlockSpec(shape, lambda i, j: (0,) * len(shape))
    return pl.pallas_call(
        _in_kernel,
        grid=(bsz, t // tm),
        in_specs=[tok(d),
                  pl.BlockSpec((1, 1, 6, d), lambda i, j: (i, jnp.where(j >= nct, 1, 0), 0, 0)),
                  full((1, d)), full((d, nw)), full((1, nw)), full((16, d)), full((16, 1))],
        out_specs=[tok(768), tok(1024), tok(512), tok(512),
                   pl.BlockSpec((1, 16, tm), lambda i, j: (i, 0, j))],
        out_shape=[jax.ShapeDtypeStruct((bsz, t, 768), F32),
                   jax.ShapeDtypeStruct((bsz, t, 1024), F32),
                   jax.ShapeDtypeStruct((bsz, t, 512), F32),
                   jax.ShapeDtypeStruct((bsz, t, 512), F32),
                   jax.ShapeDtypeStruct((bsz, 16, t), F32)],
        compiler_params=_params(("arbitrary", "arbitrary")),
        name="in_proj",
    )(xs, modtok, g1[None, :], w, b, wgt, bgt)


def _dir1_chunk(i, n_ctx_chunks, n_chunks):
    return jnp.where(i < n_ctx_chunks, n_ctx_chunks - 1 - i, n_chunks - 1 - (i - n_ctx_chunks))


def _head_tables():
    same = (np.arange(SLAB)[:, None] // CHUNK == np.arange(SLAB)[None, :] // CHUNK).astype(np.float32)
    return jnp.asarray(same, BF16), jnp.asarray(same, F32)


def _stack_heads(x_bf, same_bf):
    return jnp.concatenate([x_bf] * ML_HEADS, axis=0) * same_bf


class _Chain:
    def __init__(self, **kw):
        self.__dict__.update(kw)


def _samples_per_step(bsz):
    return 2 if bsz % 2 == 0 else 1


def _mm_split(a, b):
    hi, lo = _split(a)
    return _mm(hi, b) + _mm(lo, b)


def _mlstm_tables():
    t = np.arange(CHUNK)
    lower = (t[:, None] >= t[None, :]).astype(np.float32)
    tri = np.stack([lower, lower.T])
    neg = np.stack([np.tile(np.where(m > 0, 0.0, -np.inf), (1, ML_HEADS)) for m in (lower, lower.T)]).astype(np.float32)
    spread = np.zeros((2, LANES, ML_W), np.float32)
    for d in range(2):
        for h in range(ML_HEADS):
            spread[d, d * 8 + 4 + h, h * ML_DH:(h + 1) * ML_DH] = 1.0
    gather = spread.transpose(0, 2, 1)
    return jnp.asarray(tri, BF16), jnp.asarray(neg), jnp.asarray(spread, BF16), jnp.asarray(gather, BF16)


def _mlstm_kernel(ml_ref, gc_ref, gt_ref, cw_ref, cb_ref, tri_ref, neg_ref, spread_ref, gather_ref,
                  sameb_ref, samef_ref, o_ref, qk_s, c_s, n_s, m_s, *, t, ctx_len, tm):
    n_chunks = t // CHUNK
    n_ctx_chunks = ctx_len // CHUNK
    n_samples = ml_ref.shape[0]
    cw = cw_ref[...]
    cb = cb_ref[...]
    rid = lax.broadcasted_iota(jnp.int32, (tm, 2 * ML_W), 0)
    for b in range(n_samples):
        for j in range(t // tm):
            r0 = j * tm
            cur = ml_ref[b, r0:r0 + tm, 0:2 * ML_W]
            up = pltpu.roll(cur, 1, 0)
            if r0 in (0, ctx_len):
                up = jnp.where(rid == 0, 0.0, up)
            else:
                up = jnp.where(rid == 0, ml_ref[b, r0 - 1:r0, 0:2 * ML_W], up)
            dn = pltpu.roll(cur, tm - 1, 0)
            if r0 + tm in (ctx_len, t):
                dn = jnp.where(rid == tm - 1, 0.0, dn)
            else:
                dn = jnp.where(rid == tm - 1, ml_ref[b, r0 + tm:r0 + tm + 1, 0:2 * ML_W], dn)
            y = cw[0:1, :] * up + cw[1:2, :] * cur + cw[2:3, :] * dn + cb
            qk_s[b, r0:r0 + tm, :] = _silu(y)

    o_ref[...] = jnp.zeros_like(o_ref)
    c_s[...] = jnp.zeros_like(c_s)
    n_s[...] = jnp.zeros_like(n_s)
    m_s[...] = jnp.zeros_like(m_s)
    gate_lane = lax.broadcasted_iota(jnp.int32, (CHUNK, LANES), 1) < 16
    tok = lax.broadcasted_iota(jnp.int32, (CHUNK, LANES), 0)

    def load(ch):
        d, b = ch.d, ch.b
        ch.rows = pl.ds(pl.multiple_of(ch.c * CHUNK, CHUNK), CHUNK)
        qk = qk_s[b, ch.rows, :]
        ch.qb = qk[:, 0:ML_W].astype(BF16)
        ch.k = qk[:, ML_W:2 * ML_W] * (ML_DH ** -0.5)
        ch.vb = ml_ref[b, ch.rows, 2 * ML_W:3 * ML_W].astype(BF16)
        ch.gc = jnp.where(gate_lane, gc_ref[b, ch.rows, :], 0.0)
        ch.gt = gt_ref[b, ch.c]
        hi, lo = _split(_log_sigmoid(ch.gc))
        ch.b_c = _mm(tri_ref[d], hi) + _mm(tri_ref[d], lo)
        ch.b_r = _mm_split(_log_sigmoid(ch.gt), tri_ref[1 - d])
        ch.s_raw = _nt(ch.qb, _stack_heads(ch.k.astype(BF16), sameb_ref[...]))
        ch.qn = _mm((qk[:, 0:ML_W] * n_s[b, d][0:1, :]).astype(BF16), gather_ref[d])
        ch.c_mat = c_s[b, d]
        ch.qc = _mm(ch.qb, ch.c_mat.astype(BF16))

    def stabilise(ch):
        d = ch.d
        ch.u = pltpu.roll(ch.gc, 4, 1) - ch.b_c
        cmax = ch.u
        for step in (1, 2, 4, 8, 16, 32):
            if d == 0:
                moved = jnp.where(tok >= step, pltpu.roll(cmax, step, 0), -jnp.inf)
            else:
                moved = jnp.where(tok < CHUNK - step, pltpu.roll(cmax, CHUNK - step, 0), -jnp.inf)
            cmax = jnp.maximum(cmax, moved)
        ch.m = m_s[ch.b, d][0:1, :]
        inter = ch.b_c + ch.m
        ch.m_t = jnp.maximum(inter, ch.b_c + cmax)
        ch.w_st = jnp.exp(inter - ch.m_t)
        last = CHUNK - 1 if d == 0 else 0
        ch.bend = ch.b_c[last:last + 1, :]
        ch.m_new = jnp.maximum(ch.bend + ch.m, ch.bend + cmax[last:last + 1, :])
        fcols = [d * 8 + 4 + h for h in range(ML_HEADS)]
        icols = [d * 8 + h for h in range(ML_HEADS)]
        ch.u_row = jnp.concatenate([ch.gt[i:i + 1, :] - ch.b_r[f:f + 1, :] for i, f in zip(icols, fcols)], axis=1)
        ch.bm_wide = _mm_split(ch.b_c - ch.m_t, spread_ref[d])
        ch.wk_wide = _mm(jnp.exp(ch.bend + ch.u - ch.m_new).astype(BF16), spread_ref[d])
        ch.w_old = jnp.broadcast_to(jnp.exp(ch.bend + ch.m - ch.m_new), (8, LANES))
        ch.wold_wide = _mm_split(ch.w_old, spread_ref[d])

    def weigh(ch):
        d = ch.d
        s = ch.s_raw * jnp.exp(ch.bm_wide + ch.u_row + neg_ref[d])
        sb = s.astype(BF16)
        ch.intra = _mm(sb, _stack_heads(ch.vb, sameb_ref[...]))
        ch.rowsum = _mm(sb, gather_ref[d])
        kw = ch.k * ch.wk_wide
        ch.dc = _tn(kw.astype(BF16), ch.vb)
        ch.dn = jnp.sum(kw, axis=0, keepdims=True)

    def normalise(ch):
        d = ch.d
        den = ch.rowsum + ch.w_st * ch.qn
        r = 1.0 / jnp.maximum(jnp.abs(den), jnp.exp(-ch.m_t))
        ch.r_wide = _mm(r.astype(BF16), spread_ref[d])
        ch.wr_wide = _mm((ch.w_st * r).astype(BF16), spread_ref[d])

    def store(ch):
        b, d = ch.b, ch.d
        o_ref[b, ch.rows, :] += ch.intra * ch.r_wide + ch.qc * ch.wr_wide
        w_old = ch.wold_wide[0:1, :]
        c_s[b, d] = w_old * ch.c_mat + ch.dc * samef_ref[...]
        n_s[b, d] = jnp.broadcast_to(w_old * n_s[b, d][0:1, :] + ch.dn, (8, ML_W))
        m_s[b, d] = jnp.broadcast_to(ch.m_new, (8, LANES))

    def body(i, carry):
        chunk = (i, _dir1_chunk(i, n_ctx_chunks, n_chunks))
        chains = [_Chain(b=b, d=d, c=chunk[d]) for b in range(n_samples) for d in range(2)]
        for stage in (load, stabilise, weigh, normalise, store):
            for ch in chains:
                stage(ch)
        return carry

    lax.fori_loop(0, n_chunks, body, 0)


def _mlstm(oml, omla, gates_t, conv_w, conv_b, ctx_len, tm):
    bsz, t, _ = oml.shape
    n_chunks = t // CHUNK
    kern = functools.partial(_mlstm_kernel, t=t, ctx_len=ctx_len, tm=tm)
    consts = _mlstm_tables() + _head_tables()
    sps = _samples_per_step(bsz)
    once = dict(pipeline_mode=pl.Buffered(1))
    return pl.pallas_call(
        kern,
        grid=(bsz // sps,),
        in_specs=[pl.BlockSpec((sps, t, 768), lambda i: (i, 0, 0), **once),
                  pl.BlockSpec((sps, t, LANES), lambda i: (i, 0, 3), **once),
                  pl.BlockSpec((sps, n_chunks, 16, CHUNK), lambda i: (i, 0, 0, 0)),
                  pl.BlockSpec((3, 2 * ML_W), lambda i: (0, 0)),
                  pl.BlockSpec((1, 2 * ML_W), lambda i: (0, 0))] + [_const_spec(a) for a in consts],
        out_specs=pl.BlockSpec((sps, t, ML_W), lambda i: (i, 0, 0)),
        out_shape=jax.ShapeDtypeStruct((bsz, t, ML_W), F32),
        scratch_shapes=[pltpu.VMEM((sps, t, 2 * ML_W), F32),
                        pltpu.VMEM((sps, 2, ML_W, ML_W), F32),
                        pltpu.VMEM((sps, 2, 8, ML_W), F32),
                        pltpu.VMEM((sps, 2, 8, LANES), F32)],
        compiler_params=_params(("arbitrary",)),
        name="mlstm",
    )(oml, omla, gates_t, conv_w, conv_b[None, :], *consts)


def _hgrn_tables():
    sums = np.zeros((2, (N_MM_LEVELS + 1) * CHUNK, CHUNK), np.float32)
    pairs = np.zeros((2, N_LEVELS + 1, CHUNK, CHUNK), np.float32)
    sign = np.zeros((2, N_LEVELS - N_MM_LEVELS, CHUNK, 1), np.float32)
    for d in range(2):
        p = (lambda a: a) if d == 0 else (lambda a: CHUNK - 1 - a)
        for t in range(CHUNK):
            pairs[d, 0, p(t), p(t)] = 1.0
            for s in range(t + 1):
                sums[d, p(t), p(s)] = 1.0
        for l in range(N_LEVELS):
            w = 1 << l
            for t in range(CHUNK):
                ref = t - (t % (2 * w)) + w - 1
                if l < N_MM_LEVELS:
                    lo, hi = (ref + 1, t) if t > ref else (t + 1, ref)
                    for s in range(lo, hi + 1):
                        sums[d, (l + 1) * CHUNK + p(t), p(s)] = 1.0
                else:
                    sign[d, l - N_MM_LEVELS, p(t), 0] = 1.0 if t > ref else -1.0
                if t > ref:
                    for s in range(ref - w + 1, ref + 1):
                        pairs[d, l + 1, p(t), p(s)] = 1.0
    pairs = np.tile(pairs, (1, 1, 1, HG_HEADS))
    sign = np.broadcast_to(sign, sign.shape[:3] + (HG_W,))
    return jnp.asarray(sums, BF16), jnp.asarray(sign), jnp.asarray(pairs)


def _hgrn_ref_rows(cum, d, l):
    w = 1 << l
    parts = []
    for base in range(0, CHUNK, 2 * w):
        r = base + w - 1 if d == 0 else base + w
        parts.append(jnp.broadcast_to(cum[r:r + 1, :], (2 * w, cum.shape[1])))
    return parts[0] if len(parts) == 1 else jnp.concatenate(parts, axis=0)


def _hgrn_kernel(hg_ref, lbl_ref, sum_ref, sign_ref, pair_ref, sameb_ref, samef_ref, o_ref, st_s,
                 *, t, ctx_len, layer):
    n_chunks = t // CHUNK
    n_ctx_chunks = ctx_len // CHUNK
    w_all = HG_W

    if layer > 0:
        logits = lbl_ref[...]
        n_layers = logits.shape[0]
        mx = logits[0]
        for l in range(1, n_layers):
            mx = jnp.maximum(mx, logits[l])
        ex = [jnp.exp(logits[l] - mx) for l in range(n_layers)]
        tot = ex[0]
        for l in range(1, n_layers):
            tot = tot + ex[l]
        low_all = ex[1] / tot
        for l in range(2, layer + 1):
            low_all = low_all + ex[l] / tot
        log_low = jnp.log(low_all)
        log_1m_low = jnp.log1p(-low_all)

    o_ref[...] = jnp.zeros_like(o_ref)
    st_s[...] = jnp.zeros_like(st_s)

    def gates(ch):
        b, d = ch.b, ch.d
        ch.rows = pl.ds(pl.multiple_of(ch.c * CHUNK, CHUNK), CHUNK)
        ch.q = _silu(hg_ref[b, ch.rows, 0:w_all])
        ch.vb = hg_ref[b, ch.rows, w_all:2 * w_all].astype(BF16)
        pre = hg_ref[b, ch.rows, (2 + d) * w_all:(3 + d) * w_all]
        ls = _log_sigmoid(pre)
        if layer == 0:
            log_f = ls
            ch.key = jnp.exp(ls - pre)
        else:
            lo_, hi_ = log_low[d:d + 1, :], log_1m_low[d:d + 1, :] + ls
            log_f = jnp.maximum(lo_, hi_) + jnp.log1p(jnp.exp(-jnp.abs(lo_ - hi_)))
            ch.key = (1.0 - low_all[d:d + 1, :]) * jnp.exp(ls - pre)
        hi, lo = _split(log_f)
        ch.xb = _mm(sum_ref[d], hi) + _mm(sum_ref[d], lo)
        ch.st = st_s[b, d]
        ch.amat = _nt(ch.q.astype(BF16), _stack_heads(ch.key.astype(BF16), sameb_ref[...])) * pair_ref[d, 0]

    def levels(ch):
        d = ch.d
        cum = ch.xb[0:CHUNK]
        for l in range(N_LEVELS):
            if l < N_MM_LEVELS:
                expo = ch.xb[(l + 1) * CHUNK:(l + 2) * CHUNK]
            else:
                expo = (cum - _hgrn_ref_rows(cum, d, l)) * sign_ref[d, l - N_MM_LEVELS]
            fac = jnp.exp(expo)
            qt = (ch.q * fac).astype(BF16)
            kt = (ch.key * fac).astype(BF16)
            ch.amat = ch.amat + _nt(qt, _stack_heads(kt, sameb_ref[...])) * pair_ref[d, l + 1]
        last = CHUNK - 1 if d == 0 else 0
        bend = cum[last:last + 1, :]
        ch.inter = _nt((ch.q * jnp.exp(cum)).astype(BF16), ch.st.astype(BF16))
        ch.dst = _tn(ch.vb, (ch.key * jnp.exp(bend - cum)).astype(BF16))
        ch.decay = jnp.exp(bend)

    def readout(ch):
        ch.intra = _mm(ch.amat.astype(BF16), _stack_heads(ch.vb, sameb_ref[...]))

    def store(ch):
        o_ref[ch.b, ch.rows, :] += ch.intra + ch.inter
        st_s[ch.b, ch.d] = ch.decay * ch.st + ch.dst * samef_ref[...]

    def body(i, carry):
        chunk = (i, _dir1_chunk(i, n_ctx_chunks, n_chunks))
        chains = [_Chain(b=b, d=d, c=chunk[d]) for b in range(hg_ref.shape[0]) for d in range(2)]
        for stage in (gates, levels, readout, store):
            for ch in chains:
                stage(ch)
        return carry

    lax.fori_loop(0, n_chunks, body, 0)


def _hgrn(ohg, lb_logits, ctx_len, layer):
    bsz, t, _ = ohg.shape
    kern = functools.partial(_hgrn_kernel, t=t, ctx_len=ctx_len, layer=layer)
    consts = (lb_logits,) + _hgrn_tables() + _head_tables()
    sps = _samples_per_step(bsz)
    return pl.pallas_call(
        kern,
        grid=(bsz // sps,),
        in_specs=[pl.BlockSpec((sps, t, 4 * HG_W), lambda i: (i, 0, 0), pipeline_mode=pl.Buffered(1))]
        + [_const_spec(a) for a in consts],
        out_specs=pl.BlockSpec((sps, t, HG_W), lambda i: (i, 0, 0)),
        out_shape=jax.ShapeDtypeStruct((bsz, t, HG_W), F32),
        scratch_shapes=[pltpu.VMEM((sps, 2, HG_W, HG_W), F32)],
        compiler_params=_params(("arbitrary",)),
        name="hgrn",
    )(ohg, *consts)


def _mla_prep_kernel(x_ref, gq_ref, gkv_ref, wq_ref, wqr_ref, wk_ref, wv_ref, gqq_ref, gqr_ref, gkk_ref, gkr_ref,
                     cos_ref, sin_ref, q_ref, k_ref, v_ref):
    x = x_ref[0]
    cq = _rms_rows(x[:, 0:MLA_Q_RANK], gq_ref[...]).astype(BF16)
    ckv = _rms_rows(x[:, MLA_Q_RANK:MLA_Q_RANK + MLA_KV_RANK], gkv_ref[...]).astype(BF16)
    misc = x[:, MLA_Q_RANK + MLA_KV_RANK:]
    lane = lax.broadcasted_iota(jnp.int32, misc.shape, 1)
    rope_lane = (lane >= MLA_NOPE) & (lane < MLA_DQK)
    k_rope = jnp.where(rope_lane, misc, 0.0)
    k_rope_rot = jnp.where(rope_lane, pltpu.roll(misc, HEAD_PAD - MLA_ROPE, 1), 0.0)
    q_raw = _mm(cq, wq_ref[...])
    q_rot = _mm(cq, wqr_ref[...])
    k_raw = _mm(ckv, wk_ref[...])
    v_all = _mm(ckv, wv_ref[...]).astype(BF16)
    cos, sin = cos_ref[...], sin_ref[...]

    def norm_rope(xh, xr, g, gr):
        ms = jnp.sum(xh * xh, axis=-1, keepdims=True) * (1.0 / MLA_DQK)
        return (xh * (g * cos) + xr * (gr * sin)) * lax.rsqrt(ms + EPS)

    for h in range(MLA_HEADS):
        sl = slice(h * HEAD_PAD, (h + 1) * HEAD_PAD)
        q = norm_rope(q_raw[:, sl], q_rot[:, sl], gqq_ref[...], gqr_ref[...])
        q_ref[0, h] = (q * (MLA_DQK ** -0.5)).astype(BF16)
        k_ref[0, h] = norm_rope(k_raw[:, sl] + k_rope, k_rope_rot, gkk_ref[...], gkr_ref[...]).astype(BF16)
        v_ref[0, h] = v_all[:, h * MLA_DV:(h + 1) * MLA_DV]


def _rotate_half_columns(n):
    j = np.arange(n)
    first = j % 16 < 8
    return np.where(first, j + 8, j - 8), np.where(first, -1.0, 1.0).astype(np.float32)


def _mla_prep(omla, gq, gkv, w_uq, w_ukv, g_qq, g_kk, cos, sin, tm):
    bsz, t, _ = omla.shape
    hw = MLA_HEADS * HEAD_PAD
    src, sign = _rotate_half_columns(MLA_ROPE)
    src_pad = np.arange(HEAD_PAD)
    src_pad[MLA_NOPE:MLA_DQK] = MLA_NOPE + src
    sign_pad = np.zeros((HEAD_PAD,), np.float32)
    sign_pad[MLA_NOPE:MLA_DQK] = sign
    wq = w_uq.reshape(MLA_Q_RANK, MLA_HEADS, MLA_DQK)
    wq = jnp.pad(wq, ((0, 0), (0, 0), (0, HEAD_PAD - MLA_DQK)))
    wq_rot = (wq[:, :, src_pad] * sign_pad).reshape(MLA_Q_RANK, hw).astype(BF16)
    wq = wq.reshape(MLA_Q_RANK, hw).astype(BF16)
    wkv = w_ukv.reshape(MLA_KV_RANK, MLA_HEADS, MLA_NOPE + MLA_DV)
    wk = jnp.pad(wkv[:, :, :MLA_NOPE], ((0, 0), (0, 0), (0, HEAD_PAD - MLA_NOPE))).reshape(MLA_KV_RANK, hw).astype(BF16)
    wv = wkv[:, :, MLA_NOPE:].reshape(MLA_KV_RANK, MLA_HEADS * MLA_DV).astype(BF16)
    padg = lambda g: jnp.pad(g, (0, HEAD_PAD - MLA_DQK))
    gains = [padg(g_qq)[None, :], (padg(g_qq)[src_pad] * jnp.abs(sign_pad))[None, :],
             padg(g_kk)[None, :], (padg(g_kk)[src_pad] * jnp.abs(sign_pad))[None, :]]
    full = lambda shape: pl.BlockSpec(shape, lambda i, j: (0,) * len(shape))
    rope_spec = pl.BlockSpec((tm, HEAD_PAD), lambda i, j: (j, 0))
    return pl.pallas_call(
        _mla_prep_kernel,
        grid=(bsz, t // tm),
        in_specs=[pl.BlockSpec((1, tm, 512), lambda i, j: (i, j, 0)),
                  full((1, MLA_Q_RANK)), full((1, MLA_KV_RANK)),
                  full((MLA_Q_RANK, hw)), full((MLA_Q_RANK, hw)), full((MLA_KV_RANK, hw)),
                  full((MLA_KV_RANK, MLA_HEADS * MLA_DV))] + [full((1, HEAD_PAD))] * 4 + [rope_spec, rope_spec],
        out_specs=[pl.BlockSpec((1, MLA_HEADS, tm, HEAD_PAD), lambda i, j: (i, 0, j, 0)),
                   pl.BlockSpec((1, MLA_HEADS, tm, HEAD_PAD), lambda i, j: (i, 0, j, 0)),
                   pl.BlockSpec((1, MLA_HEADS, tm, MLA_DV), lambda i, j: (i, 0, j, 0))],
        out_shape=[jax.ShapeDtypeStruct((bsz, MLA_HEADS, t, HEAD_PAD), BF16),
                   jax.ShapeDtypeStruct((bsz, MLA_HEADS, t, HEAD_PAD), BF16),
                   jax.ShapeDtypeStruct((bsz, MLA_HEADS, t, MLA_DV), BF16)],
        compiler_params=_params(("arbitrary", "arbitrary")),
        name="mla_prep",
    )(omla, gq[None, :], gkv[None, :], wq, wq_rot, wk, wv, *gains, cos, sin)


def _attn_kernel(q_ref, k_ref, v_ref, o_ref, *, ctx_len, nct, j0):
    is_ctx = pl.program_id(1) + j0 < nct

    def attend(n_keys):
        scores = lambda h: _nt(q_ref[0, h], k_ref[0, h, 0:n_keys, :])
        s_next = scores(0)
        for h in range(MLA_HEADS):
            s, s_next = s_next, (scores(h + 1) if h + 1 < MLA_HEADS else None)
            p = jnp.exp(s - jnp.max(s, axis=-1, keepdims=True))
            o = _mm(p.astype(BF16), v_ref[0, h, 0:n_keys, :]) / jnp.sum(p, axis=-1, keepdims=True)
            o_ref[0, :, h * MLA_DV:(h + 1) * MLA_DV] = o

    @pl.when(is_ctx)
    def _():
        attend(ctx_len)

    @pl.when(jnp.logical_not(is_ctx))
    def _():
        attend(k_ref.shape[2])


def _attention(q, k, v, ctx_len, tm, j0):
    bsz, _, t, _ = q.shape
    nct = ctx_len // tm
    nq = t // tm - j0
    kern = functools.partial(_attn_kernel, ctx_len=ctx_len, nct=nct, j0=j0)
    return pl.pallas_call(
        kern,
        grid=(bsz, nq),
        in_specs=[pl.BlockSpec((1, MLA_HEADS, tm, HEAD_PAD), lambda i, j: (i, 0, j + j0, 0)),
                  pl.BlockSpec((1, MLA_HEADS, t, HEAD_PAD), lambda i, j: (i, 0, 0, 0)),
                  pl.BlockSpec((1, MLA_HEADS, t, MLA_DV), lambda i, j: (i, 0, 0, 0))],
        out_specs=pl.BlockSpec((1, tm, MLA_HEADS * MLA_DV), lambda i, j: (i, j, 0)),
        out_shape=jax.ShapeDtypeStruct((bsz, nq * tm, MLA_HEADS * MLA_DV), F32),
        compiler_params=_params(("arbitrary", "arbitrary")),
        name="attention",
    )(q, k, v)


def _merge_kernel(ml_ref, hg_ref, at_ref, gate_ref, x_ref, mod_ref, gml_ref, ghg_ref, wo_ref,
                  g2_ref, rwt_ref, same_ref, x1_ref, h2_ref, aff_ref):
    head_mean = same_ref[...] * (1.0 / ML_DH)
    w_hi, w_lo = _split(rwt_ref[...])
    half = x_ref.shape[1] // 2

    def mean_squares(ch):
        ch.ml, ch.hg = ml_ref[0, ch.rows, :], hg_ref[0, ch.rows, :]
        hi, lo = _split(ch.ml * ch.ml)
        ch.ms_ml = _mm(hi, head_mean) + _mm(lo, head_mean)
        hi, lo = _split(ch.hg * ch.hg)
        ch.ms_hg = _mm(hi, head_mean) + _mm(lo, head_mean)

    def project(ch):
        gates = gate_ref[0, ch.rows, :]
        y_ml = ch.ml * lax.rsqrt(ch.ms_ml + EPS) * gml_ref[...] * jax.nn.sigmoid(gates[:, 0:ML_W])
        y_hg = ch.hg * lax.rsqrt(ch.ms_hg + EPS) * ghg_ref[...] * _silu(gates[:, ML_W:ML_W + HG_W])
        ch.y = (_mm(y_ml.astype(BF16), wo_ref[0:ML_W, :])
                + _mm(y_hg.astype(BF16), wo_ref[ML_W:ML_W + HG_W, :])
                + _mm(at_ref[0, ch.rows, :].astype(BF16), wo_ref[ML_W + HG_W:, :]))

    def route(ch):
        x1 = x_ref[0, ch.rows, :] + mod_ref[0, 0, 2:3, :] * ch.y
        x1_ref[0, ch.rows, :] = x1
        h2 = _rms_rows(x1, g2_ref[...]) * (1.0 + mod_ref[0, 0, 4:5, :]) + mod_ref[0, 0, 3:4, :]
        h2_ref[0, ch.rows, :] = h2.astype(BF16)
        h_hi, h_lo = _split(h2)
        ch.logits = _nt(w_hi, h_hi) + _nt(w_hi, h_lo) + _nt(w_lo, h_hi)

    def affinity(ch):
        e = jnp.exp(ch.logits - jnp.max(ch.logits, axis=0, keepdims=True))
        aff_ref[0, :, ch.rows] = e / jnp.sum(e, axis=0, keepdims=True)

    chains = [_Chain(rows=slice(i * half, (i + 1) * half)) for i in range(2)]
    for stage in (mean_squares, project, route, affinity):
        for ch in chains:
            stage(ch)


def _merge(ml, hg, at, ogate, xs, modtok, g_ml, g_hg, w_out, g2, router_w, tm, j0, n_tiles, at_j0):
    bsz, _, d = xs.shape
    n = n_tiles * tm
    sel = 0 if j0 == 0 and n_tiles * tm < xs.shape[1] else 1
    tok = lambda width, off: pl.BlockSpec((1, tm, width), lambda i, j: (i, j + off, 0))
    full = lambda shape: pl.BlockSpec(shape, lambda i, j: (0,) * len(shape))
    same_b, _ = _head_tables()
    return pl.pallas_call(
        _merge_kernel,
        grid=(bsz, n_tiles),
        in_specs=[tok(ML_W, j0), tok(HG_W, j0), tok(512, j0 - at_j0), tok(512, j0), tok(d, j0),
                  pl.BlockSpec((1, 1, 6, d), lambda i, j: (i, sel, 0, 0)),
                  full((1, ML_W)), full((1, HG_W)), full((d, d)), full((1, d)), full((N_EXPERTS, d)),
                  _const_spec(same_b)],
        out_specs=[tok(d, 0), tok(d, 0), pl.BlockSpec((1, N_EXPERTS, tm), lambda i, j: (i, 0, j))],
        out_shape=[jax.ShapeDtypeStruct((bsz, n, d), F32),
                   jax.ShapeDtypeStruct((bsz, n, d), BF16),
                   jax.ShapeDtypeStruct((bsz, N_EXPERTS, n), F32)],
        compiler_params=_params(("arbitrary", "arbitrary")),
        name="merge",
    )(ml, hg, at, ogate, xs, modtok, g_ml[None, :], g_hg[None, :], w_out.astype(BF16), g2[None, :],
      router_w.T, same_b)


def _route_kernel(aff_ref, rank_ref, *, cap, n):
    aff = aff_ref[0]

    def search(i, thr):
        cand = thr | (jnp.int32(1) << (30 - i))
        cnt = jnp.sum(jnp.where(aff >= pltpu.bitcast(cand, F32), 1, 0), axis=-1, keepdims=True)
        return jnp.where(cnt >= cap, cand, thr)

    thr = lax.fori_loop(0, 31, search, jnp.zeros((N_EXPERTS, 1), jnp.int32))
    above = aff >= pltpu.bitcast(thr + 1, F32)
    tied = jnp.logical_and(aff >= pltpu.bitcast(thr, F32), jnp.logical_not(above))
    need = cap - jnp.sum(jnp.where(above, 1, 0), axis=-1, keepdims=True)

    tl = min(n, 256)
    r = lax.broadcasted_iota(jnp.int32, (tl, tl), 0)
    c = lax.broadcasted_iota(jnp.int32, (tl, tl), 1)
    before = jnp.where(r < c, 1.0, 0.0).astype(BF16)

    def excl_cumsum(mask):
        parts, carry = [], jnp.zeros((N_EXPERTS, 1), F32)
        for j in range(n // tl):
            m = jnp.where(mask[:, j * tl:(j + 1) * tl], 1.0, 0.0)
            parts.append(_mm(m.astype(BF16), before) + carry)
            carry = carry + jnp.sum(m, axis=-1, keepdims=True)
        return jnp.concatenate(parts, axis=-1).astype(jnp.int32)

    keep = jnp.logical_or(above, jnp.logical_and(tied, excl_cumsum(tied) < need))
    rank_ref[0] = jnp.where(keep, excl_cumsum(keep), -1)


def _route(aff_t, cap):
    bsz, e, n = aff_t.shape
    kern = functools.partial(_route_kernel, cap=cap, n=n)
    return pl.pallas_call(
        kern,
        grid=(bsz,),
        in_specs=[pl.BlockSpec((1, e, n), lambda i: (i, 0, 0))],
        out_specs=pl.BlockSpec((1, e, n), lambda i: (i, 0, 0)),
        out_shape=jax.ShapeDtypeStruct((bsz, e, n), jnp.int32),
        compiler_params=_params(("arbitrary",)),
        name="route",
    )(aff_t)


def _ffn_kernel(rank_ref, aff_ref, h_ref, x1_ref, g2_ref, wg_ref, wu_ref, wd_ref, o_ref, *, cap, n, bb):
    e = pl.program_id(1)
    slot = lax.broadcasted_iota(jnp.int32, (cap, n), 0)
    picks, gates, xs = [], [], []
    for b in range(bb):
        chosen = rank_ref[b, 0] == slot
        pick = jnp.where(chosen, 1.0, 0.0).astype(BF16)
        picks.append(pick)
        gates.append(jnp.sum(jnp.where(chosen, aff_ref[b, 0], 0.0), axis=-1, keepdims=True))
        xs.append(_mm(pick, h_ref[b]).astype(BF16))
    xs = jnp.concatenate(xs, axis=0)
    hid = (_silu(_mm(xs, wg_ref[0])) * _mm(xs, wu_ref[0])).astype(BF16)
    out = _mm(hid, wd_ref[0])

    @pl.when(e == 0)
    def _():
        o_ref[...] = jnp.zeros_like(o_ref)

    d = out.shape[1]
    for b in range(bb):
        out_b = (out[b * cap:(b + 1) * cap] * gates[b]).astype(BF16)
        for j in range(d // 256):
            o_ref[b, :, j * 256:(j + 1) * 256] += _tn(picks[b], out_b[:, j * 256:(j + 1) * 256])

    @pl.when(e == pl.num_programs(1) - 1)
    def _():
        o_ref[...] = x1_ref[...] + g2_ref[...] * o_ref[...]


def _ffn(rank, aff_t, h2, x1, g2mod, wg, wu, wd, cap, bb):
    bsz, n, d = h2.shape
    n_exp = wg.shape[0]
    kern = functools.partial(_ffn_kernel, cap=cap, n=n, bb=bb)
    once = dict(pipeline_mode=pl.Buffered(1))
    return pl.pallas_call(
        kern,
        grid=(bsz // bb, n_exp),
        in_specs=[pl.BlockSpec((bb, 1, 1, n), lambda i, e: (i, e, 0, 0)),
                  pl.BlockSpec((bb, 1, 1, n), lambda i, e: (i, e, 0, 0)),
                  pl.BlockSpec((bb, n, d), lambda i, e: (i, 0, 0), **once),
                  pl.BlockSpec((bb, n, d), lambda i, e: (i, 0, 0), **once),
                  pl.BlockSpec((bb, 1, d), lambda i, e: (i, 0, 0)),
                  pl.BlockSpec((1, d, d), lambda i, e: (e, 0, 0)),
                  pl.BlockSpec((1, d, d), lambda i, e: (e, 0, 0)),
                  pl.BlockSpec((1, d, d), lambda i, e: (e, 0, 0))],
        out_specs=pl.BlockSpec((bb, n, d), lambda i, e: (i, 0, 0)),
        out_shape=jax.ShapeDtypeStruct((bsz, n, d), F32),
        compiler_params=_params(("arbitrary", "arbitrary")),
        name="expert_ffn",
    )(rank.reshape(bsz, n_exp, 1, n), aff_t.reshape(bsz, n_exp, 1, n), h2, x1, g2mod, wg, wu, wd)


def _rope_tables(ctx_len, seq):
    half = MLA_ROPE // 2
    inv = ROPE_THETA ** (-jnp.arange(0, half, 2, dtype=F32) / half)
    rows = seq // GRID_W
    row_pos = jnp.repeat(jnp.arange(rows), GRID_W).astype(F32)
    col_pos = jnp.broadcast_to(jnp.arange(GRID_W), (rows, GRID_W)).reshape(-1).astype(F32)

    def cs(pos):
        ang = pos[:, None] * inv[None, :]
        ang = jnp.concatenate([ang, ang], axis=-1)
        return jnp.cos(ang), jnp.sin(ang)

    cos_r, sin_r = cs(row_pos)
    cos_c, sin_c = cs(col_pos)
    zeros = lambda n: jnp.zeros((seq, n), F32)
    ones = lambda n: jnp.ones((seq, n), F32)
    tail = HEAD_PAD - MLA_DQK
    cos = jnp.concatenate([ones(MLA_NOPE), cos_r, cos_c, ones(tail)], axis=-1)
    sin = jnp.concatenate([zeros(MLA_NOPE), sin_r, sin_c, zeros(tail)], axis=-1)
    ident = lambda a, fill: jnp.concatenate([jnp.full((ctx_len, HEAD_PAD), fill, F32), a], axis=0)
    return ident(cos, 1.0), ident(sin, 0.0)


def kernel(x, c, ctx, c_ctx, ada_w, ada_b, norm1_g, norm2_g, w_in, b_in, ml_conv_w, ml_conv_b, ml_norm_g, hg_lb_logits, hg_norm_g, mla_q_norm_g, mla_w_uq, mla_kv_norm_g, mla_w_ukv, mla_q_qk_g, mla_k_qk_g, w_out, router_w, ex_w_gate, ex_w_up, ex_w_down):
    bsz, seq, d = x.shape
    ctx_len = ctx.shape[1]
    depth = ada_w.shape[0]
    t = ctx_len + seq
    tm = 256 if ctx_len % 256 == 0 else 128
    assert ctx_len % tm == 0 and seq % tm == 0 and ctx_len % CHUNK == 0 and seq % GRID_W == 0
    nct = ctx_len // tm
    n_chunks = t // CHUNK

    rows = -(-(bsz + 1) // 8) * 8
    cc = jnp.concatenate([c, c_ctx[None, :], jnp.zeros((rows - bsz - 1, d), F32)], axis=0)
    mod = _modulation(cc, ada_w, ada_b)
    cos, sin = _rope_tables(ctx_len, seq)
    bb_ctx = max(bb for bb in (8, 4, 2, 1) if bsz % bb == 0)

    xs = jnp.concatenate([ctx, x], axis=1)
    lat = None
    for layer in range(depth):
        need_ctx = layer < depth - 1
        m = mod[layer]
        modtok = jnp.stack([jnp.broadcast_to(m[bsz], (bsz, 6 * d)), m[:bsz]], axis=1).reshape(bsz, 2, 6, d)
        oml, ohg, ogate, omla, ogt = _in_proj(xs, modtok, norm1_g[layer], w_in[layer], b_in[layer], tm, nct)
        gates_t = ogt.reshape(bsz, 16, n_chunks, CHUNK).transpose(0, 2, 1, 3)
        ml = _mlstm(oml, omla, gates_t, ml_conv_w[layer], ml_conv_b[layer], ctx_len, tm)
        hg = _hgrn(ohg, hg_lb_logits, ctx_len, layer)
        q, k, v = _mla_prep(omla, mla_q_norm_g[layer], mla_kv_norm_g[layer], mla_w_uq[layer], mla_w_ukv[layer],
                            mla_q_qk_g[layer], mla_k_qk_g[layer], cos, sin, tm)
        at_j0 = 0 if need_ctx else nct
        at = _attention(q, k, v, ctx_len, tm, at_j0)
        wg, wu, wd = (w[layer].astype(BF16) for w in (ex_w_gate, ex_w_up, ex_w_down))
        g2mod = modtok[:, :, 5:6, :]

        def post(j0, n_tiles, g2, bb):
            x1, h2, aff = _merge(ml, hg, at, ogate, xs, modtok, ml_norm_g[layer], hg_norm_g[layer], w_out[layer],
                                 norm2_g[layer], router_w[layer], tm, j0, n_tiles, at_j0)
            n = n_tiles * tm
            cap = EC_CAPACITY * n // N_EXPERTS
            rank = _route(aff, cap)
            return _ffn(rank, aff, h2, x1, g2, wg, wu, wd, cap, bb)

        lat = post(nct, seq // tm, g2mod[:, 1], 1)
        if need_ctx:
            xs = jnp.concatenate([post(0, nct, g2mod[:, 0], bb_ctx), lat], axis=1)
    return lat
```

```python
import functools

import numpy as np
import jax
import jax.numpy as jnp
from jax import lax
from jax.experimental import pallas as pl
from jax.experimental.pallas import tpu as pltpu

F32 = jnp.float32
BF16 = jnp.bfloat16
HIGHEST = lax.Precision.HIGHEST

D_MODEL = 1024
GRID_W = 64
ML_HEADS = 4
ML_DH = 64
ML_W = 256
HG_HEADS = 4
HG_W = 256
MLA_HEADS = 8
MLA_NOPE = 64
MLA_ROPE = 32
MLA_DQK = 96
MLA_DV = 64
MLA_Q_RANK = 256
MLA_KV_RANK = 128
N_EXPERTS = 16
EC_CAPACITY = 2
CHUNK = 64
ROPE_THETA = 10000.0
EPS = 1e-6
HEAD_PAD = 128
LANES = 128
N_LEVELS = 6
N_MM_LEVELS = 3
SLAB = ML_HEADS * CHUNK
VMEM_LIMIT = 52 * 1024 * 1024

_O_QK, _O_V, _O_OG, _O_GATES, _O_HQ, _O_HI, _O_HGATE, _O_HF, _O_CQ, _O_CKV, _O_KR, _O_END = (
    0, 512, 768, 1024, 1040, 1296, 1552, 1808, 2320, 2576, 2704, 2736)


def _nt(a, b, **kw):
    return lax.dot_general(a, b, (((1,), (1,)), ((), ())), preferred_element_type=F32, **kw)


def _tn(a, b, **kw):
    return lax.dot_general(a, b, (((0,), (0,)), ((), ())), preferred_element_type=F32, **kw)


def _mm(a, b, **kw):
    return jnp.dot(a, b, preferred_element_type=F32, **kw)


def _split(x):
    hi = x.astype(BF16)
    return hi, (x - hi.astype(F32)).astype(BF16)


def _silu(x):
    return x * jax.nn.sigmoid(x)


def _log_sigmoid(x):
    return jnp.minimum(x, 0.0) - jnp.log1p(jnp.exp(-jnp.abs(x)))


def _rms_rows(x, g):
    return x * lax.rsqrt(jnp.mean(x * x, axis=-1, keepdims=True) + EPS) * g


def _params(sem=None):
    return pltpu.CompilerParams(dimension_semantics=sem, vmem_limit_bytes=VMEM_LIMIT)


def _const_spec(a):
    return pl.BlockSpec(a.shape, lambda *_: (0,) * a.ndim)


def _mod_kernel(c_ref, w_ref, b_ref, o_ref):
    s = _silu(c_ref[...]).astype(BF16)
    o_ref[0] = _mm(s, w_ref[0].astype(BF16)) + b_ref[0]


def _modulation(cc, ada_w, ada_b):
    n_layers, d, n6 = ada_w.shape
    rows = cc.shape[0]
    tn = 1024
    return pl.pallas_call(
        _mod_kernel,
        grid=(n_layers, n6 // tn),
        in_specs=[pl.BlockSpec((rows, d), lambda l, j: (0, 0)),
                  pl.BlockSpec((1, d, tn), lambda l, j: (l, 0, j)),
                  pl.BlockSpec((1, 1, tn), lambda l, j: (l, 0, j))],
        out_specs=pl.BlockSpec((1, rows, tn), lambda l, j: (l, 0, j)),
        out_shape=jax.ShapeDtypeStruct((n_layers, rows, n6), F32),
        compiler_params=_params(("arbitrary", "arbitrary")),
        name="modulation",
    )(cc, ada_w, ada_b.reshape(n_layers, 1, n6))


def _in_kernel(c_ref, l_ref, mod_ref, g_ref, w_ref, b_ref, wgt_ref, bgt_ref,
               oml_ref, ohg_ref, ogate_ref, omla_ref, ogt_ref, *, nct):
    x = jnp.where(pl.program_id(1) < nct, c_ref[0], l_ref[0])
    sh = mod_ref[0, 0, 0:1, :]
    sc = mod_ref[0, 0, 1:2, :]
    h = (_rms_rows(x, g_ref[...]) * (1.0 + sc) + sh).astype(BF16)
    oml_ref[0] = _mm(h, w_ref[:, 0:768]) + b_ref[:, 0:768]
    ohg_ref[0] = _mm(h, w_ref[:, 768:1792]) + b_ref[:, 768:1792]
    ogate_ref[0] = _mm(h, w_ref[:, 1792:2304]) + b_ref[:, 1792:2304]
    omla_ref[0] = _mm(h, w_ref[:, 2304:2816]) + b_ref[:, 2304:2816]
    ogt_ref[0] = _nt(wgt_ref[...], h) + bgt_ref[...]


def _in_proj(ctx_x, lat_x, modtok, g1, w_in, b_in, tm, nct):
    bsz, seq, d = lat_x.shape
    t = ctx_x.shape[1] + seq
    src, sign = _rotate_half_columns(MLA_ROPE)
    zeros = lambda n: jnp.zeros((d, n), F32)
    w = jnp.concatenate([
        w_in[:, _O_QK:_O_OG], w_in[:, _O_HQ:_O_HGATE], w_in[:, _O_HF:_O_CQ],
        w_in[:, _O_OG:_O_GATES], w_in[:, _O_HGATE:_O_HF], w_in[:, _O_CQ:_O_KR],
        w_in[:, _O_GATES:_O_HQ], zeros(48), w_in[:, _O_KR:_O_END],
        w_in[:, _O_KR:_O_END][:, src] * sign], axis=1).astype(BF16)
    zb = lambda n: jnp.zeros((n,), F32)
    b = jnp.concatenate([
        b_in[_O_QK:_O_OG], b_in[_O_HQ:_O_HGATE], b_in[_O_HF:_O_CQ],
        b_in[_O_OG:_O_GATES], b_in[_O_HGATE:_O_HF], b_in[_O_CQ:_O_KR],
        b_in[_O_GATES:_O_HQ], zb(48), b_in[_O_KR:_O_END], b_in[_O_KR:_O_END][src] * sign])[None, :]
    wgt = w_in[:, _O_GATES:_O_HQ].T.astype(BF16)
    bgt = b_in[_O_GATES:_O_HQ][:, None]
    nw = w.shape[1]
    tok = lambda width: pl.BlockSpec((1, tm, width), lambda i, j: (i, j, 0))
    full = lambda shape: pl.BlockSpec(shape, lambda i, j: (0,) * len(shape))
    return pl.pallas_call(
        functools.partial(_in_kernel, nct=nct),
        grid=(bsz, t // tm),
        in_specs=[pl.BlockSpec((1, tm, d), lambda i, j: (i, jnp.minimum(j, nct - 1), 0)),
                  pl.BlockSpec((1, tm, d), lambda i, j: (i, jnp.maximum(j - nct, 0), 0)),
                  pl.BlockSpec((1, 1, 6, d), lambda i, j: (i, jnp.where(j >= nct, 1, 0), 0, 0)),
                  full((1, d)), full((d, nw)), full((1, nw)), full((16, d)), full((16, 1))],
        out_specs=[tok(768), tok(1024), tok(512), tok(512),
                   pl.BlockSpec((1, 16, tm), lambda i, j: (i, 0, j))],
        out_shape=[jax.ShapeDtypeStruct((bsz, t, 768), F32),
                   jax.ShapeDtypeStruct((bsz, t, 1024), F32),
                   jax.ShapeDtypeStruct((bsz, t, 512), F32),
                   jax.ShapeDtypeStruct((bsz, t, 512), F32),
                   jax.ShapeDtypeStruct((bsz, 16, t), F32)],
        compiler_params=_params(("arbitrary", "arbitrary")),
        name="in_proj",
    )(ctx_x, lat_x, modtok, g1[None, :], w, b, wgt, bgt)


def _dir1_chunk(i, n_ctx_chunks, n_chunks):
    return jnp.where(i < n_ctx_chunks, n_ctx_chunks - 1 - i, n_chunks - 1 - (i - n_ctx_chunks))


def _head_tables():
    same = (np.arange(SLAB)[:, None] // CHUNK == np.arange(SLAB)[None, :] // CHUNK).astype(np.float32)
    return jnp.asarray(same, BF16), jnp.asarray(same, F32)


def _stack_heads(x_bf, same_bf):
    return jnp.concatenate([x_bf] * ML_HEADS, axis=0) * same_bf


class _Chain:
    def __init__(self, **kw):
        self.__dict__.update(kw)


def _samples_per_step(bsz):
    return 2 if bsz % 2 == 0 else 1


def _mm_split(a, b):
    hi, lo = _split(a)
    return _mm(hi, b) + _mm(lo, b)


def _mlstm_tables():
    t = np.arange(CHUNK)
    lower = (t[:, None] >= t[None, :]).astype(np.float32)
    tri = np.stack([lower, lower.T])
    neg = np.stack([np.tile(np.where(m > 0, 0.0, -np.inf), (1, ML_HEADS)) for m in (lower, lower.T)]).astype(np.float32)
    spread = np.zeros((2, LANES, ML_W), np.float32)
    for d in range(2):
        for h in range(ML_HEADS):
            spread[d, d * 8 + 4 + h, h * ML_DH:(h + 1) * ML_DH] = 1.0
    gather = spread.transpose(0, 2, 1)
    return jnp.asarray(tri, BF16), jnp.asarray(neg), jnp.asarray(spread, BF16), jnp.asarray(gather, BF16)


def _mlstm_kernel(ml_ref, gc_ref, gt_ref, cw_ref, cb_ref, tri_ref, neg_ref, spread_ref, gather_ref,
                  sameb_ref, samef_ref, o_ref, qk_s, c_s, n_s, m_s, *, t, ctx_len, tm):
    n_chunks = t // CHUNK
    n_ctx_chunks = ctx_len // CHUNK
    n_samples = ml_ref.shape[0]
    cw = cw_ref[...]
    cb = cb_ref[...]
    rid = lax.broadcasted_iota(jnp.int32, (tm, 2 * ML_W), 0)
    for b in range(n_samples):
        for j in range(t // tm):
            r0 = j * tm
            cur = ml_ref[b, r0:r0 + tm, 0:2 * ML_W]
            up = pltpu.roll(cur, 1, 0)
            if r0 in (0, ctx_len):
                up = jnp.where(rid == 0, 0.0, up)
            else:
                up = jnp.where(rid == 0, ml_ref[b, r0 - 1:r0, 0:2 * ML_W], up)
            dn = pltpu.roll(cur, tm - 1, 0)
            if r0 + tm in (ctx_len, t):
                dn = jnp.where(rid == tm - 1, 0.0, dn)
            else:
                dn = jnp.where(rid == tm - 1, ml_ref[b, r0 + tm:r0 + tm + 1, 0:2 * ML_W], dn)
            y = cw[0:1, :] * up + cw[1:2, :] * cur + cw[2:3, :] * dn + cb
            qk_s[b, r0:r0 + tm, :] = _silu(y)

    o_ref[...] = jnp.zeros_like(o_ref)
    c_s[...] = jnp.zeros_like(c_s)
    n_s[...] = jnp.zeros_like(n_s)
    m_s[...] = jnp.zeros_like(m_s)
    gate_lane = lax.broadcasted_iota(jnp.int32, (CHUNK, LANES), 1) < 16
    tok = lax.broadcasted_iota(jnp.int32, (CHUNK, LANES), 0)

    def load(ch):
        d, b = ch.d, ch.b
        ch.rows = pl.ds(pl.multiple_of(ch.c * CHUNK, CHUNK), CHUNK)
        qk = qk_s[b, ch.rows, :]
        ch.qb = qk[:, 0:ML_W].astype(BF16)
        ch.k = qk[:, ML_W:2 * ML_W] * (ML_DH ** -0.5)
        ch.vb = ml_ref[b, ch.rows, 2 * ML_W:3 * ML_W].astype(BF16)
        ch.gc = jnp.where(gate_lane, gc_ref[b, ch.rows, :], 0.0)
        ch.gt = gt_ref[b, ch.c]
        hi, lo = _split(_log_sigmoid(ch.gc))
        ch.b_c = _mm(tri_ref[d], hi) + _mm(tri_ref[d], lo)
        ch.b_r = _mm_split(_log_sigmoid(ch.gt), tri_ref[1 - d])
        ch.s_raw = _nt(ch.qb, _stack_heads(ch.k.astype(BF16), sameb_ref[...]))
        ch.qn = _mm((qk[:, 0:ML_W] * n_s[b, d][0:1, :]).astype(BF16), gather_ref[d])
        ch.c_mat = c_s[b, d]
        ch.qc = _mm(ch.qb, ch.c_mat.astype(BF16))

    def stabilise(ch):
        d = ch.d
        ch.u = pltpu.roll(ch.gc, 4, 1) - ch.b_c
        cmax = ch.u
        for step in (1, 2, 4, 8, 16, 32):
            if d == 0:
                moved = jnp.where(tok >= step, pltpu.roll(cmax, step, 0), -jnp.inf)
            else:
                moved = jnp.where(tok < CHUNK - step, pltpu.roll(cmax, CHUNK - step, 0), -jnp.inf)
            cmax = jnp.maximum(cmax, moved)
        ch.m = m_s[ch.b, d][0:1, :]
        inter = ch.b_c + ch.m
        ch.m_t = jnp.maximum(inter, ch.b_c + cmax)
        ch.w_st = jnp.exp(inter - ch.m_t)
        last = CHUNK - 1 if d == 0 else 0
        ch.bend = ch.b_c[last:last + 1, :]
        ch.m_new = jnp.maximum(ch.bend + ch.m, ch.bend + cmax[last:last + 1, :])
        fcols = [d * 8 + 4 + h for h in range(ML_HEADS)]
        icols = [d * 8 + h for h in range(ML_HEADS)]
        ch.u_row = jnp.concatenate([ch.gt[i:i + 1, :] - ch.b_r[f:f + 1, :] for i, f in zip(icols, fcols)], axis=1)
        ch.bm_wide = _mm_split(ch.b_c - ch.m_t, spread_ref[d])
        ch.wk_wide = _mm(jnp.exp(ch.bend + ch.u - ch.m_new).astype(BF16), spread_ref[d])
        ch.w_old = jnp.broadcast_to(jnp.exp(ch.bend + ch.m - ch.m_new), (8, LANES))
        ch.wold_wide = _mm_split(ch.w_old, spread_ref[d])

    def weigh(ch):
        d = ch.d
        s = ch.s_raw * jnp.exp(ch.bm_wide + ch.u_row + neg_ref[d])
        sb = s.astype(BF16)
        ch.intra = _mm(sb, _stack_heads(ch.vb, sameb_ref[...]))
        ch.rowsum = _mm(sb, gather_ref[d])
        kw = ch.k * ch.wk_wide
        ch.dc = _tn(kw.astype(BF16), ch.vb)
        ch.dn = jnp.sum(kw, axis=0, keepdims=True)

    def normalise(ch):
        d = ch.d
        den = ch.rowsum + ch.w_st * ch.qn
        r = 1.0 / jnp.maximum(jnp.abs(den), jnp.exp(-ch.m_t))
        ch.r_wide = _mm(r.astype(BF16), spread_ref[d])
        ch.wr_wide = _mm((ch.w_st * r).astype(BF16), spread_ref[d])

    def store(ch):
        b, d = ch.b, ch.d
        o_ref[b, ch.rows, :] += ch.intra * ch.r_wide + ch.qc * ch.wr_wide
        w_old = ch.wold_wide[0:1, :]
        c_s[b, d] = w_old * ch.c_mat + ch.dc * samef_ref[...]
        n_s[b, d] = jnp.broadcast_to(w_old * n_s[b, d][0:1, :] + ch.dn, (8, ML_W))
        m_s[b, d] = jnp.broadcast_to(ch.m_new, (8, LANES))

    def body(i, carry):
        chunk = (i, _dir1_chunk(i, n_ctx_chunks, n_chunks))
        chains = [_Chain(b=b, d=d, c=chunk[d]) for b in range(n_samples) for d in range(2)]
        for stage in (load, stabilise, weigh, normalise, store):
            for ch in chains:
                stage(ch)
        return carry

    lax.fori_loop(0, n_chunks, body, 0)


def _mlstm(oml, omla, gates_t, conv_w, conv_b, ctx_len, tm):
    bsz, t, _ = oml.shape
    n_chunks = t // CHUNK
    kern = functools.partial(_mlstm_kernel, t=t, ctx_len=ctx_len, tm=tm)
    consts = _mlstm_tables() + _head_tables()
    sps = _samples_per_step(bsz)
    once = dict(pipeline_mode=pl.Buffered(1))
    return pl.pallas_call(
        kern,
        grid=(bsz // sps,),
        in_specs=[pl.BlockSpec((sps, t, 768), lambda i: (i, 0, 0), **once),
                  pl.BlockSpec((sps, t, LANES), lambda i: (i, 0, 3), **once),
                  pl.BlockSpec((sps, n_chunks, 16, CHUNK), lambda i: (i, 0, 0, 0)),
                  pl.BlockSpec((3, 2 * ML_W), lambda i: (0, 0)),
                  pl.BlockSpec((1, 2 * ML_W), lambda i: (0, 0))] + [_const_spec(a) for a in consts],
        out_specs=pl.BlockSpec((sps, t, ML_W), lambda i: (i, 0, 0)),
        out_shape=jax.ShapeDtypeStruct((bsz, t, ML_W), F32),
        scratch_shapes=[pltpu.VMEM((sps, t, 2 * ML_W), F32),
                        pltpu.VMEM((sps, 2, ML_W, ML_W), F32),
                        pltpu.VMEM((sps, 2, 8, ML_W), F32),
                        pltpu.VMEM((sps, 2, 8, LANES), F32)],
        compiler_params=_params(("arbitrary",)),
        name="mlstm",
    )(oml, omla, gates_t, conv_w, conv_b[None, :], *consts)


def _hgrn_tables():
    sums = np.zeros((2, (N_MM_LEVELS + 1) * CHUNK, CHUNK), np.float32)
    pairs = np.zeros((2, N_LEVELS + 1, CHUNK, CHUNK), np.float32)
    sign = np.zeros((2, N_LEVELS - N_MM_LEVELS, CHUNK, 1), np.float32)
    for d in range(2):
        p = (lambda a: a) if d == 0 else (lambda a: CHUNK - 1 - a)
        for t in range(CHUNK):
            pairs[d, 0, p(t), p(t)] = 1.0
            for s in range(t + 1):
                sums[d, p(t), p(s)] = 1.0
        for l in range(N_LEVELS):
            w = 1 << l
            for t in range(CHUNK):
                ref = t - (t % (2 * w)) + w - 1
                if l < N_MM_LEVELS:
                    lo, hi = (ref + 1, t) if t > ref else (t + 1, ref)
                    for s in range(lo, hi + 1):
                        sums[d, (l + 1) * CHUNK + p(t), p(s)] = 1.0
                else:
                    sign[d, l - N_MM_LEVELS, p(t), 0] = 1.0 if t > ref else -1.0
                if t > ref:
                    for s in range(ref - w + 1, ref + 1):
                        pairs[d, l + 1, p(t), p(s)] = 1.0
    pairs = np.tile(pairs, (1, 1, 1, HG_HEADS))
    sign = np.broadcast_to(sign, sign.shape[:3] + (HG_W,))
    return jnp.asarray(sums, BF16), jnp.asarray(sign), jnp.asarray(pairs)


def _hgrn_ref_rows(cum, d, l):
    w = 1 << l
    parts = []
    for base in range(0, CHUNK, 2 * w):
        r = base + w - 1 if d == 0 else base + w
        parts.append(jnp.broadcast_to(cum[r:r + 1, :], (2 * w, cum.shape[1])))
    return parts[0] if len(parts) == 1 else jnp.concatenate(parts, axis=0)


def _hgrn_kernel(hg_ref, lbl_ref, sum_ref, sign_ref, pair_ref, sameb_ref, samef_ref, o_ref, st_s,
                 *, t, ctx_len, layer):
    n_chunks = t // CHUNK
    n_ctx_chunks = ctx_len // CHUNK
    w_all = HG_W

    if layer > 0:
        logits = lbl_ref[...]
        n_layers = logits.shape[0]
        mx = logits[0]
        for l in range(1, n_layers):
            mx = jnp.maximum(mx, logits[l])
        ex = [jnp.exp(logits[l] - mx) for l in range(n_layers)]
        tot = ex[0]
        for l in range(1, n_layers):
            tot = tot + ex[l]
        low_all = ex[1] / tot
        for l in range(2, layer + 1):
            low_all = low_all + ex[l] / tot
        log_low = jnp.log(low_all)
        log_1m_low = jnp.log1p(-low_all)

    o_ref[...] = jnp.zeros_like(o_ref)
    st_s[...] = jnp.zeros_like(st_s)

    def gates(ch):
        b, d = ch.b, ch.d
        ch.rows = pl.ds(pl.multiple_of(ch.c * CHUNK, CHUNK), CHUNK)
        ch.q = _silu(hg_ref[b, ch.rows, 0:w_all])
        ch.vb = hg_ref[b, ch.rows, w_all:2 * w_all].astype(BF16)
        pre = hg_ref[b, ch.rows, (2 + d) * w_all:(3 + d) * w_all]
        ls = _log_sigmoid(pre)
        if layer == 0:
            log_f = ls
            ch.key = jnp.exp(ls - pre)
        else:
            lo_, hi_ = log_low[d:d + 1, :], log_1m_low[d:d + 1, :] + ls
            log_f = jnp.maximum(lo_, hi_) + jnp.log1p(jnp.exp(-jnp.abs(lo_ - hi_)))
            ch.key = (1.0 - low_all[d:d + 1, :]) * jnp.exp(ls - pre)
        hi, lo = _split(log_f)
        ch.xb = _mm(sum_ref[d], hi) + _mm(sum_ref[d], lo)
        ch.st = st_s[b, d]
        ch.amat = _nt(ch.q.astype(BF16), _stack_heads(ch.key.astype(BF16), sameb_ref[...])) * pair_ref[d, 0]

    def levels(ch):
        d = ch.d
        cum = ch.xb[0:CHUNK]
        for l in range(N_LEVELS):
            if l < N_MM_LEVELS:
                expo = ch.xb[(l + 1) * CHUNK:(l + 2) * CHUNK]
            else:
                expo = (cum - _hgrn_ref_rows(cum, d, l)) * sign_ref[d, l - N_MM_LEVELS]
            fac = jnp.exp(expo)
            qt = (ch.q * fac).astype(BF16)
            kt = (ch.key * fac).astype(BF16)
            ch.amat = ch.amat + _nt(qt, _stack_heads(kt, sameb_ref[...])) * pair_ref[d, l + 1]
        last = CHUNK - 1 if d == 0 else 0
        bend = cum[last:last + 1, :]
        ch.inter = _nt((ch.q * jnp.exp(cum)).astype(BF16), ch.st.astype(BF16))
        ch.dst = _tn(ch.vb, (ch.key * jnp.exp(bend - cum)).astype(BF16))
        ch.decay = jnp.exp(bend)

    def readout(ch):
        ch.intra = _mm(ch.amat.astype(BF16), _stack_heads(ch.vb, sameb_ref[...]))

    def store(ch):
        o_ref[ch.b, ch.rows, :] += ch.intra + ch.inter
        st_s[ch.b, ch.d] = ch.decay * ch.st + ch.dst * samef_ref[...]

    def body(i, carry):
        chunk = (i, _dir1_chunk(i, n_ctx_chunks, n_chunks))
        chains = [_Chain(b=b, d=d, c=chunk[d]) for b in range(hg_ref.shape[0]) for d in range(2)]
        for stage in (gates, levels, readout, store):
            for ch in chains:
                stage(ch)
        return carry

    lax.fori_loop(0, n_chunks, body, 0)


def _hgrn(ohg, lb_logits, ctx_len, layer):
    bsz, t, _ = ohg.shape
    kern = functools.partial(_hgrn_kernel, t=t, ctx_len=ctx_len, layer=layer)
    consts = (lb_logits,) + _hgrn_tables() + _head_tables()
    sps = _samples_per_step(bsz)
    return pl.pallas_call(
        kern,
        grid=(bsz // sps,),
        in_specs=[pl.BlockSpec((sps, t, 4 * HG_W), lambda i: (i, 0, 0), pipeline_mode=pl.Buffered(1))]
        + [_const_spec(a) for a in consts],
        out_specs=pl.BlockSpec((sps, t, HG_W), lambda i: (i, 0, 0)),
        out_shape=jax.ShapeDtypeStruct((bsz, t, HG_W), F32),
        scratch_shapes=[pltpu.VMEM((sps, 2, HG_W, HG_W), F32)],
        compiler_params=_params(("arbitrary",)),
        name="hgrn",
    )(ohg, *consts)


def _mla_prep_kernel(x_ref, gq_ref, gkv_ref, wq_ref, wqr_ref, wk_ref, wv_ref, gqq_ref, gqr_ref, gkk_ref, gkr_ref,
                     cos_ref, sin_ref, q_ref, k_ref, v_ref):
    x = x_ref[0]
    cq = _rms_rows(x[:, 0:MLA_Q_RANK], gq_ref[...]).astype(BF16)
    ckv = _rms_rows(x[:, MLA_Q_RANK:MLA_Q_RANK + MLA_KV_RANK], gkv_ref[...]).astype(BF16)
    misc = x[:, MLA_Q_RANK + MLA_KV_RANK:]
    lane = lax.broadcasted_iota(jnp.int32, misc.shape, 1)
    rope_lane = (lane >= MLA_NOPE) & (lane < MLA_DQK)
    k_rope = jnp.where(rope_lane, misc, 0.0)
    k_rope_rot = jnp.where(rope_lane, pltpu.roll(misc, HEAD_PAD - MLA_ROPE, 1), 0.0)
    q_raw = _mm(cq, wq_ref[...])
    q_rot = _mm(cq, wqr_ref[...])
    k_raw = _mm(ckv, wk_ref[...])
    v_all = _mm(ckv, wv_ref[...]).astype(BF16)
    cos, sin = cos_ref[...], sin_ref[...]

    def norm_rope(xh, xr, g, gr):
        ms = jnp.sum(xh * xh, axis=-1, keepdims=True) * (1.0 / MLA_DQK)
        return (xh * (g * cos) + xr * (gr * sin)) * lax.rsqrt(ms + EPS)

    for h in range(MLA_HEADS):
        sl = slice(h * HEAD_PAD, (h + 1) * HEAD_PAD)
        q = norm_rope(q_raw[:, sl], q_rot[:, sl], gqq_ref[...], gqr_ref[...])
        q_ref[0, h] = (q * (MLA_DQK ** -0.5)).astype(BF16)
        k_ref[0, h] = norm_rope(k_raw[:, sl] + k_rope, k_rope_rot, gkk_ref[...], gkr_ref[...]).astype(BF16)
        v_ref[0, h] = v_all[:, h * MLA_DV:(h + 1) * MLA_DV]


def _rotate_half_columns(n):
    j = np.arange(n)
    first = j % 16 < 8
    return np.where(first, j + 8, j - 8), np.where(first, -1.0, 1.0).astype(np.float32)


def _mla_prep(omla, gq, gkv, w_uq, w_ukv, g_qq, g_kk, cos, sin, tm):
    bsz, t, _ = omla.shape
    hw = MLA_HEADS * HEAD_PAD
    src, sign = _rotate_half_columns(MLA_ROPE)
    src_pad = np.arange(HEAD_PAD)
    src_pad[MLA_NOPE:MLA_DQK] = MLA_NOPE + src
    sign_pad = np.zeros((HEAD_PAD,), np.float32)
    sign_pad[MLA_NOPE:MLA_DQK] = sign
    wq = w_uq.reshape(MLA_Q_RANK, MLA_HEADS, MLA_DQK)
    wq = jnp.pad(wq, ((0, 0), (0, 0), (0, HEAD_PAD - MLA_DQK)))
    wq_rot = (wq[:, :, src_pad] * sign_pad).reshape(MLA_Q_RANK, hw).astype(BF16)
    wq = wq.reshape(MLA_Q_RANK, hw).astype(BF16)
    wkv = w_ukv.reshape(MLA_KV_RANK, MLA_HEADS, MLA_NOPE + MLA_DV)
    wk = jnp.pad(wkv[:, :, :MLA_NOPE], ((0, 0), (0, 0), (0, HEAD_PAD - MLA_NOPE))).reshape(MLA_KV_RANK, hw).astype(BF16)
    wv = wkv[:, :, MLA_NOPE:].reshape(MLA_KV_RANK, MLA_HEADS * MLA_DV).astype(BF16)
    padg = lambda g: jnp.pad(g, (0, HEAD_PAD - MLA_DQK))
    gains = [padg(g_qq)[None, :], (padg(g_qq)[src_pad] * jnp.abs(sign_pad))[None, :],
             padg(g_kk)[None, :], (padg(g_kk)[src_pad] * jnp.abs(sign_pad))[None, :]]
    full = lambda shape: pl.BlockSpec(shape, lambda i, j: (0,) * len(shape))
    rope_spec = pl.BlockSpec((tm, HEAD_PAD), lambda i, j: (j, 0))
    return pl.pallas_call(
        _mla_prep_kernel,
        grid=(bsz, t // tm),
        in_specs=[pl.BlockSpec((1, tm, 512), lambda i, j: (i, j, 0)),
                  full((1, MLA_Q_RANK)), full((1, MLA_KV_RANK)),
                  full((MLA_Q_RANK, hw)), full((MLA_Q_RANK, hw)), full((MLA_KV_RANK, hw)),
                  full((MLA_KV_RANK, MLA_HEADS * MLA_DV))] + [full((1, HEAD_PAD))] * 4 + [rope_spec, rope_spec],
        out_specs=[pl.BlockSpec((1, MLA_HEADS, tm, HEAD_PAD), lambda i, j: (i, 0, j, 0)),
                   pl.BlockSpec((1, MLA_HEADS, tm, HEAD_PAD), lambda i, j: (i, 0, j, 0)),
                   pl.BlockSpec((1, MLA_HEADS, tm, MLA_DV), lambda i, j: (i, 0, j, 0))],
        out_shape=[jax.ShapeDtypeStruct((bsz, MLA_HEADS, t, HEAD_PAD), BF16),
                   jax.ShapeDtypeStruct((bsz, MLA_HEADS, t, HEAD_PAD), BF16),
                   jax.ShapeDtypeStruct((bsz, MLA_HEADS, t, MLA_DV), BF16)],
        compiler_params=_params(("arbitrary", "arbitrary")),
        name="mla_prep",
    )(omla, gq[None, :], gkv[None, :], wq, wq_rot, wk, wv, *gains, cos, sin)


def _attn_kernel(q_ref, k_ref, v_ref, o_ref, *, ctx_len, nct, j0):
    is_ctx = pl.program_id(1) + j0 < nct

    def attend(n_keys):
        scores = lambda h: _nt(q_ref[0, h], k_ref[0, h, 0:n_keys, :])
        s_next = scores(0)
        for h in range(MLA_HEADS):
            s, s_next = s_next, (scores(h + 1) if h + 1 < MLA_HEADS else None)
            p = jnp.exp(s - jnp.max(s, axis=-1, keepdims=True))
            o = _mm(p.astype(BF16), v_ref[0, h, 0:n_keys, :]) / jnp.sum(p, axis=-1, keepdims=True)
            o_ref[0, :, h * MLA_DV:(h + 1) * MLA_DV] = o

    @pl.when(is_ctx)
    def _():
        attend(ctx_len)

    @pl.when(jnp.logical_not(is_ctx))
    def _():
        attend(k_ref.shape[2])


def _attention(q, k, v, ctx_len, tm, j0):
    bsz, _, t, _ = q.shape
    nct = ctx_len // tm
    nq = t // tm - j0
    kern = functools.partial(_attn_kernel, ctx_len=ctx_len, nct=nct, j0=j0)
    return pl.pallas_call(
        kern,
        grid=(bsz, nq),
        in_specs=[pl.BlockSpec((1, MLA_HEADS, tm, HEAD_PAD), lambda i, j: (i, 0, j + j0, 0)),
                  pl.BlockSpec((1, MLA_HEADS, t, HEAD_PAD), lambda i, j: (i, 0, 0, 0)),
                  pl.BlockSpec((1, MLA_HEADS, t, MLA_DV), lambda i, j: (i, 0, 0, 0))],
        out_specs=pl.BlockSpec((1, tm, MLA_HEADS * MLA_DV), lambda i, j: (i, j, 0)),
        out_shape=jax.ShapeDtypeStruct((bsz, nq * tm, MLA_HEADS * MLA_DV), F32),
        compiler_params=_params(("arbitrary", "arbitrary")),
        name="attention",
    )(q, k, v)


def _merge_kernel(ml_ref, hg_ref, at_ref, gate_ref, x_ref, mod_ref, gml_ref, ghg_ref, wo_ref,
                  g2_ref, rwt_ref, same_ref, x1_ref, h2_ref, aff_ref):
    head_mean = same_ref[...] * (1.0 / ML_DH)
    w_hi, w_lo = _split(rwt_ref[...])
    half = x_ref.shape[1] // 2

    def mean_squares(ch):
        ch.ml, ch.hg = ml_ref[0, ch.rows, :], hg_ref[0, ch.rows, :]
        hi, lo = _split(ch.ml * ch.ml)
        ch.ms_ml = _mm(hi, head_mean) + _mm(lo, head_mean)
        hi, lo = _split(ch.hg * ch.hg)
        ch.ms_hg = _mm(hi, head_mean) + _mm(lo, head_mean)

    def project(ch):
        gates = gate_ref[0, ch.rows, :]
        y_ml = ch.ml * lax.rsqrt(ch.ms_ml + EPS) * gml_ref[...] * jax.nn.sigmoid(gates[:, 0:ML_W])
        y_hg = ch.hg * lax.rsqrt(ch.ms_hg + EPS) * ghg_ref[...] * _silu(gates[:, ML_W:ML_W + HG_W])
        ch.y = (_mm(y_ml.astype(BF16), wo_ref[0:ML_W, :])
                + _mm(y_hg.astype(BF16), wo_ref[ML_W:ML_W + HG_W, :])
                + _mm(at_ref[0, ch.rows, :].astype(BF16), wo_ref[ML_W + HG_W:, :]))

    def route(ch):
        x1 = x_ref[0, ch.rows, :] + mod_ref[0, 0, 2:3, :] * ch.y
        x1_ref[0, ch.rows, :] = x1
        h2 = _rms_rows(x1, g2_ref[...]) * (1.0 + mod_ref[0, 0, 4:5, :]) + mod_ref[0, 0, 3:4, :]
        h2_ref[0, ch.rows, :] = h2.astype(BF16)
        h_hi, h_lo = _split(h2)
        ch.logits = _nt(w_hi, h_hi) + _nt(w_hi, h_lo) + _nt(w_lo, h_hi)

    def affinity(ch):
        e = jnp.exp(ch.logits - jnp.max(ch.logits, axis=0, keepdims=True))
        aff_ref[0, :, ch.rows] = e / jnp.sum(e, axis=0, keepdims=True)

    chains = [_Chain(rows=slice(i * half, (i + 1) * half)) for i in range(2)]
    for stage in (mean_squares, project, route, affinity):
        for ch in chains:
            stage(ch)


def _merge(ml, hg, at, ogate, xs, modtok, g_ml, g_hg, w_out, g2, router_w, tm, j0, sel, at_j0):
    bsz, n, d = xs.shape
    n_tiles = n // tm
    tok = lambda width, off: pl.BlockSpec((1, tm, width), lambda i, j: (i, j + off, 0))
    full = lambda shape: pl.BlockSpec(shape, lambda i, j: (0,) * len(shape))
    same_b, _ = _head_tables()
    return pl.pallas_call(
        _merge_kernel,
        grid=(bsz, n_tiles),
        in_specs=[tok(ML_W, j0), tok(HG_W, j0), tok(512, j0 - at_j0), tok(512, j0), tok(d, 0),
                  pl.BlockSpec((1, 1, 6, d), lambda i, j: (i, sel, 0, 0)),
                  full((1, ML_W)), full((1, HG_W)), full((d, d)), full((1, d)), full((N_EXPERTS, d)),
                  _const_spec(same_b)],
        out_specs=[tok(d, 0), tok(d, 0), pl.BlockSpec((1, N_EXPERTS, tm), lambda i, j: (i, 0, j))],
        out_shape=[jax.ShapeDtypeStruct((bsz, n, d), F32),
                   jax.ShapeDtypeStruct((bsz, n, d), BF16),
                   jax.ShapeDtypeStruct((bsz, N_EXPERTS, n), F32)],
        compiler_params=_params(("arbitrary", "arbitrary")),
        name="merge",
    )(ml, hg, at, ogate, xs, modtok, g_ml[None, :], g_hg[None, :], w_out.astype(BF16), g2[None, :],
      router_w.T, same_b)


def _route_kernel(aff_ref, rank_ref, *, cap, n):
    aff = aff_ref[0]

    def search(i, thr):
        cand = thr | (jnp.int32(1) << (30 - i))
        cnt = jnp.sum(jnp.where(aff >= pltpu.bitcast(cand, F32), 1, 0), axis=-1, keepdims=True)
        return jnp.where(cnt >= cap, cand, thr)

    thr = lax.fori_loop(0, 31, search, jnp.zeros((N_EXPERTS, 1), jnp.int32))
    above = aff >= pltpu.bitcast(thr + 1, F32)
    tied = jnp.logical_and(aff >= pltpu.bitcast(thr, F32), jnp.logical_not(above))
    need = cap - jnp.sum(jnp.where(above, 1, 0), axis=-1, keepdims=True)

    tl = min(n, 256)
    r = lax.broadcasted_iota(jnp.int32, (tl, tl), 0)
    c = lax.broadcasted_iota(jnp.int32, (tl, tl), 1)
    before = jnp.where(r < c, 1.0, 0.0).astype(BF16)

    def excl_cumsum(mask):
        parts, carry = [], jnp.zeros((N_EXPERTS, 1), F32)
        for j in range(n // tl):
            m = jnp.where(mask[:, j * tl:(j + 1) * tl], 1.0, 0.0)
            parts.append(_mm(m.astype(BF16), before) + carry)
            carry = carry + jnp.sum(m, axis=-1, keepdims=True)
        return jnp.concatenate(parts, axis=-1).astype(jnp.int32)

    keep = jnp.logical_or(above, jnp.logical_and(tied, excl_cumsum(tied) < need))
    rank_ref[0] = jnp.where(keep, excl_cumsum(keep), -1)


def _route(aff_t, cap):
    bsz, e, n = aff_t.shape
    kern = functools.partial(_route_kernel, cap=cap, n=n)
    return pl.pallas_call(
        kern,
        grid=(bsz,),
        in_specs=[pl.BlockSpec((1, e, n), lambda i: (i, 0, 0))],
        out_specs=pl.BlockSpec((1, e, n), lambda i: (i, 0, 0)),
        out_shape=jax.ShapeDtypeStruct((bsz, e, n), jnp.int32),
        compiler_params=_params(("arbitrary",)),
        name="route",
    )(aff_t)


def _pick_and_gate(rank_row, aff_row, cap):
    slot = lax.broadcasted_iota(jnp.int32, (cap, rank_row.shape[1]), 0)
    chosen = rank_row == slot
    gate = jnp.sum(jnp.where(chosen, aff_row, 0.0), axis=-1, keepdims=True)
    return jnp.where(chosen, 1.0, 0.0).astype(BF16), gate


def _ffn_experts(xs, picks, gates, x1_ref, g2_ref, wg_ref, wu_ref, wd_ref, o_ref, cap):
    e = pl.program_id(1)
    hid = (_silu(_mm(xs, wg_ref[0, 0])) * _mm(xs, wu_ref[0, 0])).astype(BF16)
    out = _mm(hid, wd_ref[0, 0])

    @pl.when(e == 0)
    def _():
        o_ref[...] = jnp.zeros_like(o_ref)

    d = out.shape[1]
    for b, (pick, gate) in enumerate(zip(picks, gates)):
        out_b = (out[b * cap:(b + 1) * cap] * gate).astype(BF16)
        for j in range(d // 256):
            o_ref[b, :, j * 256:(j + 1) * 256] += _tn(pick, out_b[:, j * 256:(j + 1) * 256])

    @pl.when(e == pl.num_programs(1) - 1)
    def _():
        o_ref[...] = x1_ref[...] + g2_ref[...] * o_ref[...]


def _ffn_kernel(rank_ref, aff_ref, h_ref, x1_ref, g2_ref, wg_ref, wu_ref, wd_ref, o_ref, *, cap, bb):
    picks, gates, xs = [], [], []
    for b in range(bb):
        pick, gate = _pick_and_gate(rank_ref[b, 0], aff_ref[b, 0], cap)
        picks.append(pick)
        gates.append(gate)
        xs.append(_mm(pick, h_ref[b].astype(BF16)).astype(BF16))
    _ffn_experts(jnp.concatenate(xs, axis=0), picks, gates, x1_ref, g2_ref, wg_ref, wu_ref, wd_ref, o_ref, cap)


def _ffn(rank, aff_t, h2, x1, g2mod, wg, wu, wd, layer, cap, bb):
    bsz, n, d = h2.shape
    n_exp = wg.shape[1]
    once = dict(pipeline_mode=pl.Buffered(1))
    per_expert = lambda width: pl.BlockSpec((bb, 1, 1, width), lambda i, e: (i, e, 0, 0))
    common = [pl.BlockSpec((bb, n, d), lambda i, e: (i, 0, 0), **once),
              pl.BlockSpec((bb, 1, d), lambda i, e: (i, 0, 0)),
              pl.BlockSpec((1, 1, d, d), lambda i, e: (layer, e, 0, 0)),
              pl.BlockSpec((1, 1, d, d), lambda i, e: (layer, e, 0, 0)),
              pl.BlockSpec((1, 1, d, d), lambda i, e: (layer, e, 0, 0))]
    out_spec = pl.BlockSpec((bb, n, d), lambda i, e: (i, 0, 0))
    out_shape = jax.ShapeDtypeStruct((bsz, n, d), F32)
    rank4, aff4 = rank.reshape(bsz, n_exp, 1, n), aff_t.reshape(bsz, n_exp, 1, n)
    return pl.pallas_call(
        functools.partial(_ffn_kernel, cap=cap, bb=bb),
        grid=(bsz // bb, n_exp),
        in_specs=[per_expert(n), per_expert(n), pl.BlockSpec((bb, n, d), lambda i, e: (i, 0, 0), **once)] + common,
        out_specs=out_spec, out_shape=out_shape,
        compiler_params=_params(("arbitrary", "arbitrary")),
        name="expert_ffn",
    )(rank4, aff4, h2, x1, g2mod, wg, wu, wd)


def _rope_tables(ctx_len, seq):
    half = MLA_ROPE // 2
    inv = ROPE_THETA ** (-jnp.arange(0, half, 2, dtype=F32) / half)
    rows = seq // GRID_W
    row_pos = jnp.repeat(jnp.arange(rows), GRID_W).astype(F32)
    col_pos = jnp.broadcast_to(jnp.arange(GRID_W), (rows, GRID_W)).reshape(-1).astype(F32)

    def cs(pos):
        ang = pos[:, None] * inv[None, :]
        ang = jnp.concatenate([ang, ang], axis=-1)
        return jnp.cos(ang), jnp.sin(ang)

    cos_r, sin_r = cs(row_pos)
    cos_c, sin_c = cs(col_pos)
    zeros = lambda n: jnp.zeros((seq, n), F32)
    ones = lambda n: jnp.ones((seq, n), F32)
    tail = HEAD_PAD - MLA_DQK
    cos = jnp.concatenate([ones(MLA_NOPE), cos_r, cos_c, ones(tail)], axis=-1)
    sin = jnp.concatenate([zeros(MLA_NOPE), sin_r, sin_c, zeros(tail)], axis=-1)
    ident = lambda a, fill: jnp.concatenate([jnp.full((ctx_len, HEAD_PAD), fill, F32), a], axis=0)
    return ident(cos, 1.0), ident(sin, 0.0)


def kernel(x, c, ctx, c_ctx, ada_w, ada_b, norm1_g, norm2_g, w_in, b_in, ml_conv_w, ml_conv_b, ml_norm_g, hg_lb_logits, hg_norm_g, mla_q_norm_g, mla_w_uq, mla_kv_norm_g, mla_w_ukv, mla_q_qk_g, mla_k_qk_g, w_out, router_w, ex_w_gate, ex_w_up, ex_w_down):
    bsz, seq, d = x.shape
    ctx_len = ctx.shape[1]
    depth = ada_w.shape[0]
    t = ctx_len + seq
    tm = 256 if ctx_len % 256 == 0 else 128
    assert ctx_len % tm == 0 and seq % tm == 0 and ctx_len % CHUNK == 0 and seq % GRID_W == 0
    nct = ctx_len // tm
    n_chunks = t // CHUNK

    rows = -(-(bsz + 1) // 8) * 8
    cc = jnp.concatenate([c, c_ctx[None, :], jnp.zeros((rows - bsz - 1, d), F32)], axis=0)
    mod = _modulation(cc, ada_w, ada_b)
    cos, sin = _rope_tables(ctx_len, seq)
    bb_ctx = max(bb for bb in (8, 4, 2, 1) if bsz % bb == 0)

    wg, wu, wd = (w.astype(BF16) for w in (ex_w_gate, ex_w_up, ex_w_down))
    lat = x
    for layer in range(depth):
        need_ctx = layer < depth - 1
        m = mod[layer]
        modtok = jnp.stack([jnp.broadcast_to(m[bsz], (bsz, 6 * d)), m[:bsz]], axis=1).reshape(bsz, 2, 6, d)
        oml, ohg, ogate, omla, ogt = _in_proj(ctx, lat, modtok, norm1_g[layer], w_in[layer], b_in[layer], tm, nct)
        gates_t = ogt.reshape(bsz, 16, n_chunks, CHUNK).transpose(0, 2, 1, 3)
        ml = _mlstm(oml, omla, gates_t, ml_conv_w[layer], ml_conv_b[layer], ctx_len, tm)
        hg = _hgrn(ohg, hg_lb_logits, ctx_len, layer)
        q, k, v = _mla_prep(omla, mla_q_norm_g[layer], mla_kv_norm_g[layer], mla_w_uq[layer], mla_w_ukv[layer],
                            mla_q_qk_g[layer], mla_k_qk_g[layer], cos, sin, tm)
        at_j0 = 0 if need_ctx else nct
        at = _attention(q, k, v, ctx_len, tm, at_j0)
        g2mod = modtok[:, :, 5:6, :]

        def post(xs, j0, sel, bb):
            x1, h2, aff = _merge(ml, hg, at, ogate, xs, modtok, ml_norm_g[layer], hg_norm_g[layer], w_out[layer],
                                 norm2_g[layer], router_w[layer], tm, j0, sel, at_j0)
            cap = EC_CAPACITY * xs.shape[1] // N_EXPERTS
            rank = _route(aff, cap)
            return _ffn(rank, aff, h2, x1, g2mod[:, sel], wg, wu, wd, layer, cap, bb)

        new_lat = post(lat, nct, 1, 1)
        if need_ctx:
            ctx = post(ctx, 0, 0, bb_ctx)
        lat = new_lat
    return lat
```

```python
import functools

import numpy as np
import jax
import jax.numpy as jnp
from jax import lax
from jax.experimental import pallas as pl
from jax.experimental.pallas import tpu as pltpu

F32 = jnp.float32
BF16 = jnp.bfloat16
HIGHEST = lax.Precision.HIGHEST

D_MODEL = 1024
GRID_W = 64
ML_HEADS = 4
ML_DH = 64
ML_W = 256
HG_HEADS = 4
HG_W = 256
MLA_HEADS = 8
MLA_NOPE = 64
MLA_ROPE = 32
MLA_DQK = 96
MLA_DV = 64
MLA_Q_RANK = 256
MLA_KV_RANK = 128
N_EXPERTS = 16
EC_CAPACITY = 2
CHUNK = 64
ROPE_THETA = 10000.0
EPS = 1e-6
HEAD_PAD = 128
LANES = 128
N_LEVELS = 6
N_MM_LEVELS = 3
SLAB = ML_HEADS * CHUNK
VMEM_LIMIT = 52 * 1024 * 1024

_O_QK, _O_V, _O_OG, _O_GATES, _O_HQ, _O_HI, _O_HGATE, _O_HF, _O_CQ, _O_CKV, _O_KR, _O_END = (
    0, 512, 768, 1024, 1040, 1296, 1552, 1808, 2320, 2576, 2704, 2736)


def _nt(a, b, **kw):
    return lax.dot_general(a, b, (((1,), (1,)), ((), ())), preferred_element_type=F32, **kw)


def _tn(a, b, **kw):
    return lax.dot_general(a, b, (((0,), (0,)), ((), ())), preferred_element_type=F32, **kw)


def _mm(a, b, **kw):
    return jnp.dot(a, b, preferred_element_type=F32, **kw)


def _split(x):
    hi = x.astype(BF16)
    return hi, (x - hi.astype(F32)).astype(BF16)


def _silu(x):
    return x * jax.nn.sigmoid(x)


def _log_sigmoid(x):
    return jnp.minimum(x, 0.0) - jnp.log1p(jnp.exp(-jnp.abs(x)))


def _rms_rows(x, g):
    return x * lax.rsqrt(jnp.mean(x * x, axis=-1, keepdims=True) + EPS) * g


def _params(sem=None):
    return pltpu.CompilerParams(dimension_semantics=sem, vmem_limit_bytes=VMEM_LIMIT)


def _const_spec(a):
    return pl.BlockSpec(a.shape, lambda *_: (0,) * a.ndim)


def _mod_kernel(c_ref, w_ref, b_ref, o_ref):
    s = _silu(c_ref[...]).astype(BF16)
    o_ref[0] = _mm(s, w_ref[0].astype(BF16)) + b_ref[0]


def _modulation(cc, ada_w, ada_b):
    n_layers, d, n6 = ada_w.shape
    rows = cc.shape[0]
    tn = 1024
    return pl.pallas_call(
        _mod_kernel,
        grid=(n_layers, n6 // tn),
        in_specs=[pl.BlockSpec((rows, d), lambda l, j: (0, 0)),
                  pl.BlockSpec((1, d, tn), lambda l, j: (l, 0, j)),
                  pl.BlockSpec((1, 1, tn), lambda l, j: (l, 0, j))],
        out_specs=pl.BlockSpec((1, rows, tn), lambda l, j: (l, 0, j)),
        out_shape=jax.ShapeDtypeStruct((n_layers, rows, n6), F32),
        compiler_params=_params(("arbitrary", "arbitrary")),
        name="modulation",
    )(cc, ada_w, ada_b.reshape(n_layers, 1, n6))


def _in_kernel(c_ref, l_ref, mod_ref, g_ref, w_ref, b_ref, wgt_ref, bgt_ref,
               oml_ref, ohqv_ref, ohf_ref, ogate_ref, omla_ref, ogt_ref, *, nct):
    x = jnp.where(pl.program_id(1) < nct, c_ref[0], l_ref[0])
    sh = mod_ref[0, 0, 0:1, :]
    sc = mod_ref[0, 0, 1:2, :]
    h = (_rms_rows(x, g_ref[...]) * (1.0 + sc) + sh).astype(BF16)
    oml_ref[0] = (_mm(h, w_ref[:, 0:768]) + b_ref[:, 0:768]).astype(BF16)
    ohqv_ref[0] = (_mm(h, w_ref[:, 768:1280]) + b_ref[:, 768:1280]).astype(BF16)
    ohf_ref[0] = _mm(h, w_ref[:, 1280:1792]) + b_ref[:, 1280:1792]
    ogate_ref[0] = _mm(h, w_ref[:, 1792:2304]) + b_ref[:, 1792:2304]
    omla_ref[0] = _mm(h, w_ref[:, 2304:2816]) + b_ref[:, 2304:2816]
    ogt_ref[0] = _nt(wgt_ref[...], h) + bgt_ref[...]


def _in_proj(ctx_x, lat_x, modtok, g1, w_in, b_in, tm, nct):
    bsz, seq, d = lat_x.shape
    t = ctx_x.shape[1] + seq
    src, sign = _rotate_half_columns(MLA_ROPE)
    zeros = lambda n: jnp.zeros((d, n), F32)
    w = jnp.concatenate([
        w_in[:, _O_QK:_O_OG], w_in[:, _O_HQ:_O_HGATE], w_in[:, _O_HF:_O_CQ],
        w_in[:, _O_OG:_O_GATES], w_in[:, _O_HGATE:_O_HF], w_in[:, _O_CQ:_O_KR],
        w_in[:, _O_GATES:_O_HQ], zeros(48), w_in[:, _O_KR:_O_END],
        w_in[:, _O_KR:_O_END][:, src] * sign], axis=1).astype(BF16)
    zb = lambda n: jnp.zeros((n,), F32)
    b = jnp.concatenate([
        b_in[_O_QK:_O_OG], b_in[_O_HQ:_O_HGATE], b_in[_O_HF:_O_CQ],
        b_in[_O_OG:_O_GATES], b_in[_O_HGATE:_O_HF], b_in[_O_CQ:_O_KR],
        b_in[_O_GATES:_O_HQ], zb(48), b_in[_O_KR:_O_END], b_in[_O_KR:_O_END][src] * sign])[None, :]
    wgt = w_in[:, _O_GATES:_O_HQ].T.astype(BF16)
    bgt = b_in[_O_GATES:_O_HQ][:, None]
    nw = w.shape[1]
    tok = lambda width: pl.BlockSpec((1, tm, width), lambda i, j: (i, j, 0))
    full = lambda shape: pl.BlockSpec(shape, lambda i, j: (0,) * len(shape))
    return pl.pallas_call(
        functools.partial(_in_kernel, nct=nct),
        grid=(bsz, t // tm),
        in_specs=[pl.BlockSpec((1, tm, d), lambda i, j: (i, jnp.minimum(j, nct - 1), 0)),
                  pl.BlockSpec((1, tm, d), lambda i, j: (i, jnp.maximum(j - nct, 0), 0)),
                  pl.BlockSpec((1, 1, 6, d), lambda i, j: (i, jnp.where(j >= nct, 1, 0), 0, 0)),
                  full((1, d)), full((d, nw)), full((1, nw)), full((16, d)), full((16, 1))],
        out_specs=[tok(768), tok(512), tok(512), tok(512), tok(512),
                   pl.BlockSpec((1, 16, tm), lambda i, j: (i, 0, j))],
        out_shape=[jax.ShapeDtypeStruct((bsz, t, 768), BF16),
                   jax.ShapeDtypeStruct((bsz, t, 512), BF16),
                   jax.ShapeDtypeStruct((bsz, t, 512), F32),
                   jax.ShapeDtypeStruct((bsz, t, 512), F32),
                   jax.ShapeDtypeStruct((bsz, t, 512), F32),
                   jax.ShapeDtypeStruct((bsz, 16, t), F32)],
        compiler_params=_params(("arbitrary", "arbitrary")),
        name="in_proj",
    )(ctx_x, lat_x, modtok, g1[None, :], w, b, wgt, bgt)


def _dir1_chunk(i, n_ctx_chunks, n_chunks):
    return jnp.where(i < n_ctx_chunks, n_ctx_chunks - 1 - i, n_chunks - 1 - (i - n_ctx_chunks))


def _head_tables():
    same = (np.arange(SLAB)[:, None] // CHUNK == np.arange(SLAB)[None, :] // CHUNK).astype(np.float32)
    return jnp.asarray(same, BF16), jnp.asarray(same, F32)


def _stack_heads(x_bf, same_bf):
    return jnp.concatenate([x_bf] * ML_HEADS, axis=0) * same_bf


class _Chain:
    def __init__(self, **kw):
        self.__dict__.update(kw)


def _samples_per_step(bsz):
    return 2 if bsz % 2 == 0 else 1


def _mm_split(a, b):
    hi, lo = _split(a)
    return _mm(hi, b) + _mm(lo, b)


def _mlstm_tables():
    t = np.arange(CHUNK)
    lower = (t[:, None] >= t[None, :]).astype(np.float32)
    tri = np.stack([lower, lower.T])
    neg = np.stack([np.tile(np.where(m > 0, 0.0, -np.inf), (1, ML_HEADS)) for m in (lower, lower.T)]).astype(np.float32)
    spread = np.zeros((2, LANES, ML_W), np.float32)
    for d in range(2):
        for h in range(ML_HEADS):
            spread[d, d * 8 + 4 + h, h * ML_DH:(h + 1) * ML_DH] = 1.0
    gather = spread.transpose(0, 2, 1)
    return jnp.asarray(tri, BF16), jnp.asarray(neg), jnp.asarray(spread, BF16), jnp.asarray(gather, BF16)


def _mlstm_kernel(ml_ref, gc_ref, gt_ref, cw_ref, cb_ref, tri_ref, neg_ref, spread_ref, gather_ref,
                  sameb_ref, samef_ref, o_ref, qk_s, c_s, n_s, m_s, *, t, ctx_len, tm):
    n_chunks = t // CHUNK
    n_ctx_chunks = ctx_len // CHUNK
    n_samples = ml_ref.shape[0]
    cw = cw_ref[...]
    cb = cb_ref[...]
    rid = lax.broadcasted_iota(jnp.int32, (tm, 2 * ML_W), 0)
    for b in range(n_samples):
        for j in range(t // tm):
            r0 = j * tm
            cur = ml_ref[b, r0:r0 + tm, 0:2 * ML_W].astype(F32)
            up = pltpu.roll(cur, 1, 0)
            if r0 in (0, ctx_len):
                up = jnp.where(rid == 0, 0.0, up)
            else:
                up = jnp.where(rid == 0, ml_ref[b, r0 - 16:r0, 0:2 * ML_W].astype(F32)[15:16], up)
            dn = pltpu.roll(cur, tm - 1, 0)
            if r0 + tm in (ctx_len, t):
                dn = jnp.where(rid == tm - 1, 0.0, dn)
            else:
                dn = jnp.where(rid == tm - 1, ml_ref[b, r0 + tm:r0 + tm + 16, 0:2 * ML_W].astype(F32)[0:1], dn)
            y = cw[0:1, :] * up + cw[1:2, :] * cur + cw[2:3, :] * dn + cb
            qk_s[b, r0:r0 + tm, :] = _silu(y)

    o_ref[...] = jnp.zeros_like(o_ref)
    c_s[...] = jnp.zeros_like(c_s)
    n_s[...] = jnp.zeros_like(n_s)
    m_s[...] = jnp.zeros_like(m_s)
    gate_lane = lax.broadcasted_iota(jnp.int32, (CHUNK, LANES), 1) < 16
    tok = lax.broadcasted_iota(jnp.int32, (CHUNK, LANES), 0)

    def load(ch):
        d, b = ch.d, ch.b
        ch.rows = pl.ds(pl.multiple_of(ch.c * CHUNK, CHUNK), CHUNK)
        qk = qk_s[b, ch.rows, :]
        ch.qb = qk[:, 0:ML_W].astype(BF16)
        ch.k = qk[:, ML_W:2 * ML_W] * (ML_DH ** -0.5)
        ch.vb = ml_ref[b, ch.rows, 2 * ML_W:3 * ML_W]
        ch.gc = jnp.where(gate_lane, gc_ref[b, ch.rows, :], 0.0)
        ch.gt = gt_ref[b, ch.c]
        hi, lo = _split(_log_sigmoid(ch.gc))
        ch.b_c = _mm(tri_ref[d], hi) + _mm(tri_ref[d], lo)
        ch.b_r = _mm_split(_log_sigmoid(ch.gt), tri_ref[1 - d])
        ch.s_raw = _nt(ch.qb, _stack_heads(ch.k.astype(BF16), sameb_ref[...]))
        ch.qn = _mm((qk[:, 0:ML_W] * n_s[b, d][0:1, :]).astype(BF16), gather_ref[d])
        ch.c_mat = c_s[b, d]
        ch.qc = _mm(ch.qb, ch.c_mat.astype(BF16))

    def stabilise(ch):
        d = ch.d
        ch.u = pltpu.roll(ch.gc, 4, 1) - ch.b_c
        cmax = ch.u
        for step in (1, 2, 4, 8, 16, 32):
            if d == 0:
                moved = jnp.where(tok >= step, pltpu.roll(cmax, step, 0), -jnp.inf)
            else:
                moved = jnp.where(tok < CHUNK - step, pltpu.roll(cmax, CHUNK - step, 0), -jnp.inf)
            cmax = jnp.maximum(cmax, moved)
        ch.m = m_s[ch.b, d][0:1, :]
        inter = ch.b_c + ch.m
        ch.m_t = jnp.maximum(inter, ch.b_c + cmax)
        ch.w_st = jnp.exp(inter - ch.m_t)
        last = CHUNK - 1 if d == 0 else 0
        ch.bend = ch.b_c[last:last + 1, :]
        ch.m_new = jnp.maximum(ch.bend + ch.m, ch.bend + cmax[last:last + 1, :])
        fcols = [d * 8 + 4 + h for h in range(ML_HEADS)]
        icols = [d * 8 + h for h in range(ML_HEADS)]
        ch.u_row = jnp.concatenate([ch.gt[i:i + 1, :] - ch.b_r[f:f + 1, :] for i, f in zip(icols, fcols)], axis=1)
        ch.bm_wide = _mm_split(ch.b_c - ch.m_t, spread_ref[d])
        ch.wk_wide = _mm(jnp.exp(ch.bend + ch.u - ch.m_new).astype(BF16), spread_ref[d])
        ch.w_old = jnp.broadcast_to(jnp.exp(ch.bend + ch.m - ch.m_new), (8, LANES))
        ch.wold_wide = _mm_split(ch.w_old, spread_ref[d])

    def weigh(ch):
        d = ch.d
        s = ch.s_raw * jnp.exp(ch.bm_wide + ch.u_row + neg_ref[d])
        sb = s.astype(BF16)
        ch.intra = _mm(sb, _stack_heads(ch.vb, sameb_ref[...]))
        ch.rowsum = _mm(sb, gather_ref[d])
        kw = ch.k * ch.wk_wide
        ch.dc = _tn(kw.astype(BF16), ch.vb)
        ch.dn = jnp.sum(kw, axis=0, keepdims=True)

    def normalise(ch):
        d = ch.d
        den = ch.rowsum + ch.w_st * ch.qn
        r = 1.0 / jnp.maximum(jnp.abs(den), jnp.exp(-ch.m_t))
        ch.r_wide = _mm(r.astype(BF16), spread_ref[d])
        ch.wr_wide = _mm((ch.w_st * r).astype(BF16), spread_ref[d])

    def store(ch):
        b, d = ch.b, ch.d
        o_ref[b, ch.rows, :] += ch.intra * ch.r_wide + ch.qc * ch.wr_wide
        w_old = ch.wold_wide[0:1, :]
        c_s[b, d] = w_old * ch.c_mat + ch.dc * samef_ref[...]
        n_s[b, d] = jnp.broadcast_to(w_old * n_s[b, d][0:1, :] + ch.dn, (8, ML_W))
        m_s[b, d] = jnp.broadcast_to(ch.m_new, (8, LANES))

    def body(i, carry):
        chunk = (i, _dir1_chunk(i, n_ctx_chunks, n_chunks))
        chains = [_Chain(b=b, d=d, c=chunk[d]) for b in range(n_samples) for d in range(2)]
        for stage in (load, stabilise, weigh, normalise, store):
            for ch in chains:
                stage(ch)
        return carry

    lax.fori_loop(0, n_chunks, body, 0)


def _mlstm(oml, omla, gates_t, conv_w, conv_b, ctx_len, tm):
    bsz, t, _ = oml.shape
    n_chunks = t // CHUNK
    kern = functools.partial(_mlstm_kernel, t=t, ctx_len=ctx_len, tm=tm)
    consts = _mlstm_tables() + _head_tables()
    sps = _samples_per_step(bsz)
    return pl.pallas_call(
        kern,
        grid=(bsz // sps,),
        in_specs=[pl.BlockSpec((sps, t, 768), lambda i: (i, 0, 0)),
                  pl.BlockSpec((sps, t, LANES), lambda i: (i, 0, 3)),
                  pl.BlockSpec((sps, n_chunks, 16, CHUNK), lambda i: (i, 0, 0, 0)),
                  pl.BlockSpec((3, 2 * ML_W), lambda i: (0, 0)),
                  pl.BlockSpec((1, 2 * ML_W), lambda i: (0, 0))] + [_const_spec(a) for a in consts],
        out_specs=pl.BlockSpec((sps, t, ML_W), lambda i: (i, 0, 0)),
        out_shape=jax.ShapeDtypeStruct((bsz, t, ML_W), F32),
        scratch_shapes=[pltpu.VMEM((sps, t, 2 * ML_W), F32),
                        pltpu.VMEM((sps, 2, ML_W, ML_W), F32),
                        pltpu.VMEM((sps, 2, 8, ML_W), F32),
                        pltpu.VMEM((sps, 2, 8, LANES), F32)],
        compiler_params=_params(("arbitrary",)),
        name="mlstm",
    )(oml, omla, gates_t, conv_w, conv_b[None, :], *consts)


def _hgrn_tables():
    sums = np.zeros((2, (N_MM_LEVELS + 1) * CHUNK, CHUNK), np.float32)
    pairs = np.zeros((2, N_LEVELS + 1, CHUNK, CHUNK), np.float32)
    sign = np.zeros((2, N_LEVELS - N_MM_LEVELS, CHUNK, 1), np.float32)
    for d in range(2):
        p = (lambda a: a) if d == 0 else (lambda a: CHUNK - 1 - a)
        for t in range(CHUNK):
            pairs[d, 0, p(t), p(t)] = 1.0
            for s in range(t + 1):
                sums[d, p(t), p(s)] = 1.0
        for l in range(N_LEVELS):
            w = 1 << l
            for t in range(CHUNK):
                ref = t - (t % (2 * w)) + w - 1
                if l < N_MM_LEVELS:
                    lo, hi = (ref + 1, t) if t > ref else (t + 1, ref)
                    for s in range(lo, hi + 1):
                        sums[d, (l + 1) * CHUNK + p(t), p(s)] = 1.0
                else:
                    sign[d, l - N_MM_LEVELS, p(t), 0] = 1.0 if t > ref else -1.0
                if t > ref:
                    for s in range(ref - w + 1, ref + 1):
                        pairs[d, l + 1, p(t), p(s)] = 1.0
    pairs = np.tile(pairs, (1, 1, 1, HG_HEADS))
    sign = np.broadcast_to(sign, sign.shape[:3] + (HG_W,))
    return jnp.asarray(sums, BF16), jnp.asarray(sign), jnp.asarray(pairs)


def _hgrn_ref_rows(cum, d, l):
    w = 1 << l
    parts = []
    for base in range(0, CHUNK, 2 * w):
        r = base + w - 1 if d == 0 else base + w
        parts.append(jnp.broadcast_to(cum[r:r + 1, :], (2 * w, cum.shape[1])))
    return parts[0] if len(parts) == 1 else jnp.concatenate(parts, axis=0)


def _hgrn_kernel(hqv_ref, hf_ref, lbl_ref, sum_ref, sign_ref, pair_ref, sameb_ref, samef_ref, o_ref, st_s,
                 *, t, ctx_len, layer):
    n_chunks = t // CHUNK
    n_ctx_chunks = ctx_len // CHUNK
    w_all = HG_W

    if layer > 0:
        logits = lbl_ref[...]
        n_layers = logits.shape[0]
        mx = logits[0]
        for l in range(1, n_layers):
            mx = jnp.maximum(mx, logits[l])
        ex = [jnp.exp(logits[l] - mx) for l in range(n_layers)]
        tot = ex[0]
        for l in range(1, n_layers):
            tot = tot + ex[l]
        low_all = ex[1] / tot
        for l in range(2, layer + 1):
            low_all = low_all + ex[l] / tot
        log_low = jnp.log(low_all)
        log_1m_low = jnp.log1p(-low_all)

    o_ref[...] = jnp.zeros_like(o_ref)
    st_s[...] = jnp.zeros_like(st_s)

    def gates(ch):
        b, d = ch.b, ch.d
        ch.rows = pl.ds(pl.multiple_of(ch.c * CHUNK, CHUNK), CHUNK)
        ch.q = _silu(hqv_ref[b, ch.rows, 0:w_all].astype(F32))
        ch.vb = hqv_ref[b, ch.rows, w_all:2 * w_all]
        pre = hf_ref[b, ch.rows, d * w_all:(d + 1) * w_all]
        ls = _log_sigmoid(pre)
        if layer == 0:
            log_f = ls
            ch.key = jnp.exp(ls - pre)
        else:
            lo_, hi_ = log_low[d:d + 1, :], log_1m_low[d:d + 1, :] + ls
            log_f = jnp.maximum(lo_, hi_) + jnp.log1p(jnp.exp(-jnp.abs(lo_ - hi_)))
            ch.key = (1.0 - low_all[d:d + 1, :]) * jnp.exp(ls - pre)
        hi, lo = _split(log_f)
        ch.xb = _mm(sum_ref[d], hi) + _mm(sum_ref[d], lo)
        ch.st = st_s[b, d]
        ch.amat = _nt(ch.q.astype(BF16), _stack_heads(ch.key.astype(BF16), sameb_ref[...])) * pair_ref[d, 0]

    def levels(ch):
        d = ch.d
        cum = ch.xb[0:CHUNK]
        for l in range(N_LEVELS):
            if l < N_MM_LEVELS:
                expo = ch.xb[(l + 1) * CHUNK:(l + 2) * CHUNK]
            else:
                expo = (cum - _hgrn_ref_rows(cum, d, l)) * sign_ref[d, l - N_MM_LEVELS]
            fac = jnp.exp(expo)
            qt = (ch.q * fac).astype(BF16)
            kt = (ch.key * fac).astype(BF16)
            ch.amat = ch.amat + _nt(qt, _stack_heads(kt, sameb_ref[...])) * pair_ref[d, l + 1]
        last = CHUNK - 1 if d == 0 else 0
        bend = cum[last:last + 1, :]
        ch.inter = _nt((ch.q * jnp.exp(cum)).astype(BF16), ch.st.astype(BF16))
        ch.dst = _tn(ch.vb, (ch.key * jnp.exp(bend - cum)).astype(BF16))
        ch.decay = jnp.exp(bend)

    def readout(ch):
        ch.intra = _mm(ch.amat.astype(BF16), _stack_heads(ch.vb, sameb_ref[...]))

    def store(ch):
        o_ref[ch.b, ch.rows, :] += ch.intra + ch.inter
        st_s[ch.b, ch.d] = ch.decay * ch.st + ch.dst * samef_ref[...]

    def body(i, carry):
        chunk = (i, _dir1_chunk(i, n_ctx_chunks, n_chunks))
        chains = [_Chain(b=b, d=d, c=chunk[d]) for b in range(hf_ref.shape[0]) for d in range(2)]
        for stage in (gates, levels, readout, store):
            for ch in chains:
                stage(ch)
        return carry

    lax.fori_loop(0, n_chunks, body, 0)


def _hgrn(ohqv, ohf, lb_logits, ctx_len, layer):
    bsz, t, _ = ohf.shape
    kern = functools.partial(_hgrn_kernel, t=t, ctx_len=ctx_len, layer=layer)
    consts = (lb_logits,) + _hgrn_tables() + _head_tables()
    sps = _samples_per_step(bsz)
    return pl.pallas_call(
        kern,
        grid=(bsz // sps,),
        in_specs=[pl.BlockSpec((sps, t, 2 * HG_W), lambda i: (i, 0, 0)),
                  pl.BlockSpec((sps, t, 2 * HG_W), lambda i: (i, 0, 0))] + [_const_spec(a) for a in consts],
        out_specs=pl.BlockSpec((sps, t, HG_W), lambda i: (i, 0, 0)),
        out_shape=jax.ShapeDtypeStruct((bsz, t, HG_W), F32),
        scratch_shapes=[pltpu.VMEM((sps, 2, HG_W, HG_W), F32)],
        compiler_params=_params(("arbitrary",)),
        name="hgrn",
    )(ohqv, ohf, *consts)


def _mla_prep_kernel(x_ref, gq_ref, gkv_ref, wq_ref, wqr_ref, wk_ref, wv_ref, gqq_ref, gqr_ref, gkk_ref, gkr_ref,
                     cos_ref, sin_ref, q_ref, k_ref, v_ref):
    x = x_ref[0]
    cq = _rms_rows(x[:, 0:MLA_Q_RANK], gq_ref[...]).astype(BF16)
    ckv = _rms_rows(x[:, MLA_Q_RANK:MLA_Q_RANK + MLA_KV_RANK], gkv_ref[...]).astype(BF16)
    misc = x[:, MLA_Q_RANK + MLA_KV_RANK:]
    lane = lax.broadcasted_iota(jnp.int32, misc.shape, 1)
    rope_lane = (lane >= MLA_NOPE) & (lane < MLA_DQK)
    k_rope = jnp.where(rope_lane, misc, 0.0)
    k_rope_rot = jnp.where(rope_lane, pltpu.roll(misc, HEAD_PAD - MLA_ROPE, 1), 0.0)
    q_raw = _mm(cq, wq_ref[...])
    q_rot = _mm(cq, wqr_ref[...])
    k_raw = _mm(ckv, wk_ref[...])
    v_all = _mm(ckv, wv_ref[...]).astype(BF16)
    cos, sin = cos_ref[...], sin_ref[...]

    def norm_rope(xh, xr, g, gr):
        ms = jnp.sum(xh * xh, axis=-1, keepdims=True) * (1.0 / MLA_DQK)
        return (xh * (g * cos) + xr * (gr * sin)) * lax.rsqrt(ms + EPS)

    for h in range(MLA_HEADS):
        sl = slice(h * HEAD_PAD, (h + 1) * HEAD_PAD)
        q = norm_rope(q_raw[:, sl], q_rot[:, sl], gqq_ref[...], gqr_ref[...])
        q_ref[0, h] = (q * (MLA_DQK ** -0.5)).astype(BF16)
        k_ref[0, h] = norm_rope(k_raw[:, sl] + k_rope, k_rope_rot, gkk_ref[...], gkr_ref[...]).astype(BF16)
        v_ref[0, h] = v_all[:, h * MLA_DV:(h + 1) * MLA_DV]


def _rotate_half_columns(n):
    j = np.arange(n)
    first = j % 16 < 8
    return np.where(first, j + 8, j - 8), np.where(first, -1.0, 1.0).astype(np.float32)


def _mla_prep(omla, gq, gkv, w_uq, w_ukv, g_qq, g_kk, cos, sin, tm):
    bsz, t, _ = omla.shape
    hw = MLA_HEADS * HEAD_PAD
    src, sign = _rotate_half_columns(MLA_ROPE)
    src_pad = np.arange(HEAD_PAD)
    src_pad[MLA_NOPE:MLA_DQK] = MLA_NOPE + src
    sign_pad = np.zeros((HEAD_PAD,), np.float32)
    sign_pad[MLA_NOPE:MLA_DQK] = sign
    wq = w_uq.reshape(MLA_Q_RANK, MLA_HEADS, MLA_DQK)
    wq = jnp.pad(wq, ((0, 0), (0, 0), (0, HEAD_PAD - MLA_DQK)))
    wq_rot = (wq[:, :, src_pad] * sign_pad).reshape(MLA_Q_RANK, hw).astype(BF16)
    wq = wq.reshape(MLA_Q_RANK, hw).astype(BF16)
    wkv = w_ukv.reshape(MLA_KV_RANK, MLA_HEADS, MLA_NOPE + MLA_DV)
    wk = jnp.pad(wkv[:, :, :MLA_NOPE], ((0, 0), (0, 0), (0, HEAD_PAD - MLA_NOPE))).reshape(MLA_KV_RANK, hw).astype(BF16)
    wv = wkv[:, :, MLA_NOPE:].reshape(MLA_KV_RANK, MLA_HEADS * MLA_DV).astype(BF16)
    padg = lambda g: jnp.pad(g, (0, HEAD_PAD - MLA_DQK))
    gains = [padg(g_qq)[None, :], (padg(g_qq)[src_pad] * jnp.abs(sign_pad))[None, :],
             padg(g_kk)[None, :], (padg(g_kk)[src_pad] * jnp.abs(sign_pad))[None, :]]
    full = lambda shape: pl.BlockSpec(shape, lambda i, j: (0,) * len(shape))
    rope_spec = pl.BlockSpec((tm, HEAD_PAD), lambda i, j: (j, 0))
    return pl.pallas_call(
        _mla_prep_kernel,
        grid=(bsz, t // tm),
        in_specs=[pl.BlockSpec((1, tm, 512), lambda i, j: (i, j, 0)),
                  full((1, MLA_Q_RANK)), full((1, MLA_KV_RANK)),
                  full((MLA_Q_RANK, hw)), full((MLA_Q_RANK, hw)), full((MLA_KV_RANK, hw)),
                  full((MLA_KV_RANK, MLA_HEADS * MLA_DV))] + [full((1, HEAD_PAD))] * 4 + [rope_spec, rope_spec],
        out_specs=[pl.BlockSpec((1, MLA_HEADS, tm, HEAD_PAD), lambda i, j: (i, 0, j, 0)),
                   pl.BlockSpec((1, MLA_HEADS, tm, HEAD_PAD), lambda i, j: (i, 0, j, 0)),
                   pl.BlockSpec((1, MLA_HEADS, tm, MLA_DV), lambda i, j: (i, 0, j, 0))],
        out_shape=[jax.ShapeDtypeStruct((bsz, MLA_HEADS, t, HEAD_PAD), BF16),
                   jax.ShapeDtypeStruct((bsz, MLA_HEADS, t, HEAD_PAD), BF16),
                   jax.ShapeDtypeStruct((bsz, MLA_HEADS, t, MLA_DV), BF16)],
        compiler_params=_params(("arbitrary", "arbitrary")),
        name="mla_prep",
    )(omla, gq[None, :], gkv[None, :], wq, wq_rot, wk, wv, *gains, cos, sin)


def _attn_kernel(q_ref, k_ref, v_ref, o_ref, *, ctx_len, nct, j0):
    is_ctx = pl.program_id(1) + j0 < nct

    def attend(n_keys):
        scores = lambda h: _nt(q_ref[0, h], k_ref[0, h, 0:n_keys, :])
        s_next = scores(0)
        for h in range(MLA_HEADS):
            s, s_next = s_next, (scores(h + 1) if h + 1 < MLA_HEADS else None)
            p = jnp.exp(s - jnp.max(s, axis=-1, keepdims=True))
            o = _mm(p.astype(BF16), v_ref[0, h, 0:n_keys, :]) / jnp.sum(p, axis=-1, keepdims=True)
            o_ref[0, :, h * MLA_DV:(h + 1) * MLA_DV] = o

    @pl.when(is_ctx)
    def _():
        attend(ctx_len)

    @pl.when(jnp.logical_not(is_ctx))
    def _():
        attend(k_ref.shape[2])


def _attention(q, k, v, ctx_len, tm, j0):
    bsz, _, t, _ = q.shape
    nct = ctx_len // tm
    nq = t // tm - j0
    kern = functools.partial(_attn_kernel, ctx_len=ctx_len, nct=nct, j0=j0)
    return pl.pallas_call(
        kern,
        grid=(bsz, nq),
        in_specs=[pl.BlockSpec((1, MLA_HEADS, tm, HEAD_PAD), lambda i, j: (i, 0, j + j0, 0)),
                  pl.BlockSpec((1, MLA_HEADS, t, HEAD_PAD), lambda i, j: (i, 0, 0, 0)),
                  pl.BlockSpec((1, MLA_HEADS, t, MLA_DV), lambda i, j: (i, 0, 0, 0))],
        out_specs=pl.BlockSpec((1, tm, MLA_HEADS * MLA_DV), lambda i, j: (i, j, 0)),
        out_shape=jax.ShapeDtypeStruct((bsz, nq * tm, MLA_HEADS * MLA_DV), F32),
        compiler_params=_params(("arbitrary", "arbitrary")),
        name="attention",
    )(q, k, v)


def _merge_kernel(ml_ref, hg_ref, at_ref, gate_ref, x_ref, mod_ref, gml_ref, ghg_ref, wo_ref,
                  g2_ref, rwt_ref, same_ref, x1_ref, h2_ref, aff_ref):
    head_mean = same_ref[...] * (1.0 / ML_DH)
    w_hi, w_lo = _split(rwt_ref[...])
    half = x_ref.shape[1] // 2

    def mean_squares(ch):
        ch.ml, ch.hg = ml_ref[0, ch.rows, :], hg_ref[0, ch.rows, :]
        hi, lo = _split(ch.ml * ch.ml)
        ch.ms_ml = _mm(hi, head_mean) + _mm(lo, head_mean)
        hi, lo = _split(ch.hg * ch.hg)
        ch.ms_hg = _mm(hi, head_mean) + _mm(lo, head_mean)

    def project(ch):
        gates = gate_ref[0, ch.rows, :]
        y_ml = ch.ml * lax.rsqrt(ch.ms_ml + EPS) * gml_ref[...] * jax.nn.sigmoid(gates[:, 0:ML_W])
        y_hg = ch.hg * lax.rsqrt(ch.ms_hg + EPS) * ghg_ref[...] * _silu(gates[:, ML_W:ML_W + HG_W])
        ch.y = (_mm(y_ml.astype(BF16), wo_ref[0:ML_W, :])
                + _mm(y_hg.astype(BF16), wo_ref[ML_W:ML_W + HG_W, :])
                + _mm(at_ref[0, ch.rows, :].astype(BF16), wo_ref[ML_W + HG_W:, :]))

    def route(ch):
        x1 = x_ref[0, ch.rows, :] + mod_ref[0, 0, 2:3, :] * ch.y
        x1_ref[0, ch.rows, :] = x1
        h2 = _rms_rows(x1, g2_ref[...]) * (1.0 + mod_ref[0, 0, 4:5, :]) + mod_ref[0, 0, 3:4, :]
        h2_ref[0, ch.rows, :] = h2.astype(BF16)
        h_hi, h_lo = _split(h2)
        ch.logits = _nt(w_hi, h_hi) + _nt(w_hi, h_lo) + _nt(w_lo, h_hi)

    def affinity(ch):
        e = jnp.exp(ch.logits - jnp.max(ch.logits, axis=0, keepdims=True))
        aff_ref[0, :, ch.rows] = e / jnp.sum(e, axis=0, keepdims=True)

    chains = [_Chain(rows=slice(i * half, (i + 1) * half)) for i in range(2)]
    for stage in (mean_squares, project, route, affinity):
        for ch in chains:
            stage(ch)


def _merge(ml, hg, at, ogate, xs, modtok, g_ml, g_hg, w_out, g2, router_w, tm, j0, sel, at_j0):
    bsz, n, d = xs.shape
    n_tiles = n // tm
    tok = lambda width, off: pl.BlockSpec((1, tm, width), lambda i, j: (i, j + off, 0))
    full = lambda shape: pl.BlockSpec(shape, lambda i, j: (0,) * len(shape))
    same_b, _ = _head_tables()
    return pl.pallas_call(
        _merge_kernel,
        grid=(bsz, n_tiles),
        in_specs=[tok(ML_W, j0), tok(HG_W, j0), tok(512, j0 - at_j0), tok(512, j0), tok(d, 0),
                  pl.BlockSpec((1, 1, 6, d), lambda i, j: (i, sel, 0, 0)),
                  full((1, ML_W)), full((1, HG_W)), full((d, d)), full((1, d)), full((N_EXPERTS, d)),
                  _const_spec(same_b)],
        out_specs=[tok(d, 0), tok(d, 0), pl.BlockSpec((1, N_EXPERTS, tm), lambda i, j: (i, 0, j))],
        out_shape=[jax.ShapeDtypeStruct((bsz, n, d), F32),
                   jax.ShapeDtypeStruct((bsz, n, d), BF16),
                   jax.ShapeDtypeStruct((bsz, N_EXPERTS, n), F32)],
        compiler_params=_params(("arbitrary", "arbitrary")),
        name="merge",
    )(ml, hg, at, ogate, xs, modtok, g_ml[None, :], g_hg[None, :], w_out.astype(BF16), g2[None, :],
      router_w.T, same_b)


def _route_kernel(aff_ref, rank_ref, *, cap, n):
    aff = aff_ref[0]

    def search(i, thr):
        cand = thr | (jnp.int32(1) << (30 - i))
        cnt = jnp.sum(jnp.where(aff >= pltpu.bitcast(cand, F32), 1, 0), axis=-1, keepdims=True)
        return jnp.where(cnt >= cap, cand, thr)

    thr = lax.fori_loop(0, 31, search, jnp.zeros((N_EXPERTS, 1), jnp.int32))
    above = aff >= pltpu.bitcast(thr + 1, F32)
    tied = jnp.logical_and(aff >= pltpu.bitcast(thr, F32), jnp.logical_not(above))
    need = cap - jnp.sum(jnp.where(above, 1, 0), axis=-1, keepdims=True)

    tl = min(n, 256)
    r = lax.broadcasted_iota(jnp.int32, (tl, tl), 0)
    c = lax.broadcasted_iota(jnp.int32, (tl, tl), 1)
    before = jnp.where(r < c, 1.0, 0.0).astype(BF16)

    def excl_cumsum(mask):
        parts, carry = [], jnp.zeros((N_EXPERTS, 1), F32)
        for j in range(n // tl):
            m = jnp.where(mask[:, j * tl:(j + 1) * tl], 1.0, 0.0)
            parts.append(_mm(m.astype(BF16), before) + carry)
            carry = carry + jnp.sum(m, axis=-1, keepdims=True)
        return jnp.concatenate(parts, axis=-1).astype(jnp.int32)

    keep = jnp.logical_or(above, jnp.logical_and(tied, excl_cumsum(tied) < need))
    rank_ref[0] = jnp.where(keep, excl_cumsum(keep), -1)


def _route(aff_t, cap):
    bsz, e, n = aff_t.shape
    kern = functools.partial(_route_kernel, cap=cap, n=n)
    return pl.pallas_call(
        kern,
        grid=(bsz,),
        in_specs=[pl.BlockSpec((1, e, n), lambda i: (i, 0, 0))],
        out_specs=pl.BlockSpec((1, e, n), lambda i: (i, 0, 0)),
        out_shape=jax.ShapeDtypeStruct((bsz, e, n), jnp.int32),
        compiler_params=_params(("arbitrary",)),
        name="route",
    )(aff_t)


def _pick_and_gate(rank_row, aff_row, cap):
    slot = lax.broadcasted_iota(jnp.int32, (cap, rank_row.shape[1]), 0)
    chosen = rank_row == slot
    gate = jnp.sum(jnp.where(chosen, aff_row, 0.0), axis=-1, keepdims=True)
    return jnp.where(chosen, 1.0, 0.0).astype(BF16), gate


def _ffn_experts(xs, picks, gates, x1_ref, g2_ref, wg_ref, wu_ref, wd_ref, o_ref, cap):
    e = pl.program_id(1)
    hid = (_silu(_mm(xs, wg_ref[0, 0])) * _mm(xs, wu_ref[0, 0])).astype(BF16)
    out = _mm(hid, wd_ref[0, 0])

    @pl.when(e == 0)
    def _():
        o_ref[...] = jnp.zeros_like(o_ref)

    d = out.shape[1]
    for b, (pick, gate) in enumerate(zip(picks, gates)):
        out_b = (out[b * cap:(b + 1) * cap] * gate).astype(BF16)
        for j in range(d // 256):
            o_ref[b, :, j * 256:(j + 1) * 256] += _tn(pick, out_b[:, j * 256:(j + 1) * 256])

    @pl.when(e == pl.num_programs(1) - 1)
    def _():
        o_ref[...] = x1_ref[...] + g2_ref[...] * o_ref[...]


def _ffn_kernel(rank_ref, aff_ref, h_ref, x1_ref, g2_ref, wg_ref, wu_ref, wd_ref, o_ref, *, cap, bb):
    picks, gates, xs = [], [], []
    for b in range(bb):
        pick, gate = _pick_and_gate(rank_ref[b, 0], aff_ref[b, 0], cap)
        picks.append(pick)
        gates.append(gate)
        xs.append(_mm(pick, h_ref[b].astype(BF16)).astype(BF16))
    _ffn_experts(jnp.concatenate(xs, axis=0), picks, gates, x1_ref, g2_ref, wg_ref, wu_ref, wd_ref, o_ref, cap)


def _ffn(rank, aff_t, h2, x1, g2mod, wg, wu, wd, layer, cap, bb):
    bsz, n, d = h2.shape
    n_exp = wg.shape[1]
    once = dict(pipeline_mode=pl.Buffered(1))
    per_expert = lambda width: pl.BlockSpec((bb, 1, 1, width), lambda i, e: (i, e, 0, 0))
    common = [pl.BlockSpec((bb, n, d), lambda i, e: (i, 0, 0), **once),
              pl.BlockSpec((bb, 1, d), lambda i, e: (i, 0, 0)),
              pl.BlockSpec((1, 1, d, d), lambda i, e: (layer, e, 0, 0)),
              pl.BlockSpec((1, 1, d, d), lambda i, e: (layer, e, 0, 0)),
              pl.BlockSpec((1, 1, d, d), lambda i, e: (layer, e, 0, 0))]
    out_spec = pl.BlockSpec((bb, n, d), lambda i, e: (i, 0, 0))
    out_shape = jax.ShapeDtypeStruct((bsz, n, d), F32)
    rank4, aff4 = rank.reshape(bsz, n_exp, 1, n), aff_t.reshape(bsz, n_exp, 1, n)
    return pl.pallas_call(
        functools.partial(_ffn_kernel, cap=cap, bb=bb),
        grid=(bsz // bb, n_exp),
        in_specs=[per_expert(n), per_expert(n), pl.BlockSpec((bb, n, d), lambda i, e: (i, 0, 0), **once)] + common,
        out_specs=out_spec, out_shape=out_shape,
        compiler_params=_params(("arbitrary", "arbitrary")),
        name="expert_ffn",
    )(rank4, aff4, h2, x1, g2mod, wg, wu, wd)


def _rope_tables(ctx_len, seq):
    half = MLA_ROPE // 2
    inv = ROPE_THETA ** (-jnp.arange(0, half, 2, dtype=F32) / half)
    rows = seq // GRID_W
    row_pos = jnp.repeat(jnp.arange(rows), GRID_W).astype(F32)
    col_pos = jnp.broadcast_to(jnp.arange(GRID_W), (rows, GRID_W)).reshape(-1).astype(F32)

    def cs(pos):
        ang = pos[:, None] * inv[None, :]
        ang = jnp.concatenate([ang, ang], axis=-1)
        return jnp.cos(ang), jnp.sin(ang)

    cos_r, sin_r = cs(row_pos)
    cos_c, sin_c = cs(col_pos)
    zeros = lambda n: jnp.zeros((seq, n), F32)
    ones = lambda n: jnp.ones((seq, n), F32)
    tail = HEAD_PAD - MLA_DQK
    cos = jnp.concatenate([ones(MLA_NOPE), cos_r, cos_c, ones(tail)], axis=-1)
    sin = jnp.concatenate([zeros(MLA_NOPE), sin_r, sin_c, zeros(tail)], axis=-1)
    ident = lambda a, fill: jnp.concatenate([jnp.full((ctx_len, HEAD_PAD), fill, F32), a], axis=0)
    return ident(cos, 1.0), ident(sin, 0.0)


def kernel(x, c, ctx, c_ctx, ada_w, ada_b, norm1_g, norm2_g, w_in, b_in, ml_conv_w, ml_conv_b, ml_norm_g, hg_lb_logits, hg_norm_g, mla_q_norm_g, mla_w_uq, mla_kv_norm_g, mla_w_ukv, mla_q_qk_g, mla_k_qk_g, w_out, router_w, ex_w_gate, ex_w_up, ex_w_down):
    bsz, seq, d = x.shape
    ctx_len = ctx.shape[1]
    depth = ada_w.shape[0]
    t = ctx_len + seq
    tm = 256 if ctx_len % 256 == 0 else 128
    assert ctx_len % tm == 0 and seq % tm == 0 and ctx_len % CHUNK == 0 and seq % GRID_W == 0
    nct = ctx_len // tm
    n_chunks = t // CHUNK

    rows = -(-(bsz + 1) // 8) * 8
    cc = jnp.concatenate([c, c_ctx[None, :], jnp.zeros((rows - bsz - 1, d), F32)], axis=0)
    mod = _modulation(cc, ada_w, ada_b)
    cos, sin = _rope_tables(ctx_len, seq)
    bb_ctx = max(bb for bb in (8, 4, 2, 1) if bsz % bb == 0)

    wg, wu, wd = (w.astype(BF16) for w in (ex_w_gate, ex_w_up, ex_w_down))
    lat = x
    for layer in range(depth):
        need_ctx = layer < depth - 1
        m = mod[layer]
        modtok = jnp.stack([jnp.broadcast_to(m[bsz], (bsz, 6 * d)), m[:bsz]], axis=1).reshape(bsz, 2, 6, d)
        oml, ohqv, ohf, ogate, omla, ogt = _in_proj(ctx, lat, modtok, norm1_g[layer], w_in[layer], b_in[layer], tm, nct)
        gates_t = ogt.reshape(bsz, 16, n_chunks, CHUNK).transpose(0, 2, 1, 3)
        ml = _mlstm(oml, omla, gates_t, ml_conv_w[layer], ml_conv_b[layer], ctx_len, tm)
        hg = _hgrn(ohqv, ohf, hg_lb_logits, ctx_len, layer)
        q, k, v = _mla_prep(omla, mla_q_norm_g[layer], mla_kv_norm_g[layer], mla_w_uq[layer], mla_w_ukv[layer],
                            mla_q_qk_g[layer], mla_k_qk_g[layer], cos, sin, tm)
        at_j0 = 0 if need_ctx else nct
        at = _attention(q, k, v, ctx_len, tm, at_j0)
        g2mod = modtok[:, :, 5:6, :]

        def post(xs, j0, sel, bb):
            x1, h2, aff = _merge(ml, hg, at, ogate, xs, modtok, ml_norm_g[layer], hg_norm_g[layer], w_out[layer],
                                 norm2_g[layer], router_w[layer], tm, j0, sel, at_j0)
            cap = EC_CAPACITY * xs.shape[1] // N_EXPERTS
            rank = _route(aff, cap)
            return _ffn(rank, aff, h2, x1, g2mod[:, sel], wg, wu, wd, layer, cap, bb)

        new_lat = post(lat, nct, 1, 1)
        if need_ctx:
            ctx = post(ctx, 0, 0, bb_ctx)
        lat = new_lat
    return lat
```

```python
import functools

import numpy as np
import jax
import jax.numpy as jnp
from jax import lax
from jax.experimental import pallas as pl
from jax.experimental.pallas import tpu as pltpu

F32 = jnp.float32
BF16 = jnp.bfloat16
HIGHEST = lax.Precision.HIGHEST

D_MODEL = 1024
GRID_W = 64
ML_HEADS = 4
ML_DH = 64
ML_W = 256
HG_HEADS = 4
HG_W = 256
MLA_HEADS = 8
MLA_NOPE = 64
MLA_ROPE = 32
MLA_DQK = 96
MLA_DV = 64
MLA_Q_RANK = 256
MLA_KV_RANK = 128
N_EXPERTS = 16
EC_CAPACITY = 2
CHUNK = 64
ML_CHUNK = 128
ROPE_THETA = 10000.0
EPS = 1e-6
HEAD_PAD = 128
LANES = 128
N_LEVELS = 6
N_MM_LEVELS = 3
SLAB = HG_HEADS * CHUNK
ML_SLAB = ML_HEADS * ML_CHUNK
VMEM_LIMIT = 52 * 1024 * 1024

_O_QK, _O_V, _O_OG, _O_GATES, _O_HQ, _O_HI, _O_HGATE, _O_HF, _O_CQ, _O_CKV, _O_KR, _O_END = (
    0, 512, 768, 1024, 1040, 1296, 1552, 1808, 2320, 2576, 2704, 2736)


def _nt(a, b, **kw):
    return lax.dot_general(a, b, (((1,), (1,)), ((), ())), preferred_element_type=F32, **kw)


def _tn(a, b, **kw):
    return lax.dot_general(a, b, (((0,), (0,)), ((), ())), preferred_element_type=F32, **kw)


def _mm(a, b, **kw):
    return jnp.dot(a, b, preferred_element_type=F32, **kw)


def _split(x):
    hi = x.astype(BF16)
    return hi, (x - hi.astype(F32)).astype(BF16)


def _silu(x):
    return x * jax.nn.sigmoid(x)


def _log_sigmoid(x):
    return jnp.minimum(x, 0.0) - jnp.log1p(jnp.exp(-jnp.abs(x)))


def _rms_rows(x, g):
    return x * lax.rsqrt(jnp.mean(x * x, axis=-1, keepdims=True) + EPS) * g


def _params(sem=None):
    return pltpu.CompilerParams(dimension_semantics=sem, vmem_limit_bytes=VMEM_LIMIT)


def _const_spec(a):
    return pl.BlockSpec(a.shape, lambda *_: (0,) * a.ndim)


def _mod_kernel(c_ref, w_ref, b_ref, o_ref):
    s = _silu(c_ref[...]).astype(BF16)
    o_ref[0] = _mm(s, w_ref[0].astype(BF16)) + b_ref[0]


def _modulation(cc, ada_w, ada_b):
    n_layers, d, n6 = ada_w.shape
    rows = cc.shape[0]
    tn = 1024
    return pl.pallas_call(
        _mod_kernel,
        grid=(n_layers, n6 // tn),
        in_specs=[pl.BlockSpec((rows, d), lambda l, j: (0, 0)),
                  pl.BlockSpec((1, d, tn), lambda l, j: (l, 0, j)),
                  pl.BlockSpec((1, 1, tn), lambda l, j: (l, 0, j))],
        out_specs=pl.BlockSpec((1, rows, tn), lambda l, j: (l, 0, j)),
        out_shape=jax.ShapeDtypeStruct((n_layers, rows, n6), F32),
        compiler_params=_params(("arbitrary", "arbitrary")),
        name="modulation",
    )(cc, ada_w, ada_b.reshape(n_layers, 1, n6))


def _in_kernel(c_ref, l_ref, mod_ref, g_ref, w_ref, b_ref, wgt_ref, bgt_ref,
               oml_ref, ohqv_ref, ohf_ref, ogate_ref, omla_ref, ogt_ref, *, nct):
    x = jnp.where(pl.program_id(1) < nct, c_ref[0], l_ref[0])
    sh = mod_ref[0, 0, 0:1, :]
    sc = mod_ref[0, 0, 1:2, :]
    h = (_rms_rows(x, g_ref[...]) * (1.0 + sc) + sh).astype(BF16)
    oml_ref[0] = (_mm(h, w_ref[:, 0:768]) + b_ref[:, 0:768]).astype(BF16)
    ohqv_ref[0] = (_mm(h, w_ref[:, 768:1280]) + b_ref[:, 768:1280]).astype(BF16)
    ohf_ref[0] = _mm(h, w_ref[:, 1280:1792]) + b_ref[:, 1280:1792]
    ogate_ref[0] = _mm(h, w_ref[:, 1792:2304]) + b_ref[:, 1792:2304]
    omla_ref[0] = _mm(h, w_ref[:, 2304:2816]) + b_ref[:, 2304:2816]
    ogt_ref[0] = _nt(wgt_ref[...], h) + bgt_ref[...]


def _in_proj(ctx_x, lat_x, modtok, g1, w_in, b_in, tm, nct):
    bsz, seq, d = lat_x.shape
    t = ctx_x.shape[1] + seq
    src, sign = _rotate_half_columns(MLA_ROPE)
    zeros = lambda n: jnp.zeros((d, n), F32)
    w = jnp.concatenate([
        w_in[:, _O_QK:_O_OG], w_in[:, _O_HQ:_O_HGATE], w_in[:, _O_HF:_O_CQ],
        w_in[:, _O_OG:_O_GATES], w_in[:, _O_HGATE:_O_HF], w_in[:, _O_CQ:_O_KR],
        w_in[:, _O_GATES:_O_HQ], zeros(48), w_in[:, _O_KR:_O_END],
        w_in[:, _O_KR:_O_END][:, src] * sign], axis=1).astype(BF16)
    zb = lambda n: jnp.zeros((n,), F32)
    b = jnp.concatenate([
        b_in[_O_QK:_O_OG], b_in[_O_HQ:_O_HGATE], b_in[_O_HF:_O_CQ],
        b_in[_O_OG:_O_GATES], b_in[_O_HGATE:_O_HF], b_in[_O_CQ:_O_KR],
        b_in[_O_GATES:_O_HQ], zb(48), b_in[_O_KR:_O_END], b_in[_O_KR:_O_END][src] * sign])[None, :]
    wgt = w_in[:, _O_GATES:_O_HQ].T.astype(BF16)
    bgt = b_in[_O_GATES:_O_HQ][:, None]
    nw = w.shape[1]
    tok = lambda width: pl.BlockSpec((1, tm, width), lambda i, j: (i, j, 0))
    full = lambda shape: pl.BlockSpec(shape, lambda i, j: (0,) * len(shape))
    return pl.pallas_call(
        functools.partial(_in_kernel, nct=nct),
        grid=(bsz, t // tm),
        in_specs=[pl.BlockSpec((1, tm, d), lambda i, j: (i, jnp.minimum(j, nct - 1), 0)),
                  pl.BlockSpec((1, tm, d), lambda i, j: (i, jnp.maximum(j - nct, 0), 0)),
                  pl.BlockSpec((1, 1, 6, d), lambda i, j: (i, jnp.where(j >= nct, 1, 0), 0, 0)),
                  full((1, d)), full((d, nw)), full((1, nw)), full((16, d)), full((16, 1))],
        out_specs=[tok(768), tok(512), tok(512), tok(512), tok(512),
                   pl.BlockSpec((1, 16, tm), lambda i, j: (i, 0, j))],
        out_shape=[jax.ShapeDtypeStruct((bsz, t, 768), BF16),
                   jax.ShapeDtypeStruct((bsz, t, 512), BF16),
                   jax.ShapeDtypeStruct((bsz, t, 512), F32),
                   jax.ShapeDtypeStruct((bsz, t, 512), F32),
                   jax.ShapeDtypeStruct((bsz, t, 512), F32),
                   jax.ShapeDtypeStruct((bsz, 16, t), F32)],
        compiler_params=_params(("arbitrary", "arbitrary")),
        name="in_proj",
    )(ctx_x, lat_x, modtok, g1[None, :], w, b, wgt, bgt)


def _dir1_chunk(i, n_ctx_chunks, n_chunks):
    return jnp.where(i < n_ctx_chunks, n_ctx_chunks - 1 - i, n_chunks - 1 - (i - n_ctx_chunks))


def _head_tables(chunk):
    feat_head = np.arange(ML_W) // ML_DH
    stack = (np.arange(ML_HEADS * chunk)[:, None] // chunk == feat_head[None, :]).astype(np.float32)
    square = (feat_head[:, None] == feat_head[None, :]).astype(np.float32)
    return jnp.asarray(stack, BF16), jnp.asarray(square, F32)


def _stack_heads(x_bf, same_bf):
    return jnp.concatenate([x_bf] * ML_HEADS, axis=0) * same_bf


class _Chain:
    def __init__(self, **kw):
        self.__dict__.update(kw)


def _samples_per_step(bsz):
    return 2 if bsz % 2 == 0 else 1


def _mm_split(a, b):
    hi, lo = _split(a)
    return _mm(hi, b) + _mm(lo, b)


def _mlstm_tables():
    t = np.arange(ML_CHUNK)
    lower = (t[:, None] >= t[None, :]).astype(np.float32)
    tri = np.stack([lower, lower.T])
    neg = np.stack([np.tile(np.where(m > 0, 0.0, -np.inf), (1, ML_HEADS)) for m in (lower, lower.T)]).astype(np.float32)
    spread = np.zeros((2, LANES, ML_W), np.float32)
    spread_s = np.zeros((2, LANES, ML_SLAB), np.float32)
    for d in range(2):
        for h in range(ML_HEADS):
            spread[d, d * 8 + 4 + h, h * ML_DH:(h + 1) * ML_DH] = 1.0
            spread_s[d, d * 8 + 4 + h, h * ML_CHUNK:(h + 1) * ML_CHUNK] = 1.0
    as_bf = lambda a: jnp.asarray(a, BF16)
    return (as_bf(tri), jnp.asarray(neg), as_bf(spread), as_bf(spread.transpose(0, 2, 1)),
            as_bf(spread_s), as_bf(spread_s.transpose(0, 2, 1)))


def _mlstm_kernel(ml_ref, gc_ref, gt_ref, cw_ref, cb_ref, tri_ref, neg_ref, spread_ref, gather_ref, spread_s_ref,
                  gather_s_ref, sameb_ref, samef_ref, o_ref, qk_s, c_s, n_s, m_s, *, t, ctx_len, tm):
    n_chunks = t // ML_CHUNK
    n_ctx_chunks = ctx_len // ML_CHUNK
    n_samples = ml_ref.shape[0]
    cw = cw_ref[...]
    cb = cb_ref[...]
    rid = lax.broadcasted_iota(jnp.int32, (tm, 2 * ML_W), 0)
    for b in range(n_samples):
        for j in range(t // tm):
            r0 = j * tm
            cur = ml_ref[b, r0:r0 + tm, 0:2 * ML_W].astype(F32)
            up = pltpu.roll(cur, 1, 0)
            if r0 in (0, ctx_len):
                up = jnp.where(rid == 0, 0.0, up)
            else:
                up = jnp.where(rid == 0, ml_ref[b, r0 - 16:r0, 0:2 * ML_W].astype(F32)[15:16], up)
            dn = pltpu.roll(cur, tm - 1, 0)
            if r0 + tm in (ctx_len, t):
                dn = jnp.where(rid == tm - 1, 0.0, dn)
            else:
                dn = jnp.where(rid == tm - 1, ml_ref[b, r0 + tm:r0 + tm + 16, 0:2 * ML_W].astype(F32)[0:1], dn)
            y = cw[0:1, :] * up + cw[1:2, :] * cur + cw[2:3, :] * dn + cb
            qk_s[b, r0:r0 + tm, :] = _silu(y)

    o_ref[...] = jnp.zeros_like(o_ref)
    c_s[...] = jnp.zeros_like(c_s)
    n_s[...] = jnp.zeros_like(n_s)
    m_s[...] = jnp.zeros_like(m_s)
    gate_lane = lax.broadcasted_iota(jnp.int32, (ML_CHUNK, LANES), 1) < 16
    tok = lax.broadcasted_iota(jnp.int32, (ML_CHUNK, LANES), 0)

    def load(ch):
        d, b = ch.d, ch.b
        ch.rows = pl.ds(pl.multiple_of(ch.c * ML_CHUNK, ML_CHUNK), ML_CHUNK)
        qk = qk_s[b, ch.rows, :]
        ch.qb = qk[:, 0:ML_W].astype(BF16)
        ch.k = qk[:, ML_W:2 * ML_W] * (ML_DH ** -0.5)
        ch.vb = ml_ref[b, ch.rows, 2 * ML_W:3 * ML_W]
        ch.gc = jnp.where(gate_lane, gc_ref[b, ch.rows, :], 0.0)
        ch.gt = gt_ref[b, ch.c]
        hi, lo = _split(_log_sigmoid(ch.gc))
        ch.b_c = _mm(tri_ref[d], hi) + _mm(tri_ref[d], lo)
        ch.b_r = _mm_split(_log_sigmoid(ch.gt), tri_ref[1 - d])
        ch.s_raw = _nt(ch.qb, _stack_heads(ch.k.astype(BF16), sameb_ref[...]))
        ch.qn = _mm((qk[:, 0:ML_W] * n_s[b, d][0:1, :]).astype(BF16), gather_ref[d])
        ch.c_mat = c_s[b, d]
        ch.qc = _mm(ch.qb, ch.c_mat.astype(BF16))

    def stabilise(ch):
        d = ch.d
        ch.u = pltpu.roll(ch.gc, 4, 1) - ch.b_c
        cmax = ch.u
        for step in [1 << i for i in range(ML_CHUNK.bit_length() - 1)]:
            if d == 0:
                moved = jnp.where(tok >= step, pltpu.roll(cmax, step, 0), -jnp.inf)
            else:
                moved = jnp.where(tok < ML_CHUNK - step, pltpu.roll(cmax, ML_CHUNK - step, 0), -jnp.inf)
            cmax = jnp.maximum(cmax, moved)
        ch.m = m_s[ch.b, d][0:1, :]
        inter = ch.b_c + ch.m
        ch.m_t = jnp.maximum(inter, ch.b_c + cmax)
        ch.w_st = jnp.exp(inter - ch.m_t)
        last = ML_CHUNK - 1 if d == 0 else 0
        ch.bend = ch.b_c[last:last + 1, :]
        ch.m_new = jnp.maximum(ch.bend + ch.m, ch.bend + cmax[last:last + 1, :])
        fcols = [d * 8 + 4 + h for h in range(ML_HEADS)]
        icols = [d * 8 + h for h in range(ML_HEADS)]
        ch.u_row = jnp.concatenate([ch.gt[i:i + 1, :] - ch.b_r[f:f + 1, :] for i, f in zip(icols, fcols)], axis=1)
        ch.bm_wide = _mm_split(ch.b_c - ch.m_t, spread_s_ref[d])
        ch.wk_wide = _mm(jnp.exp(ch.bend + ch.u - ch.m_new).astype(BF16), spread_ref[d])
        ch.w_old = jnp.broadcast_to(jnp.exp(ch.bend + ch.m - ch.m_new), (8, LANES))
        ch.wold_wide = _mm_split(ch.w_old, spread_ref[d])

    def weigh(ch):
        d = ch.d
        s = ch.s_raw * jnp.exp(ch.bm_wide + ch.u_row + neg_ref[d])
        sb = s.astype(BF16)
        ch.intra = _mm(sb, _stack_heads(ch.vb, sameb_ref[...]))
        ch.rowsum = _mm(sb, gather_s_ref[d])
        kw = ch.k * ch.wk_wide
        ch.dc = _tn(kw.astype(BF16), ch.vb)
        ch.dn = jnp.sum(kw, axis=0, keepdims=True)

    def normalise(ch):
        d = ch.d
        den = ch.rowsum + ch.w_st * ch.qn
        r = 1.0 / jnp.maximum(jnp.abs(den), jnp.exp(-ch.m_t))
        ch.r_wide = _mm(r.astype(BF16), spread_ref[d])
        ch.wr_wide = _mm((ch.w_st * r).astype(BF16), spread_ref[d])

    def store(ch):
        b, d = ch.b, ch.d
        o_ref[b, ch.rows, :] += ch.intra * ch.r_wide + ch.qc * ch.wr_wide
        w_old = ch.wold_wide[0:1, :]
        c_s[b, d] = w_old * ch.c_mat + ch.dc * samef_ref[...]
        n_s[b, d] = jnp.broadcast_to(w_old * n_s[b, d][0:1, :] + ch.dn, (8, ML_W))
        m_s[b, d] = jnp.broadcast_to(ch.m_new, (8, LANES))

    def body(i, carry):
        chunk = (i, _dir1_chunk(i, n_ctx_chunks, n_chunks))
        chains = [_Chain(b=b, d=d, c=chunk[d]) for b in range(n_samples) for d in range(2)]
        for stage in (load, stabilise, weigh, normalise, store):
            for ch in chains:
                stage(ch)
        return carry

    lax.fori_loop(0, n_chunks, body, 0)


def _mlstm(oml, omla, gates_t, conv_w, conv_b, ctx_len, tm):
    bsz, t, _ = oml.shape
    n_chunks = t // ML_CHUNK
    kern = functools.partial(_mlstm_kernel, t=t, ctx_len=ctx_len, tm=tm)
    consts = _mlstm_tables() + _head_tables(ML_CHUNK)
    sps = _samples_per_step(bsz)
    return pl.pallas_call(
        kern,
        grid=(bsz // sps,),
        in_specs=[pl.BlockSpec((sps, t, 768), lambda i: (i, 0, 0)),
                  pl.BlockSpec((sps, t, LANES), lambda i: (i, 0, 3)),
                  pl.BlockSpec((sps, n_chunks, 16, ML_CHUNK), lambda i: (i, 0, 0, 0)),
                  pl.BlockSpec((3, 2 * ML_W), lambda i: (0, 0)),
                  pl.BlockSpec((1, 2 * ML_W), lambda i: (0, 0))] + [_const_spec(a) for a in consts],
        out_specs=pl.BlockSpec((sps, t, ML_W), lambda i: (i, 0, 0)),
        out_shape=jax.ShapeDtypeStruct((bsz, t, ML_W), F32),
        scratch_shapes=[pltpu.VMEM((sps, t, 2 * ML_W), F32),
                        pltpu.VMEM((sps, 2, ML_W, ML_W), F32),
                        pltpu.VMEM((sps, 2, 8, ML_W), F32),
                        pltpu.VMEM((sps, 2, 8, LANES), F32)],
        compiler_params=_params(("arbitrary",)),
        name="mlstm",
    )(oml, omla, gates_t, conv_w, conv_b[None, :], *consts)


def _hgrn_tables():
    sums = np.zeros((2, (N_MM_LEVELS + 1) * CHUNK, CHUNK), np.float32)
    pairs = np.zeros((2, N_LEVELS + 1, CHUNK, CHUNK), np.float32)
    sign = np.zeros((2, N_LEVELS - N_MM_LEVELS, CHUNK, 1), np.float32)
    for d in range(2):
        p = (lambda a: a) if d == 0 else (lambda a: CHUNK - 1 - a)
        for t in range(CHUNK):
            pairs[d, 0, p(t), p(t)] = 1.0
            for s in range(t + 1):
                sums[d, p(t), p(s)] = 1.0
        for l in range(N_LEVELS):
            w = 1 << l
            for t in range(CHUNK):
                ref = t - (t % (2 * w)) + w - 1
                if l < N_MM_LEVELS:
                    lo, hi = (ref + 1, t) if t > ref else (t + 1, ref)
                    for s in range(lo, hi + 1):
                        sums[d, (l + 1) * CHUNK + p(t), p(s)] = 1.0
                else:
                    sign[d, l - N_MM_LEVELS, p(t), 0] = 1.0 if t > ref else -1.0
                if t > ref:
                    for s in range(ref - w + 1, ref + 1):
                        pairs[d, l + 1, p(t), p(s)] = 1.0
    pairs = np.tile(pairs, (1, 1, 1, HG_HEADS))
    sign = np.broadcast_to(sign, sign.shape[:3] + (HG_W,))
    return jnp.asarray(sums, BF16), jnp.asarray(sign), jnp.asarray(pairs)


def _hgrn_ref_rows(cum, d, l):
    w = 1 << l
    parts = []
    for base in range(0, CHUNK, 2 * w):
        r = base + w - 1 if d == 0 else base + w
        parts.append(jnp.broadcast_to(cum[r:r + 1, :], (2 * w, cum.shape[1])))
    return parts[0] if len(parts) == 1 else jnp.concatenate(parts, axis=0)


def _hgrn_kernel(hqv_ref, hf_ref, lbl_ref, sum_ref, sign_ref, pair_ref, sameb_ref, samef_ref, o_ref, st_s,
                 *, t, ctx_len, layer):
    n_chunks = t // CHUNK
    n_ctx_chunks = ctx_len // CHUNK
    w_all = HG_W

    if layer > 0:
        logits = lbl_ref[...]
        n_layers = logits.shape[0]
        mx = logits[0]
        for l in range(1, n_layers):
            mx = jnp.maximum(mx, logits[l])
        ex = [jnp.exp(logits[l] - mx) for l in range(n_layers)]
        tot = ex[0]
        for l in range(1, n_layers):
            tot = tot + ex[l]
        low_all = ex[1] / tot
        for l in range(2, layer + 1):
            low_all = low_all + ex[l] / tot
        log_low = jnp.log(low_all)
        log_1m_low = jnp.log1p(-low_all)

    o_ref[...] = jnp.zeros_like(o_ref)
    st_s[...] = jnp.zeros_like(st_s)

    def gates(ch):
        b, d = ch.b, ch.d
        ch.rows = pl.ds(pl.multiple_of(ch.c * CHUNK, CHUNK), CHUNK)
        ch.q = _silu(hqv_ref[b, ch.rows, 0:w_all].astype(F32))
        ch.vb = hqv_ref[b, ch.rows, w_all:2 * w_all]
        pre = hf_ref[b, ch.rows, d * w_all:(d + 1) * w_all]
        ls = _log_sigmoid(pre)
        if layer == 0:
            log_f = ls
            ch.key = jnp.exp(ls - pre)
        else:
            lo_, hi_ = log_low[d:d + 1, :], log_1m_low[d:d + 1, :] + ls
            log_f = jnp.maximum(lo_, hi_) + jnp.log1p(jnp.exp(-jnp.abs(lo_ - hi_)))
            ch.key = (1.0 - low_all[d:d + 1, :]) * jnp.exp(ls - pre)
        hi, lo = _split(log_f)
        ch.xb = _mm(sum_ref[d], hi) + _mm(sum_ref[d], lo)
        ch.st = st_s[b, d]
        ch.amat = _nt(ch.q.astype(BF16), _stack_heads(ch.key.astype(BF16), sameb_ref[...])) * pair_ref[d, 0]

    def levels(ch):
        d = ch.d
        cum = ch.xb[0:CHUNK]
        for l in range(N_LEVELS):
            if l < N_MM_LEVELS:
                expo = ch.xb[(l + 1) * CHUNK:(l + 2) * CHUNK]
            else:
                expo = (cum - _hgrn_ref_rows(cum, d, l)) * sign_ref[d, l - N_MM_LEVELS]
            fac = jnp.exp(expo)
            qt = (ch.q * fac).astype(BF16)
            kt = (ch.key * fac).astype(BF16)
            ch.amat = ch.amat + _nt(qt, _stack_heads(kt, sameb_ref[...])) * pair_ref[d, l + 1]
        last = CHUNK - 1 if d == 0 else 0
        bend = cum[last:last + 1, :]
        ch.inter = _nt((ch.q * jnp.exp(cum)).astype(BF16), ch.st.astype(BF16))
        ch.dst = _tn(ch.vb, (ch.key * jnp.exp(bend - cum)).astype(BF16))
        ch.decay = jnp.exp(bend)

    def readout(ch):
        ch.intra = _mm(ch.amat.astype(BF16), _stack_heads(ch.vb, sameb_ref[...]))

    def store(ch):
        o_ref[ch.b, ch.rows, :] += ch.intra + ch.inter
        st_s[ch.b, ch.d] = ch.decay * ch.st + ch.dst * samef_ref[...]

    def body(i, carry):
        chunk = (i, _dir1_chunk(i, n_ctx_chunks, n_chunks))
        chains = [_Chain(b=b, d=d, c=chunk[d]) for b in range(hf_ref.shape[0]) for d in range(2)]
        for stage in (gates, levels, readout, store):
            for ch in chains:
                stage(ch)
        return carry

    lax.fori_loop(0, n_chunks, body, 0)


def _hgrn(ohqv, ohf, lb_logits, ctx_len, layer):
    bsz, t, _ = ohf.shape
    kern = functools.partial(_hgrn_kernel, t=t, ctx_len=ctx_len, layer=layer)
    consts = (lb_logits,) + _hgrn_tables() + _head_tables(CHUNK)
    sps = _samples_per_step(bsz)
    return pl.pallas_call(
        kern,
        grid=(bsz // sps,),
        in_specs=[pl.BlockSpec((sps, t, 2 * HG_W), lambda i: (i, 0, 0)),
                  pl.BlockSpec((sps, t, 2 * HG_W), lambda i: (i, 0, 0))] + [_const_spec(a) for a in consts],
        out_specs=pl.BlockSpec((sps, t, HG_W), lambda i: (i, 0, 0)),
        out_shape=jax.ShapeDtypeStruct((bsz, t, HG_W), F32),
        scratch_shapes=[pltpu.VMEM((sps, 2, HG_W, HG_W), F32)],
        compiler_params=_params(("arbitrary",)),
        name="hgrn",
    )(ohqv, ohf, *consts)


def _mla_prep_kernel(x_ref, gq_ref, gkv_ref, wq_ref, wqr_ref, wk_ref, wv_ref, gqq_ref, gqr_ref, gkk_ref, gkr_ref,
                     cos_ref, sin_ref, q_ref, k_ref, v_ref):
    x = x_ref[0]
    cq = _rms_rows(x[:, 0:MLA_Q_RANK], gq_ref[...]).astype(BF16)
    ckv = _rms_rows(x[:, MLA_Q_RANK:MLA_Q_RANK + MLA_KV_RANK], gkv_ref[...]).astype(BF16)
    misc = x[:, MLA_Q_RANK + MLA_KV_RANK:]
    lane = lax.broadcasted_iota(jnp.int32, misc.shape, 1)
    rope_lane = (lane >= MLA_NOPE) & (lane < MLA_DQK)
    k_rope = jnp.where(rope_lane, misc, 0.0)
    k_rope_rot = jnp.where(rope_lane, pltpu.roll(misc, HEAD_PAD - MLA_ROPE, 1), 0.0)
    q_raw = _mm(cq, wq_ref[...])
    q_rot = _mm(cq, wqr_ref[...])
    k_raw = _mm(ckv, wk_ref[...])
    v_all = _mm(ckv, wv_ref[...]).astype(BF16)
    cos, sin = cos_ref[...], sin_ref[...]

    def norm_rope(xh, xr, g, gr):
        ms = jnp.sum(xh * xh, axis=-1, keepdims=True) * (1.0 / MLA_DQK)
        return (xh * (g * cos) + xr * (gr * sin)) * lax.rsqrt(ms + EPS)

    for h in range(MLA_HEADS):
        sl = slice(h * HEAD_PAD, (h + 1) * HEAD_PAD)
        q = norm_rope(q_raw[:, sl], q_rot[:, sl], gqq_ref[...], gqr_ref[...])
        q_ref[0, h] = (q * (MLA_DQK ** -0.5)).astype(BF16)
        k_ref[0, h] = norm_rope(k_raw[:, sl] + k_rope, k_rope_rot, gkk_ref[...], gkr_ref[...]).astype(BF16)
        v_ref[0, h] = v_all[:, h * MLA_DV:(h + 1) * MLA_DV]


def _rotate_half_columns(n):
    j = np.arange(n)
    first = j % 16 < 8
    return np.where(first, j + 8, j - 8), np.where(first, -1.0, 1.0).astype(np.float32)


def _mla_prep(omla, gq, gkv, w_uq, w_ukv, g_qq, g_kk, cos, sin, tm):
    bsz, t, _ = omla.shape
    hw = MLA_HEADS * HEAD_PAD
    src, sign = _rotate_half_columns(MLA_ROPE)
    src_pad = np.arange(HEAD_PAD)
    src_pad[MLA_NOPE:MLA_DQK] = MLA_NOPE + src
    sign_pad = np.zeros((HEAD_PAD,), np.float32)
    sign_pad[MLA_NOPE:MLA_DQK] = sign
    wq = w_uq.reshape(MLA_Q_RANK, MLA_HEADS, MLA_DQK)
    wq = jnp.pad(wq, ((0, 0), (0, 0), (0, HEAD_PAD - MLA_DQK)))
    wq_rot = (wq[:, :, src_pad] * sign_pad).reshape(MLA_Q_RANK, hw).astype(BF16)
    wq = wq.reshape(MLA_Q_RANK, hw).astype(BF16)
    wkv = w_ukv.reshape(MLA_KV_RANK, MLA_HEADS, MLA_NOPE + MLA_DV)
    wk = jnp.pad(wkv[:, :, :MLA_NOPE], ((0, 0), (0, 0), (0, HEAD_PAD - MLA_NOPE))).reshape(MLA_KV_RANK, hw).astype(BF16)
    wv = wkv[:, :, MLA_NOPE:].reshape(MLA_KV_RANK, MLA_HEADS * MLA_DV).astype(BF16)
    padg = lambda g: jnp.pad(g, (0, HEAD_PAD - MLA_DQK))
    gains = [padg(g_qq)[None, :], (padg(g_qq)[src_pad] * jnp.abs(sign_pad))[None, :],
             padg(g_kk)[None, :], (padg(g_kk)[src_pad] * jnp.abs(sign_pad))[None, :]]
    full = lambda shape: pl.BlockSpec(shape, lambda i, j: (0,) * len(shape))
    rope_spec = pl.BlockSpec((tm, HEAD_PAD), lambda i, j: (j, 0))
    return pl.pallas_call(
        _mla_prep_kernel,
        grid=(bsz, t // tm),
        in_specs=[pl.BlockSpec((1, tm, 512), lambda i, j: (i, j, 0)),
                  full((1, MLA_Q_RANK)), full((1, MLA_KV_RANK)),
                  full((MLA_Q_RANK, hw)), full((MLA_Q_RANK, hw)), full((MLA_KV_RANK, hw)),
                  full((MLA_KV_RANK, MLA_HEADS * MLA_DV))] + [full((1, HEAD_PAD))] * 4 + [rope_spec, rope_spec],
        out_specs=[pl.BlockSpec((1, MLA_HEADS, tm, HEAD_PAD), lambda i, j: (i, 0, j, 0)),
                   pl.BlockSpec((1, MLA_HEADS, tm, HEAD_PAD), lambda i, j: (i, 0, j, 0)),
                   pl.BlockSpec((1, MLA_HEADS, tm, MLA_DV), lambda i, j: (i, 0, j, 0))],
        out_shape=[jax.ShapeDtypeStruct((bsz, MLA_HEADS, t, HEAD_PAD), BF16),
                   jax.ShapeDtypeStruct((bsz, MLA_HEADS, t, HEAD_PAD), BF16),
                   jax.ShapeDtypeStruct((bsz, MLA_HEADS, t, MLA_DV), BF16)],
        compiler_params=_params(("arbitrary", "arbitrary")),
        name="mla_prep",
    )(omla, gq[None, :], gkv[None, :], wq, wq_rot, wk, wv, *gains, cos, sin)


def _attn_kernel(q_ref, k_ref, v_ref, o_ref, *, ctx_len, nct, j0):
    is_ctx = pl.program_id(1) + j0 < nct

    def attend(n_keys):
        scores = lambda h: _nt(q_ref[0, h], k_ref[0, h, 0:n_keys, :])
        s_next = scores(0)
        for h in range(MLA_HEADS):
            s, s_next = s_next, (scores(h + 1) if h + 1 < MLA_HEADS else None)
            p = jnp.exp(s - jnp.max(s, axis=-1, keepdims=True))
            o = _mm(p.astype(BF16), v_ref[0, h, 0:n_keys, :]) / jnp.sum(p, axis=-1, keepdims=True)
            o_ref[0, :, h * MLA_DV:(h + 1) * MLA_DV] = o

    @pl.when(is_ctx)
    def _():
        attend(ctx_len)

    @pl.when(jnp.logical_not(is_ctx))
    def _():
        attend(k_ref.shape[2])


def _attention(q, k, v, ctx_len, tm, j0):
    bsz, _, t, _ = q.shape
    nct = ctx_len // tm
    nq = t // tm - j0
    kern = functools.partial(_attn_kernel, ctx_len=ctx_len, nct=nct, j0=j0)
    return pl.pallas_call(
        kern,
        grid=(bsz, nq),
        in_specs=[pl.BlockSpec((1, MLA_HEADS, tm, HEAD_PAD), lambda i, j: (i, 0, j + j0, 0)),
                  pl.BlockSpec((1, MLA_HEADS, t, HEAD_PAD), lambda i, j: (i, 0, 0, 0)),
                  pl.BlockSpec((1, MLA_HEADS, t, MLA_DV), lambda i, j: (i, 0, 0, 0))],
        out_specs=pl.BlockSpec((1, tm, MLA_HEADS * MLA_DV), lambda i, j: (i, j, 0)),
        out_shape=jax.ShapeDtypeStruct((bsz, nq * tm, MLA_HEADS * MLA_DV), F32),
        compiler_params=_params(("arbitrary", "arbitrary")),
        name="attention",
    )(q, k, v)


def _merge_kernel(ml_ref, hg_ref, at_ref, gate_ref, x_ref, mod_ref, gml_ref, ghg_ref, wo_ref,
                  g2_ref, rwt_ref, same_ref, x1_ref, h2_ref, aff_ref):
    head_mean = same_ref[...] * (1.0 / ML_DH)
    w_hi, w_lo = _split(rwt_ref[...])
    half = x_ref.shape[1] // 2

    def mean_squares(ch):
        ch.ml, ch.hg = ml_ref[0, ch.rows, :], hg_ref[0, ch.rows, :]
        hi, lo = _split(ch.ml * ch.ml)
        ch.ms_ml = _mm(hi, head_mean) + _mm(lo, head_mean)
        hi, lo = _split(ch.hg * ch.hg)
        ch.ms_hg = _mm(hi, head_mean) + _mm(lo, head_mean)

    def project(ch):
        gates = gate_ref[0, ch.rows, :]
        y_ml = ch.ml * lax.rsqrt(ch.ms_ml + EPS) * gml_ref[...] * jax.nn.sigmoid(gates[:, 0:ML_W])
        y_hg = ch.hg * lax.rsqrt(ch.ms_hg + EPS) * ghg_ref[...] * _silu(gates[:, ML_W:ML_W + HG_W])
        ch.y = (_mm(y_ml.astype(BF16), wo_ref[0:ML_W, :])
                + _mm(y_hg.astype(BF16), wo_ref[ML_W:ML_W + HG_W, :])
                + _mm(at_ref[0, ch.rows, :].astype(BF16), wo_ref[ML_W + HG_W:, :]))

    def route(ch):
        x1 = x_ref[0, ch.rows, :] + mod_ref[0, 0, 2:3, :] * ch.y
        x1_ref[0, ch.rows, :] = x1
        h2 = _rms_rows(x1, g2_ref[...]) * (1.0 + mod_ref[0, 0, 4:5, :]) + mod_ref[0, 0, 3:4, :]
        h2_ref[0, ch.rows, :] = h2.astype(BF16)
        h_hi, h_lo = _split(h2)
        ch.logits = _nt(w_hi, h_hi) + _nt(w_hi, h_lo) + _nt(w_lo, h_hi)

    def affinity(ch):
        e = jnp.exp(ch.logits - jnp.max(ch.logits, axis=0, keepdims=True))
        aff_ref[0, :, ch.rows] = e / jnp.sum(e, axis=0, keepdims=True)

    chains = [_Chain(rows=slice(i * half, (i + 1) * half)) for i in range(2)]
    for stage in (mean_squares, project, route, affinity):
        for ch in chains:
            stage(ch)


def _merge(ml, hg, at, ogate, xs, modtok, g_ml, g_hg, w_out, g2, router_w, tm, j0, sel, at_j0):
    bsz, n, d = xs.shape
    n_tiles = n // tm
    tok = lambda width, off: pl.BlockSpec((1, tm, width), lambda i, j: (i, j + off, 0))
    full = lambda shape: pl.BlockSpec(shape, lambda i, j: (0,) * len(shape))
    same_b = _head_tables(CHUNK)[1].astype(BF16)
    return pl.pallas_call(
        _merge_kernel,
        grid=(bsz, n_tiles),
        in_specs=[tok(ML_W, j0), tok(HG_W, j0), tok(512, j0 - at_j0), tok(512, j0), tok(d, 0),
                  pl.BlockSpec((1, 1, 6, d), lambda i, j: (i, sel, 0, 0)),
                  full((1, ML_W)), full((1, HG_W)), full((d, d)), full((1, d)), full((N_EXPERTS, d)),
                  _const_spec(same_b)],
        out_specs=[tok(d, 0), tok(d, 0), pl.BlockSpec((1, N_EXPERTS, tm), lambda i, j: (i, 0, j))],
        out_shape=[jax.ShapeDtypeStruct((bsz, n, d), F32),
                   jax.ShapeDtypeStruct((bsz, n, d), BF16),
                   jax.ShapeDtypeStruct((bsz, N_EXPERTS, n), F32)],
        compiler_params=_params(("arbitrary", "arbitrary")),
        name="merge",
    )(ml, hg, at, ogate, xs, modtok, g_ml[None, :], g_hg[None, :], w_out.astype(BF16), g2[None, :],
      router_w.T, same_b)


def _route_kernel(aff_ref, rank_ref, *, cap, n):
    aff = aff_ref[0]

    def search(i, thr):
        cand = thr | (jnp.int32(1) << (30 - i))
        cnt = jnp.sum(jnp.where(aff >= pltpu.bitcast(cand, F32), 1, 0), axis=-1, keepdims=True)
        return jnp.where(cnt >= cap, cand, thr)

    thr = lax.fori_loop(0, 31, search, jnp.zeros((N_EXPERTS, 1), jnp.int32))
    above = aff >= pltpu.bitcast(thr + 1, F32)
    tied = jnp.logical_and(aff >= pltpu.bitcast(thr, F32), jnp.logical_not(above))
    need = cap - jnp.sum(jnp.where(above, 1, 0), axis=-1, keepdims=True)

    tl = min(n, 256)
    r = lax.broadcasted_iota(jnp.int32, (tl, tl), 0)
    c = lax.broadcasted_iota(jnp.int32, (tl, tl), 1)
    before = jnp.where(r < c, 1.0, 0.0).astype(BF16)

    def excl_cumsum(mask):
        parts, carry = [], jnp.zeros((N_EXPERTS, 1), F32)
        for j in range(n // tl):
            m = jnp.where(mask[:, j * tl:(j + 1) * tl], 1.0, 0.0)
            parts.append(_mm(m.astype(BF16), before) + carry)
            carry = carry + jnp.sum(m, axis=-1, keepdims=True)
        return jnp.concatenate(parts, axis=-1).astype(jnp.int32)

    keep = jnp.logical_or(above, jnp.logical_and(tied, excl_cumsum(tied) < need))
    rank_ref[0] = jnp.where(keep, excl_cumsum(keep), -1)


def _route(aff_t, cap):
    bsz, e, n = aff_t.shape
    kern = functools.partial(_route_kernel, cap=cap, n=n)
    return pl.pallas_call(
        kern,
        grid=(bsz,),
        in_specs=[pl.BlockSpec((1, e, n), lambda i: (i, 0, 0))],
        out_specs=pl.BlockSpec((1, e, n), lambda i: (i, 0, 0)),
        out_shape=jax.ShapeDtypeStruct((bsz, e, n), jnp.int32),
        compiler_params=_params(("arbitrary",)),
        name="route",
    )(aff_t)


def _pick_and_gate(rank_row, aff_row, cap):
    slot = lax.broadcasted_iota(jnp.int32, (cap, rank_row.shape[1]), 0)
    chosen = rank_row == slot
    gate = jnp.sum(jnp.where(chosen, aff_row, 0.0), axis=-1, keepdims=True)
    return jnp.where(chosen, 1.0, 0.0).astype(BF16), gate


def _ffn_experts(xs, picks, gates, x1_ref, g2_ref, wg_ref, wu_ref, wd_ref, o_ref, cap):
    e = pl.program_id(1)
    hid = (_silu(_mm(xs, wg_ref[0, 0])) * _mm(xs, wu_ref[0, 0])).astype(BF16)
    out = _mm(hid, wd_ref[0, 0])

    @pl.when(e == 0)
    def _():
        o_ref[...] = jnp.zeros_like(o_ref)

    d = out.shape[1]
    for b, (pick, gate) in enumerate(zip(picks, gates)):
        out_b = (out[b * cap:(b + 1) * cap] * gate).astype(BF16)
        for j in range(d // 256):
            o_ref[b, :, j * 256:(j + 1) * 256] += _tn(pick, out_b[:, j * 256:(j + 1) * 256])

    @pl.when(e == pl.num_programs(1) - 1)
    def _():
        o_ref[...] = x1_ref[...] + g2_ref[...] * o_ref[...]


def _ffn_kernel(rank_ref, aff_ref, h_ref, x1_ref, g2_ref, wg_ref, wu_ref, wd_ref, o_ref, *, cap, bb):
    picks, gates, xs = [], [], []
    for b in range(bb):
        pick, gate = _pick_and_gate(rank_ref[b, 0], aff_ref[b, 0], cap)
        picks.append(pick)
        gates.append(gate)
        xs.append(_mm(pick, h_ref[b].astype(BF16)).astype(BF16))
    _ffn_experts(jnp.concatenate(xs, axis=0), picks, gates, x1_ref, g2_ref, wg_ref, wu_ref, wd_ref, o_ref, cap)


def _ffn(rank, aff_t, h2, x1, g2mod, wg, wu, wd, layer, cap, bb):
    bsz, n, d = h2.shape
    n_exp = wg.shape[1]
    once = dict(pipeline_mode=pl.Buffered(1))
    per_expert = lambda width: pl.BlockSpec((bb, 1, 1, width), lambda i, e: (i, e, 0, 0))
    common = [pl.BlockSpec((bb, n, d), lambda i, e: (i, 0, 0), **once),
              pl.BlockSpec((bb, 1, d), lambda i, e: (i, 0, 0)),
              pl.BlockSpec((1, 1, d, d), lambda i, e: (layer, e, 0, 0)),
              pl.BlockSpec((1, 1, d, d), lambda i, e: (layer, e, 0, 0)),
              pl.BlockSpec((1, 1, d, d), lambda i, e: (layer, e, 0, 0))]
    out_spec = pl.BlockSpec((bb, n, d), lambda i, e: (i, 0, 0))
    out_shape = jax.ShapeDtypeStruct((bsz, n, d), F32)
    rank4, aff4 = rank.reshape(bsz, n_exp, 1, n), aff_t.reshape(bsz, n_exp, 1, n)
    return pl.pallas_call(
        functools.partial(_ffn_kernel, cap=cap, bb=bb),
        grid=(bsz // bb, n_exp),
        in_specs=[per_expert(n), per_expert(n), pl.BlockSpec((bb, n, d), lambda i, e: (i, 0, 0), **once)] + common,
        out_specs=out_spec, out_shape=out_shape,
        compiler_params=_params(("arbitrary", "arbitrary")),
        name="expert_ffn",
    )(rank4, aff4, h2, x1, g2mod, wg, wu, wd)


def _rope_tables(ctx_len, seq):
    half = MLA_ROPE // 2
    inv = ROPE_THETA ** (-jnp.arange(0, half, 2, dtype=F32) / half)
    rows = seq // GRID_W
    row_pos = jnp.repeat(jnp.arange(rows), GRID_W).astype(F32)
    col_pos = jnp.broadcast_to(jnp.arange(GRID_W), (rows, GRID_W)).reshape(-1).astype(F32)

    def cs(pos):
        ang = pos[:, None] * inv[None, :]
        ang = jnp.concatenate([ang, ang], axis=-1)
        return jnp.cos(ang), jnp.sin(ang)

    cos_r, sin_r = cs(row_pos)
    cos_c, sin_c = cs(col_pos)
    zeros = lambda n: jnp.zeros((seq, n), F32)
    ones = lambda n: jnp.ones((seq, n), F32)
    tail = HEAD_PAD - MLA_DQK
    cos = jnp.concatenate([ones(MLA_NOPE), cos_r, cos_c, ones(tail)], axis=-1)
    sin = jnp.concatenate([zeros(MLA_NOPE), sin_r, sin_c, zeros(tail)], axis=-1)
    ident = lambda a, fill: jnp.concatenate([jnp.full((ctx_len, HEAD_PAD), fill, F32), a], axis=0)
    return ident(cos, 1.0), ident(sin, 0.0)


def kernel(x, c, ctx, c_ctx, ada_w, ada_b, norm1_g, norm2_g, w_in, b_in, ml_conv_w, ml_conv_b, ml_norm_g, hg_lb_logits, hg_norm_g, mla_q_norm_g, mla_w_uq, mla_kv_norm_g, mla_w_ukv, mla_q_qk_g, mla_k_qk_g, w_out, router_w, ex_w_gate, ex_w_up, ex_w_down):
    bsz, seq, d = x.shape
    ctx_len = ctx.shape[1]
    depth = ada_w.shape[0]
    t = ctx_len + seq
    tm = 256 if ctx_len % 256 == 0 else 128
    assert ctx_len % tm == 0 and seq % tm == 0 and ctx_len % ML_CHUNK == 0 and seq % ML_CHUNK == 0 and seq % GRID_W == 0
    nct = ctx_len // tm

    rows = -(-(bsz + 1) // 8) * 8
    cc = jnp.concatenate([c, c_ctx[None, :], jnp.zeros((rows - bsz - 1, d), F32)], axis=0)
    mod = _modulation(cc, ada_w, ada_b)
    cos, sin = _rope_tables(ctx_len, seq)
    bb_ctx = max(bb for bb in (8, 4, 2, 1) if bsz % bb == 0)

    wg, wu, wd = (w.astype(BF16) for w in (ex_w_gate, ex_w_up, ex_w_down))
    lat = x
    for layer in range(depth):
        need_ctx = layer < depth - 1
        m = mod[layer]
        modtok = jnp.stack([jnp.broadcast_to(m[bsz], (bsz, 6 * d)), m[:bsz]], axis=1).reshape(bsz, 2, 6, d)
        oml, ohqv, ohf, ogate, omla, ogt = _in_proj(ctx, lat, modtok, norm1_g[layer], w_in[layer], b_in[layer], tm, nct)
        gates_t = ogt.reshape(bsz, 16, t // ML_CHUNK, ML_CHUNK).transpose(0, 2, 1, 3)
        ml = _mlstm(oml, omla, gates_t, ml_conv_w[layer], ml_conv_b[layer], ctx_len, tm)
        hg = _hgrn(ohqv, ohf, hg_lb_logits, ctx_len, layer)
        q, k, v = _mla_prep(omla, mla_q_norm_g[layer], mla_kv_norm_g[layer], mla_w_uq[layer], mla_w_ukv[layer],
                            mla_q_qk_g[layer], mla_k_qk_g[layer], cos, sin, tm)
        at_j0 = 0 if need_ctx else nct
        at = _attention(q, k, v, ctx_len, tm, at_j0)
        g2mod = modtok[:, :, 5:6, :]

        def post(xs, j0, sel, bb):
            x1, h2, aff = _merge(ml, hg, at, ogate, xs, modtok, ml_norm_g[layer], hg_norm_g[layer], w_out[layer],
                                 norm2_g[layer], router_w[layer], tm, j0, sel, at_j0)
            cap = EC_CAPACITY * xs.shape[1] // N_EXPERTS
            rank = _route(aff, cap)
            return _ffn(rank, aff, h2, x1, g2mod[:, sel], wg, wu, wd, layer, cap, bb)

        new_lat = post(lat, nct, 1, 1)
        if need_ctx:
            ctx = post(ctx, 0, 0, bb_ctx)
        lat = new_lat
    return lat
```

```python
import functools

import numpy as np
import jax
import jax.numpy as jnp
from jax import lax
from jax.experimental import pallas as pl
from jax.experimental.pallas import tpu as pltpu

F32 = jnp.float32
BF16 = jnp.bfloat16
HIGHEST = lax.Precision.HIGHEST

D_MODEL = 1024
GRID_W = 64
ML_HEADS = 4
ML_DH = 64
ML_W = 256
HG_HEADS = 4
HG_W = 256
MLA_HEADS = 8
MLA_NOPE = 64
MLA_ROPE = 32
MLA_DQK = 96
MLA_DV = 64
MLA_Q_RANK = 256
MLA_KV_RANK = 128
N_EXPERTS = 16
EC_CAPACITY = 2
CHUNK = 32
ML_CHUNK = 128
ROPE_THETA = 10000.0
EPS = 1e-6
HEAD_PAD = 128
LANES = 128
N_LEVELS = 5
N_MM_LEVELS = 3
SLAB = HG_HEADS * CHUNK
ML_SLAB = ML_HEADS * ML_CHUNK
VMEM_LIMIT = 52 * 1024 * 1024

_O_QK, _O_V, _O_OG, _O_GATES, _O_HQ, _O_HI, _O_HGATE, _O_HF, _O_CQ, _O_CKV, _O_KR, _O_END = (
    0, 512, 768, 1024, 1040, 1296, 1552, 1808, 2320, 2576, 2704, 2736)
_W_ML = 2 * ML_W + ML_W
_W_HQV = 2 * HG_W
_W_HF = 2 * HG_W
_W_GATE = ML_W + HG_W
_W_MLA = MLA_Q_RANK + MLA_KV_RANK + LANES
_C_ML, _C_HQV, _C_HF, _C_GATE, _C_MLA, _C_END = np.cumsum([0, _W_ML, _W_HQV, _W_HF, _W_GATE, _W_MLA]).tolist()
MISC_BLOCK = (MLA_Q_RANK + MLA_KV_RANK) // LANES
MOD_TILE = 1024
TOKEN_TILE = 256
SCATTER_COLS = 256


def _nt(a, b, **kw):
    return lax.dot_general(a, b, (((1,), (1,)), ((), ())), preferred_element_type=F32, **kw)


def _tn(a, b, **kw):
    return lax.dot_general(a, b, (((0,), (0,)), ((), ())), preferred_element_type=F32, **kw)


def _mm(a, b, **kw):
    return jnp.dot(a, b, preferred_element_type=F32, **kw)


def _split(x):
    hi = x.astype(BF16)
    return hi, (x - hi.astype(F32)).astype(BF16)


def _silu(x):
    return x * jax.nn.sigmoid(x)


def _log_sigmoid(x):
    return jnp.minimum(x, 0.0) - jnp.log1p(jnp.exp(-jnp.abs(x)))


def _rms_rows(x, g):
    return x * lax.rsqrt(jnp.mean(x * x, axis=-1, keepdims=True) + EPS) * g


def _params(sem=None):
    return pltpu.CompilerParams(dimension_semantics=sem, vmem_limit_bytes=VMEM_LIMIT)


def _const_spec(a):
    return pl.BlockSpec(a.shape, lambda *_: (0,) * a.ndim)


def _mod_kernel(c_ref, w_ref, b_ref, o_ref):
    s = _silu(c_ref[...]).astype(BF16)
    o_ref[0] = _mm(s, w_ref[0].astype(BF16)) + b_ref[0]


def _modulation(cc, ada_w, ada_b):
    n_layers, d, n6 = ada_w.shape
    rows = cc.shape[0]
    tn = MOD_TILE
    return pl.pallas_call(
        _mod_kernel,
        grid=(n_layers, n6 // tn),
        in_specs=[pl.BlockSpec((rows, d), lambda l, j: (0, 0)),
                  pl.BlockSpec((1, d, tn), lambda l, j: (l, 0, j)),
                  pl.BlockSpec((1, 1, tn), lambda l, j: (l, 0, j))],
        out_specs=pl.BlockSpec((1, rows, tn), lambda l, j: (l, 0, j)),
        out_shape=jax.ShapeDtypeStruct((n_layers, rows, n6), F32),
        compiler_params=_params(("arbitrary", "arbitrary")),
        name="modulation",
    )(cc, ada_w, ada_b.reshape(n_layers, 1, n6))


def _in_kernel(c_ref, l_ref, mod_ref, g_ref, w_ref, b_ref, wgt_ref, bgt_ref,
               oml_ref, ohqv_ref, ohf_ref, ogate_ref, omla_ref, ogt_ref, *, nct):
    x = jnp.where(pl.program_id(1) < nct, c_ref[0], l_ref[0])
    sh = mod_ref[0, 0, 0:1, :]
    sc = mod_ref[0, 0, 1:2, :]
    h = (_rms_rows(x, g_ref[...]) * (1.0 + sc) + sh).astype(BF16)
    proj = lambda lo, hi: _mm(h, w_ref[:, lo:hi]) + b_ref[:, lo:hi]
    oml_ref[0] = proj(_C_ML, _C_HQV).astype(BF16)
    ohqv_ref[0] = proj(_C_HQV, _C_HF).astype(BF16)
    ohf_ref[0] = proj(_C_HF, _C_GATE)
    ogate_ref[0] = proj(_C_GATE, _C_MLA)
    omla_ref[0] = proj(_C_MLA, _C_END)
    ogt_ref[0] = _nt(wgt_ref[...], h) + bgt_ref[...]


def _in_proj(ctx_x, lat_x, modtok, g1, w_in, b_in, tm, nct):
    bsz, seq, d = lat_x.shape
    t = ctx_x.shape[1] + seq
    src, sign = _rotate_half_columns(MLA_ROPE)
    zeros = lambda n: jnp.zeros((d, n), F32)
    w = jnp.concatenate([
        w_in[:, _O_QK:_O_OG], w_in[:, _O_HQ:_O_HGATE], w_in[:, _O_HF:_O_CQ],
        w_in[:, _O_OG:_O_GATES], w_in[:, _O_HGATE:_O_HF], w_in[:, _O_CQ:_O_KR],
        w_in[:, _O_GATES:_O_HQ], zeros(48), w_in[:, _O_KR:_O_END],
        w_in[:, _O_KR:_O_END][:, src] * sign], axis=1).astype(BF16)
    zb = lambda n: jnp.zeros((n,), F32)
    b = jnp.concatenate([
        b_in[_O_QK:_O_OG], b_in[_O_HQ:_O_HGATE], b_in[_O_HF:_O_CQ],
        b_in[_O_OG:_O_GATES], b_in[_O_HGATE:_O_HF], b_in[_O_CQ:_O_KR],
        b_in[_O_GATES:_O_HQ], zb(48), b_in[_O_KR:_O_END], b_in[_O_KR:_O_END][src] * sign])[None, :]
    wgt = w_in[:, _O_GATES:_O_HQ].T.astype(BF16)
    bgt = b_in[_O_GATES:_O_HQ][:, None]
    nw = w.shape[1]
    tok = lambda width: pl.BlockSpec((1, tm, width), lambda i, j: (i, j, 0))
    full = lambda shape: pl.BlockSpec(shape, lambda i, j: (0,) * len(shape))
    return pl.pallas_call(
        functools.partial(_in_kernel, nct=nct),
        grid=(bsz, t // tm),
        in_specs=[pl.BlockSpec((1, tm, d), lambda i, j: (i, jnp.minimum(j, nct - 1), 0)),
                  pl.BlockSpec((1, tm, d), lambda i, j: (i, jnp.maximum(j - nct, 0), 0)),
                  pl.BlockSpec((1, 1, 6, d), lambda i, j: (i, jnp.where(j >= nct, 1, 0), 0, 0)),
                  full((1, d)), full((d, nw)), full((1, nw)), full((16, d)), full((16, 1))],
        out_specs=[tok(_W_ML), tok(_W_HQV), tok(_W_HF), tok(_W_GATE), tok(_W_MLA),
                   pl.BlockSpec((1, 16, tm), lambda i, j: (i, 0, j))],
        out_shape=[jax.ShapeDtypeStruct((bsz, t, _W_ML), BF16),
                   jax.ShapeDtypeStruct((bsz, t, _W_HQV), BF16),
                   jax.ShapeDtypeStruct((bsz, t, _W_HF), F32),
                   jax.ShapeDtypeStruct((bsz, t, _W_GATE), F32),
                   jax.ShapeDtypeStruct((bsz, t, _W_MLA), F32),
                   jax.ShapeDtypeStruct((bsz, 16, t), F32)],
        compiler_params=_params(("arbitrary", "arbitrary")),
        name="in_proj",
    )(ctx_x, lat_x, modtok, g1[None, :], w, b, wgt, bgt)


def _dir1_chunk(i, n_ctx_chunks, n_chunks):
    return jnp.where(i < n_ctx_chunks, n_ctx_chunks - 1 - i, n_chunks - 1 - (i - n_ctx_chunks))


def _head_tables(chunk):
    feat_head = np.arange(ML_W) // ML_DH
    stack = (np.arange(ML_HEADS * chunk)[:, None] // chunk == feat_head[None, :]).astype(np.float32)
    square = (feat_head[:, None] == feat_head[None, :]).astype(np.float32)
    return jnp.asarray(stack, BF16), jnp.asarray(square, F32)


def _stack_heads(x_bf, same_bf):
    return jnp.concatenate([x_bf] * ML_HEADS, axis=0) * same_bf


class _Chain:
    def __init__(self, **kw):
        self.__dict__.update(kw)


def _samples_per_step(bsz):
    return 2 if bsz % 2 == 0 else 1


def _mm_split(a, b):
    hi, lo = _split(a)
    return _mm(hi, b) + _mm(lo, b)


def _mlstm_tables():
    t = np.arange(ML_CHUNK)
    lower = (t[:, None] >= t[None, :]).astype(np.float32)
    tri = np.stack([lower, lower.T])
    neg = np.stack([np.tile(np.where(m > 0, 0.0, -np.inf), (1, ML_HEADS)) for m in (lower, lower.T)]).astype(np.float32)
    spread = np.zeros((2, LANES, ML_W), np.float32)
    spread_s = np.zeros((2, LANES, ML_SLAB), np.float32)
    for d in range(2):
        for h in range(ML_HEADS):
            spread[d, d * 8 + 4 + h, h * ML_DH:(h + 1) * ML_DH] = 1.0
            spread_s[d, d * 8 + 4 + h, h * ML_CHUNK:(h + 1) * ML_CHUNK] = 1.0
    as_bf = lambda a: jnp.asarray(a, BF16)
    return (as_bf(tri), jnp.asarray(neg), as_bf(spread), as_bf(spread.transpose(0, 2, 1)),
            as_bf(spread_s), as_bf(spread_s.transpose(0, 2, 1)))


def _mlstm_kernel(ml_ref, gc_ref, gt_ref, cw_ref, cb_ref, tri_ref, neg_ref, spread_ref, gather_ref, spread_s_ref,
                  gather_s_ref, sameb_ref, samef_ref, o_ref, qk_s, c_s, n_s, m_s, *, t, ctx_len, tm):
    n_chunks = t // ML_CHUNK
    n_ctx_chunks = ctx_len // ML_CHUNK
    n_samples = ml_ref.shape[0]
    cw = cw_ref[...]
    cb = cb_ref[...]
    rid = lax.broadcasted_iota(jnp.int32, (tm, 2 * ML_W), 0)
    for b in range(n_samples):
        for j in range(t // tm):
            r0 = j * tm
            cur = ml_ref[b, r0:r0 + tm, 0:2 * ML_W].astype(F32)
            up = pltpu.roll(cur, 1, 0)
            if r0 in (0, ctx_len):
                up = jnp.where(rid == 0, 0.0, up)
            else:
                up = jnp.where(rid == 0, ml_ref[b, r0 - 16:r0, 0:2 * ML_W].astype(F32)[15:16], up)
            dn = pltpu.roll(cur, tm - 1, 0)
            if r0 + tm in (ctx_len, t):
                dn = jnp.where(rid == tm - 1, 0.0, dn)
            else:
                dn = jnp.where(rid == tm - 1, ml_ref[b, r0 + tm:r0 + tm + 16, 0:2 * ML_W].astype(F32)[0:1], dn)
            y = cw[0:1, :] * up + cw[1:2, :] * cur + cw[2:3, :] * dn + cb
            qk_s[b, r0:r0 + tm, :] = _silu(y)

    o_ref[...] = jnp.zeros_like(o_ref)
    c_s[...] = jnp.zeros_like(c_s)
    n_s[...] = jnp.zeros_like(n_s)
    m_s[...] = jnp.zeros_like(m_s)
    gate_lane = lax.broadcasted_iota(jnp.int32, (ML_CHUNK, LANES), 1) < 16
    tok = lax.broadcasted_iota(jnp.int32, (ML_CHUNK, LANES), 0)

    def load(ch):
        d, b = ch.d, ch.b
        ch.rows = pl.ds(pl.multiple_of(ch.c * ML_CHUNK, ML_CHUNK), ML_CHUNK)
        qk = qk_s[b, ch.rows, :]
        ch.qb = qk[:, 0:ML_W].astype(BF16)
        ch.k = qk[:, ML_W:2 * ML_W] * (ML_DH ** -0.5)
        ch.vb = ml_ref[b, ch.rows, 2 * ML_W:3 * ML_W]
        ch.gc = jnp.where(gate_lane, gc_ref[b, ch.rows, :], 0.0)
        ch.gt = gt_ref[b, ch.c]
        hi, lo = _split(_log_sigmoid(ch.gc))
        ch.b_c = _mm(tri_ref[d], hi) + _mm(tri_ref[d], lo)
        ch.b_r = _mm_split(_log_sigmoid(ch.gt), tri_ref[1 - d])
        ch.s_raw = _nt(ch.qb, _stack_heads(ch.k.astype(BF16), sameb_ref[...]))
        ch.qn = _mm((qk[:, 0:ML_W] * n_s[b, d][0:1, :]).astype(BF16), gather_ref[d])
        ch.c_mat = c_s[b, d]
        ch.qc = _mm(ch.qb, ch.c_mat.astype(BF16))

    def stabilise(ch):
        d = ch.d
        ch.u = pltpu.roll(ch.gc, 4, 1) - ch.b_c
        cmax = ch.u
        for step in [1 << i for i in range(ML_CHUNK.bit_length() - 1)]:
            if d == 0:
                moved = jnp.where(tok >= step, pltpu.roll(cmax, step, 0), -jnp.inf)
            else:
                moved = jnp.where(tok < ML_CHUNK - step, pltpu.roll(cmax, ML_CHUNK - step, 0), -jnp.inf)
            cmax = jnp.maximum(cmax, moved)
        ch.m = m_s[ch.b, d][0:1, :]
        inter = ch.b_c + ch.m
        ch.m_t = jnp.maximum(inter, ch.b_c + cmax)
        ch.w_st = jnp.exp(inter - ch.m_t)
        last = ML_CHUNK - 1 if d == 0 else 0
        ch.bend = ch.b_c[last:last + 1, :]
        ch.m_new = jnp.maximum(ch.bend + ch.m, ch.bend + cmax[last:last + 1, :])
        fcols = [d * 8 + 4 + h for h in range(ML_HEADS)]
        icols = [d * 8 + h for h in range(ML_HEADS)]
        ch.u_row = jnp.concatenate([ch.gt[i:i + 1, :] - ch.b_r[f:f + 1, :] for i, f in zip(icols, fcols)], axis=1)
        ch.bm_wide = _mm_split(ch.b_c - ch.m_t, spread_s_ref[d])
        ch.wk_wide = _mm(jnp.exp(ch.bend + ch.u - ch.m_new).astype(BF16), spread_ref[d])
        ch.w_old = jnp.broadcast_to(jnp.exp(ch.bend + ch.m - ch.m_new), (8, LANES))
        ch.wold_wide = _mm_split(ch.w_old, spread_ref[d])

    def weigh(ch):
        d = ch.d
        s = ch.s_raw * jnp.exp(ch.bm_wide + ch.u_row + neg_ref[d])
        sb = s.astype(BF16)
        ch.intra = _mm(sb, _stack_heads(ch.vb, sameb_ref[...]))
        ch.rowsum = _mm(sb, gather_s_ref[d])
        kw = ch.k * ch.wk_wide
        ch.dc = _tn(kw.astype(BF16), ch.vb)
        ch.dn = jnp.sum(kw, axis=0, keepdims=True)

    def normalise(ch):
        d = ch.d
        den = ch.rowsum + ch.w_st * ch.qn
        r = 1.0 / jnp.maximum(jnp.abs(den), jnp.exp(-ch.m_t))
        ch.r_wide = _mm(r.astype(BF16), spread_ref[d])
        ch.wr_wide = _mm((ch.w_st * r).astype(BF16), spread_ref[d])

    def store(ch):
        b, d = ch.b, ch.d
        o_ref[b, ch.rows, :] += ch.intra * ch.r_wide + ch.qc * ch.wr_wide
        w_old = ch.wold_wide[0:1, :]
        c_s[b, d] = w_old * ch.c_mat + ch.dc * samef_ref[...]
        n_s[b, d] = jnp.broadcast_to(w_old * n_s[b, d][0:1, :] + ch.dn, (8, ML_W))
        m_s[b, d] = jnp.broadcast_to(ch.m_new, (8, LANES))

    def body(i, carry):
        chunk = (i, _dir1_chunk(i, n_ctx_chunks, n_chunks))
        chains = [_Chain(b=b, d=d, c=chunk[d]) for b in range(n_samples) for d in range(2)]
        for stage in (load, stabilise, weigh, normalise, store):
            for ch in chains:
                stage(ch)
        return carry

    lax.fori_loop(0, n_chunks, body, 0)


def _mlstm(oml, omla, gates_t, conv_w, conv_b, ctx_len, tm):
    bsz, t, _ = oml.shape
    n_chunks = t // ML_CHUNK
    kern = functools.partial(_mlstm_kernel, t=t, ctx_len=ctx_len, tm=tm)
    consts = _mlstm_tables() + _head_tables(ML_CHUNK)
    sps = _samples_per_step(bsz)
    return pl.pallas_call(
        kern,
        grid=(bsz // sps,),
        in_specs=[pl.BlockSpec((sps, t, _W_ML), lambda i: (i, 0, 0)),
                  pl.BlockSpec((sps, t, LANES), lambda i: (i, 0, MISC_BLOCK)),
                  pl.BlockSpec((sps, n_chunks, 16, ML_CHUNK), lambda i: (i, 0, 0, 0)),
                  pl.BlockSpec((3, 2 * ML_W), lambda i: (0, 0)),
                  pl.BlockSpec((1, 2 * ML_W), lambda i: (0, 0))] + [_const_spec(a) for a in consts],
        out_specs=pl.BlockSpec((sps, t, ML_W), lambda i: (i, 0, 0)),
        out_shape=jax.ShapeDtypeStruct((bsz, t, ML_W), F32),
        scratch_shapes=[pltpu.VMEM((sps, t, 2 * ML_W), F32),
                        pltpu.VMEM((sps, 2, ML_W, ML_W), F32),
                        pltpu.VMEM((sps, 2, 8, ML_W), F32),
                        pltpu.VMEM((sps, 2, 8, LANES), F32)],
        compiler_params=_params(("arbitrary",)),
        name="mlstm",
    )(oml, omla, gates_t, conv_w, conv_b[None, :], *consts)


def _hgrn_tables():
    sums = np.zeros((2, (N_MM_LEVELS + 1) * CHUNK, CHUNK), np.float32)
    pairs = np.zeros((2, N_LEVELS + 1, CHUNK, CHUNK), np.float32)
    sign = np.zeros((2, N_LEVELS - N_MM_LEVELS, CHUNK, 1), np.float32)
    for d in range(2):
        p = (lambda a: a) if d == 0 else (lambda a: CHUNK - 1 - a)
        for t in range(CHUNK):
            pairs[d, 0, p(t), p(t)] = 1.0
            for s in range(t + 1):
                sums[d, p(t), p(s)] = 1.0
        for l in range(N_LEVELS):
            w = 1 << l
            for t in range(CHUNK):
                ref = t - (t % (2 * w)) + w - 1
                if l < N_MM_LEVELS:
                    lo, hi = (ref + 1, t) if t > ref else (t + 1, ref)
                    for s in range(lo, hi + 1):
                        sums[d, (l + 1) * CHUNK + p(t), p(s)] = 1.0
                else:
                    sign[d, l - N_MM_LEVELS, p(t), 0] = 1.0 if t > ref else -1.0
                if t > ref:
                    for s in range(ref - w + 1, ref + 1):
                        pairs[d, l + 1, p(t), p(s)] = 1.0
    pairs = np.tile(pairs, (1, 1, 1, HG_HEADS))
    sign = np.broadcast_to(sign, sign.shape[:3] + (HG_W,))
    return jnp.asarray(sums, BF16), jnp.asarray(sign), jnp.asarray(pairs)


def _hgrn_ref_rows(cum, d, l):
    w = 1 << l
    parts = []
    for base in range(0, CHUNK, 2 * w):
        r = base + w - 1 if d == 0 else base + w
        parts.append(jnp.broadcast_to(cum[r:r + 1, :], (2 * w, cum.shape[1])))
    return parts[0] if len(parts) == 1 else jnp.concatenate(parts, axis=0)


def _hgrn_kernel(hqv_ref, hf_ref, lbl_ref, sum_ref, sign_ref, pair_ref, sameb_ref, samef_ref, o_ref, st_s,
                 *, t, ctx_len, layer):
    n_chunks = t // CHUNK
    n_ctx_chunks = ctx_len // CHUNK
    w_all = HG_W

    if layer > 0:
        logits = lbl_ref[...]
        n_layers = logits.shape[0]
        mx = logits[0]
        for l in range(1, n_layers):
            mx = jnp.maximum(mx, logits[l])
        ex = [jnp.exp(logits[l] - mx) for l in range(n_layers)]
        tot = ex[0]
        for l in range(1, n_layers):
            tot = tot + ex[l]
        low_all = ex[1] / tot
        for l in range(2, layer + 1):
            low_all = low_all + ex[l] / tot
        log_low = jnp.log(low_all)
        log_1m_low = jnp.log1p(-low_all)

    o_ref[...] = jnp.zeros_like(o_ref)
    st_s[...] = jnp.zeros_like(st_s)

    def gates(ch):
        b, d = ch.b, ch.d
        ch.rows = pl.ds(pl.multiple_of(ch.c * CHUNK, CHUNK), CHUNK)
        ch.q = _silu(hqv_ref[b, ch.rows, 0:w_all].astype(F32))
        ch.vb = hqv_ref[b, ch.rows, w_all:2 * w_all]
        pre = hf_ref[b, ch.rows, d * w_all:(d + 1) * w_all]
        ls = _log_sigmoid(pre)
        if layer == 0:
            log_f = ls
            ch.key = jnp.exp(ls - pre)
        else:
            lo_, hi_ = log_low[d:d + 1, :], log_1m_low[d:d + 1, :] + ls
            log_f = jnp.maximum(lo_, hi_) + jnp.log1p(jnp.exp(-jnp.abs(lo_ - hi_)))
            ch.key = (1.0 - low_all[d:d + 1, :]) * jnp.exp(ls - pre)
        hi, lo = _split(log_f)
        ch.xb = _mm(sum_ref[d], hi) + _mm(sum_ref[d], lo)
        ch.st = st_s[b, d]
        ch.amat = _nt(ch.q.astype(BF16), _stack_heads(ch.key.astype(BF16), sameb_ref[...])) * pair_ref[d, 0]

    def levels(ch):
        d = ch.d
        cum = ch.xb[0:CHUNK]
        for l in range(N_LEVELS):
            if l < N_MM_LEVELS:
                expo = ch.xb[(l + 1) * CHUNK:(l + 2) * CHUNK]
            else:
                expo = (cum - _hgrn_ref_rows(cum, d, l)) * sign_ref[d, l - N_MM_LEVELS]
            fac = jnp.exp(expo)
            qt = (ch.q * fac).astype(BF16)
            kt = (ch.key * fac).astype(BF16)
            ch.amat = ch.amat + _nt(qt, _stack_heads(kt, sameb_ref[...])) * pair_ref[d, l + 1]
        last = CHUNK - 1 if d == 0 else 0
        bend = cum[last:last + 1, :]
        ch.inter = _nt((ch.q * jnp.exp(cum)).astype(BF16), ch.st.astype(BF16))
        ch.dst = _tn(ch.vb, (ch.key * jnp.exp(bend - cum)).astype(BF16))
        ch.decay = jnp.exp(bend)

    def readout(ch):
        ch.intra = _mm(ch.amat.astype(BF16), _stack_heads(ch.vb, sameb_ref[...]))

    def store(ch):
        o_ref[ch.b, ch.rows, :] += ch.intra + ch.inter
        st_s[ch.b, ch.d] = ch.decay * ch.st + ch.dst * samef_ref[...]

    def body(i, carry):
        chunk = (i, _dir1_chunk(i, n_ctx_chunks, n_chunks))
        chains = [_Chain(b=b, d=d, c=chunk[d]) for b in range(hf_ref.shape[0]) for d in range(2)]
        for stage in (gates, levels, readout, store):
            for ch in chains:
                stage(ch)
        return carry

    lax.fori_loop(0, n_chunks, body, 0)


def _hgrn(ohqv, ohf, lb_logits, ctx_len, layer):
    bsz, t, _ = ohf.shape
    kern = functools.partial(_hgrn_kernel, t=t, ctx_len=ctx_len, layer=layer)
    consts = (lb_logits,) + _hgrn_tables() + _head_tables(CHUNK)
    sps = _samples_per_step(bsz)
    return pl.pallas_call(
        kern,
        grid=(bsz // sps,),
        in_specs=[pl.BlockSpec((sps, t, 2 * HG_W), lambda i: (i, 0, 0)),
                  pl.BlockSpec((sps, t, 2 * HG_W), lambda i: (i, 0, 0))] + [_const_spec(a) for a in consts],
        out_specs=pl.BlockSpec((sps, t, HG_W), lambda i: (i, 0, 0)),
        out_shape=jax.ShapeDtypeStruct((bsz, t, HG_W), F32),
        scratch_shapes=[pltpu.VMEM((sps, 2, HG_W, HG_W), F32)],
        compiler_params=_params(("arbitrary",)),
        name="hgrn",
    )(ohqv, ohf, *consts)


def _mla_prep_kernel(x_ref, gq_ref, gkv_ref, wq_ref, wqr_ref, wk_ref, wv_ref, gqq_ref, gqr_ref, gkk_ref, gkr_ref,
                     cos_ref, sin_ref, q_ref, k_ref, v_ref):
    x = x_ref[0]
    cq = _rms_rows(x[:, 0:MLA_Q_RANK], gq_ref[...]).astype(BF16)
    ckv = _rms_rows(x[:, MLA_Q_RANK:MLA_Q_RANK + MLA_KV_RANK], gkv_ref[...]).astype(BF16)
    misc = x[:, MLA_Q_RANK + MLA_KV_RANK:]
    lane = lax.broadcasted_iota(jnp.int32, misc.shape, 1)
    rope_lane = (lane >= MLA_NOPE) & (lane < MLA_DQK)
    k_rope = jnp.where(rope_lane, misc, 0.0)
    k_rope_rot = jnp.where(rope_lane, pltpu.roll(misc, HEAD_PAD - MLA_ROPE, 1), 0.0)
    q_raw = _mm(cq, wq_ref[...])
    q_rot = _mm(cq, wqr_ref[...])
    k_raw = _mm(ckv, wk_ref[...])
    v_all = _mm(ckv, wv_ref[...]).astype(BF16)
    cos, sin = cos_ref[...], sin_ref[...]

    def norm_rope(xh, xr, g, gr):
        ms = jnp.sum(xh * xh, axis=-1, keepdims=True) * (1.0 / MLA_DQK)
        return (xh * (g * cos) + xr * (gr * sin)) * lax.rsqrt(ms + EPS)

    for h in range(MLA_HEADS):
        sl = slice(h * HEAD_PAD, (h + 1) * HEAD_PAD)
        q = norm_rope(q_raw[:, sl], q_rot[:, sl], gqq_ref[...], gqr_ref[...])
        q_ref[0, h] = (q * (MLA_DQK ** -0.5)).astype(BF16)
        k_ref[0, h] = norm_rope(k_raw[:, sl] + k_rope, k_rope_rot, gkk_ref[...], gkr_ref[...]).astype(BF16)
        v_ref[0, h] = v_all[:, h * MLA_DV:(h + 1) * MLA_DV]


def _rotate_half_columns(n):
    j = np.arange(n)
    first = j % 16 < 8
    return np.where(first, j + 8, j - 8), np.where(first, -1.0, 1.0).astype(np.float32)


def _mla_prep(omla, gq, gkv, w_uq, w_ukv, g_qq, g_kk, cos, sin, tm):
    bsz, t, _ = omla.shape
    hw = MLA_HEADS * HEAD_PAD
    src, sign = _rotate_half_columns(MLA_ROPE)
    src_pad = np.arange(HEAD_PAD)
    src_pad[MLA_NOPE:MLA_DQK] = MLA_NOPE + src
    sign_pad = np.zeros((HEAD_PAD,), np.float32)
    sign_pad[MLA_NOPE:MLA_DQK] = sign
    wq = w_uq.reshape(MLA_Q_RANK, MLA_HEADS, MLA_DQK)
    wq = jnp.pad(wq, ((0, 0), (0, 0), (0, HEAD_PAD - MLA_DQK)))
    wq_rot = (wq[:, :, src_pad] * sign_pad).reshape(MLA_Q_RANK, hw).astype(BF16)
    wq = wq.reshape(MLA_Q_RANK, hw).astype(BF16)
    wkv = w_ukv.reshape(MLA_KV_RANK, MLA_HEADS, MLA_NOPE + MLA_DV)
    wk = jnp.pad(wkv[:, :, :MLA_NOPE], ((0, 0), (0, 0), (0, HEAD_PAD - MLA_NOPE))).reshape(MLA_KV_RANK, hw).astype(BF16)
    wv = wkv[:, :, MLA_NOPE:].reshape(MLA_KV_RANK, MLA_HEADS * MLA_DV).astype(BF16)
    padg = lambda g: jnp.pad(g, (0, HEAD_PAD - MLA_DQK))
    gains = [padg(g_qq)[None, :], (padg(g_qq)[src_pad] * jnp.abs(sign_pad))[None, :],
             padg(g_kk)[None, :], (padg(g_kk)[src_pad] * jnp.abs(sign_pad))[None, :]]
    full = lambda shape: pl.BlockSpec(shape, lambda i, j: (0,) * len(shape))
    rope_spec = pl.BlockSpec((tm, HEAD_PAD), lambda i, j: (j, 0))
    return pl.pallas_call(
        _mla_prep_kernel,
        grid=(bsz, t // tm),
        in_specs=[pl.BlockSpec((1, tm, _W_MLA), lambda i, j: (i, j, 0)),
                  full((1, MLA_Q_RANK)), full((1, MLA_KV_RANK)),
                  full((MLA_Q_RANK, hw)), full((MLA_Q_RANK, hw)), full((MLA_KV_RANK, hw)),
                  full((MLA_KV_RANK, MLA_HEADS * MLA_DV))] + [full((1, HEAD_PAD))] * 4 + [rope_spec, rope_spec],
        out_specs=[pl.BlockSpec((1, MLA_HEADS, tm, HEAD_PAD), lambda i, j: (i, 0, j, 0)),
                   pl.BlockSpec((1, MLA_HEADS, tm, HEAD_PAD), lambda i, j: (i, 0, j, 0)),
                   pl.BlockSpec((1, MLA_HEADS, tm, MLA_DV), lambda i, j: (i, 0, j, 0))],
        out_shape=[jax.ShapeDtypeStruct((bsz, MLA_HEADS, t, HEAD_PAD), BF16),
                   jax.ShapeDtypeStruct((bsz, MLA_HEADS, t, HEAD_PAD), BF16),
                   jax.ShapeDtypeStruct((bsz, MLA_HEADS, t, MLA_DV), BF16)],
        compiler_params=_params(("arbitrary", "arbitrary")),
        name="mla_prep",
    )(omla, gq[None, :], gkv[None, :], wq, wq_rot, wk, wv, *gains, cos, sin)


def _attn_kernel(q_ref, k_ref, v_ref, o_ref, *, ctx_len, nct, j0):
    is_ctx = pl.program_id(1) + j0 < nct

    def attend(n_keys):
        scores = lambda h: _nt(q_ref[0, h], k_ref[0, h, 0:n_keys, :])
        s_next = scores(0)
        for h in range(MLA_HEADS):
            s, s_next = s_next, (scores(h + 1) if h + 1 < MLA_HEADS else None)
            p = jnp.exp(s - jnp.max(s, axis=-1, keepdims=True))
            o = _mm(p.astype(BF16), v_ref[0, h, 0:n_keys, :]) / jnp.sum(p, axis=-1, keepdims=True)
            o_ref[0, :, h * MLA_DV:(h + 1) * MLA_DV] = o

    @pl.when(is_ctx)
    def _():
        attend(ctx_len)

    @pl.when(jnp.logical_not(is_ctx))
    def _():
        attend(k_ref.shape[2])


def _attention(q, k, v, ctx_len, tm, j0):
    bsz, _, t, _ = q.shape
    nct = ctx_len // tm
    nq = t // tm - j0
    kern = functools.partial(_attn_kernel, ctx_len=ctx_len, nct=nct, j0=j0)
    return pl.pallas_call(
        kern,
        grid=(bsz, nq),
        in_specs=[pl.BlockSpec((1, MLA_HEADS, tm, HEAD_PAD), lambda i, j: (i, 0, j + j0, 0)),
                  pl.BlockSpec((1, MLA_HEADS, t, HEAD_PAD), lambda i, j: (i, 0, 0, 0)),
                  pl.BlockSpec((1, MLA_HEADS, t, MLA_DV), lambda i, j: (i, 0, 0, 0))],
        out_specs=pl.BlockSpec((1, tm, MLA_HEADS * MLA_DV), lambda i, j: (i, j, 0)),
        out_shape=jax.ShapeDtypeStruct((bsz, nq * tm, MLA_HEADS * MLA_DV), F32),
        compiler_params=_params(("arbitrary", "arbitrary")),
        name="attention",
    )(q, k, v)


def _merge_kernel(ml_ref, hg_ref, at_ref, gate_ref, x_ref, mod_ref, gml_ref, ghg_ref, wo_ref,
                  g2_ref, rwt_ref, same_ref, x1_ref, h2_ref, aff_ref):
    head_mean = same_ref[...] * (1.0 / ML_DH)
    w_hi, w_lo = _split(rwt_ref[...])
    half = x_ref.shape[1] // 2

    def mean_squares(ch):
        ch.ml, ch.hg = ml_ref[0, ch.rows, :], hg_ref[0, ch.rows, :]
        hi, lo = _split(ch.ml * ch.ml)
        ch.ms_ml = _mm(hi, head_mean) + _mm(lo, head_mean)
        hi, lo = _split(ch.hg * ch.hg)
        ch.ms_hg = _mm(hi, head_mean) + _mm(lo, head_mean)

    def project(ch):
        gates = gate_ref[0, ch.rows, :]
        y_ml = ch.ml * lax.rsqrt(ch.ms_ml + EPS) * gml_ref[...] * jax.nn.sigmoid(gates[:, 0:ML_W])
        y_hg = ch.hg * lax.rsqrt(ch.ms_hg + EPS) * ghg_ref[...] * _silu(gates[:, ML_W:ML_W + HG_W])
        ch.y = (_mm(y_ml.astype(BF16), wo_ref[0:ML_W, :])
                + _mm(y_hg.astype(BF16), wo_ref[ML_W:ML_W + HG_W, :])
                + _mm(at_ref[0, ch.rows, :].astype(BF16), wo_ref[ML_W + HG_W:, :]))

    def route(ch):
        x1 = x_ref[0, ch.rows, :] + mod_ref[0, 0, 2:3, :] * ch.y
        x1_ref[0, ch.rows, :] = x1
        h2 = _rms_rows(x1, g2_ref[...]) * (1.0 + mod_ref[0, 0, 4:5, :]) + mod_ref[0, 0, 3:4, :]
        h2_ref[0, ch.rows, :] = h2.astype(BF16)
        h_hi, h_lo = _split(h2)
        ch.logits = _nt(w_hi, h_hi) + _nt(w_hi, h_lo) + _nt(w_lo, h_hi)

    def affinity(ch):
        e = jnp.exp(ch.logits - jnp.max(ch.logits, axis=0, keepdims=True))
        aff_ref[0, :, ch.rows] = e / jnp.sum(e, axis=0, keepdims=True)

    chains = [_Chain(rows=slice(i * half, (i + 1) * half)) for i in range(2)]
    for stage in (mean_squares, project, route, affinity):
        for ch in chains:
            stage(ch)


def _merge(ml, hg, at, ogate, xs, modtok, g_ml, g_hg, w_out, g2, router_w, tm, j0, sel, at_j0):
    bsz, n, d = xs.shape
    n_tiles = n // tm
    tok = lambda width, off: pl.BlockSpec((1, tm, width), lambda i, j: (i, j + off, 0))
    full = lambda shape: pl.BlockSpec(shape, lambda i, j: (0,) * len(shape))
    same_b = _head_tables(CHUNK)[1].astype(BF16)
    return pl.pallas_call(
        _merge_kernel,
        grid=(bsz, n_tiles),
        in_specs=[tok(ML_W, j0), tok(HG_W, j0), tok(MLA_HEADS * MLA_DV, j0 - at_j0), tok(_W_GATE, j0), tok(d, 0),
                  pl.BlockSpec((1, 1, 6, d), lambda i, j: (i, sel, 0, 0)),
                  full((1, ML_W)), full((1, HG_W)), full((d, d)), full((1, d)), full((N_EXPERTS, d)),
                  _const_spec(same_b)],
        out_specs=[tok(d, 0), tok(d, 0), pl.BlockSpec((1, N_EXPERTS, tm), lambda i, j: (i, 0, j))],
        out_shape=[jax.ShapeDtypeStruct((bsz, n, d), F32),
                   jax.ShapeDtypeStruct((bsz, n, d), BF16),
                   jax.ShapeDtypeStruct((bsz, N_EXPERTS, n), F32)],
        compiler_params=_params(("arbitrary", "arbitrary")),
        name="merge",
    )(ml, hg, at, ogate, xs, modtok, g_ml[None, :], g_hg[None, :], w_out.astype(BF16), g2[None, :],
      router_w.T, same_b)


def _route_kernel(aff_ref, rank_ref, *, cap, n):
    aff = aff_ref[0]

    def search(i, thr):
        cand = thr | (jnp.int32(1) << (30 - i))
        cnt = jnp.sum(jnp.where(aff >= pltpu.bitcast(cand, F32), 1, 0), axis=-1, keepdims=True)
        return jnp.where(cnt >= cap, cand, thr)

    thr = lax.fori_loop(0, 31, search, jnp.zeros((N_EXPERTS, 1), jnp.int32))
    above = aff >= pltpu.bitcast(thr + 1, F32)
    tied = jnp.logical_and(aff >= pltpu.bitcast(thr, F32), jnp.logical_not(above))
    need = cap - jnp.sum(jnp.where(above, 1, 0), axis=-1, keepdims=True)

    tl = min(n, TOKEN_TILE)
    r = lax.broadcasted_iota(jnp.int32, (tl, tl), 0)
    c = lax.broadcasted_iota(jnp.int32, (tl, tl), 1)
    before = jnp.where(r < c, 1.0, 0.0).astype(BF16)

    def excl_cumsum(mask):
        parts, carry = [], jnp.zeros((N_EXPERTS, 1), F32)
        for j in range(n // tl):
            m = jnp.where(mask[:, j * tl:(j + 1) * tl], 1.0, 0.0)
            parts.append(_mm(m.astype(BF16), before) + carry)
            carry = carry + jnp.sum(m, axis=-1, keepdims=True)
        return jnp.concatenate(parts, axis=-1).astype(jnp.int32)

    keep = jnp.logical_or(above, jnp.logical_and(tied, excl_cumsum(tied) < need))
    rank_ref[0] = jnp.where(keep, excl_cumsum(keep), -1)


def _route(aff_t, cap):
    bsz, e, n = aff_t.shape
    kern = functools.partial(_route_kernel, cap=cap, n=n)
    return pl.pallas_call(
        kern,
        grid=(bsz,),
        in_specs=[pl.BlockSpec((1, e, n), lambda i: (i, 0, 0))],
        out_specs=pl.BlockSpec((1, e, n), lambda i: (i, 0, 0)),
        out_shape=jax.ShapeDtypeStruct((bsz, e, n), jnp.int32),
        compiler_params=_params(("arbitrary",)),
        name="route",
    )(aff_t)


def _pick_and_gate(rank_row, aff_row, cap):
    slot = lax.broadcasted_iota(jnp.int32, (cap, rank_row.shape[1]), 0)
    chosen = rank_row == slot
    gate = jnp.sum(jnp.where(chosen, aff_row, 0.0), axis=-1, keepdims=True)
    return jnp.where(chosen, 1.0, 0.0).astype(BF16), gate


def _ffn_experts(xs, picks, gates, x1_ref, g2_ref, wg_ref, wu_ref, wd_ref, o_ref, cap):
    e = pl.program_id(1)
    hid = (_silu(_mm(xs, wg_ref[0, 0])) * _mm(xs, wu_ref[0, 0])).astype(BF16)
    out = _mm(hid, wd_ref[0, 0])

    @pl.when(e == 0)
    def _():
        o_ref[...] = jnp.zeros_like(o_ref)

    d = out.shape[1]
    for b, (pick, gate) in enumerate(zip(picks, gates)):
        out_b = (out[b * cap:(b + 1) * cap] * gate).astype(BF16)
        for j in range(d // SCATTER_COLS):
            cols = slice(j * SCATTER_COLS, (j + 1) * SCATTER_COLS)
            o_ref[b, :, cols] += _tn(pick, out_b[:, cols])

    @pl.when(e == pl.num_programs(1) - 1)
    def _():
        o_ref[...] = x1_ref[...] + g2_ref[...] * o_ref[...]


def _ffn_kernel(rank_ref, aff_ref, h_ref, x1_ref, g2_ref, wg_ref, wu_ref, wd_ref, o_ref, *, cap, bb):
    picks, gates, xs = [], [], []
    for b in range(bb):
        pick, gate = _pick_and_gate(rank_ref[b, 0], aff_ref[b, 0], cap)
        picks.append(pick)
        gates.append(gate)
        xs.append(_mm(pick, h_ref[b].astype(BF16)).astype(BF16))
    _ffn_experts(jnp.concatenate(xs, axis=0), picks, gates, x1_ref, g2_ref, wg_ref, wu_ref, wd_ref, o_ref, cap)


def _ffn(rank, aff_t, h2, x1, g2mod, wg, wu, wd, layer, cap, bb):
    bsz, n, d = h2.shape
    n_exp = wg.shape[1]
    once = dict(pipeline_mode=pl.Buffered(1))
    per_expert = lambda width: pl.BlockSpec((bb, 1, 1, width), lambda i, e: (i, e, 0, 0))
    common = [pl.BlockSpec((bb, n, d), lambda i, e: (i, 0, 0), **once),
              pl.BlockSpec((bb, 1, d), lambda i, e: (i, 0, 0)),
              pl.BlockSpec((1, 1, d, d), lambda i, e: (layer, e, 0, 0)),
              pl.BlockSpec((1, 1, d, d), lambda i, e: (layer, e, 0, 0)),
              pl.BlockSpec((1, 1, d, d), lambda i, e: (layer, e, 0, 0))]
    out_spec = pl.BlockSpec((bb, n, d), lambda i, e: (i, 0, 0))
    out_shape = jax.ShapeDtypeStruct((bsz, n, d), F32)
    rank4, aff4 = rank.reshape(bsz, n_exp, 1, n), aff_t.reshape(bsz, n_exp, 1, n)
    return pl.pallas_call(
        functools.partial(_ffn_kernel, cap=cap, bb=bb),
        grid=(bsz // bb, n_exp),
        in_specs=[per_expert(n), per_expert(n), pl.BlockSpec((bb, n, d), lambda i, e: (i, 0, 0), **once)] + common,
        out_specs=out_spec, out_shape=out_shape,
        compiler_params=_params(("arbitrary", "arbitrary")),
        name="expert_ffn",
    )(rank4, aff4, h2, x1, g2mod, wg, wu, wd)


def _rope_tables(ctx_len, seq):
    half = MLA_ROPE // 2
    inv = ROPE_THETA ** (-jnp.arange(0, half, 2, dtype=F32) / half)
    rows = seq // GRID_W
    row_pos = jnp.repeat(jnp.arange(rows), GRID_W).astype(F32)
    col_pos = jnp.broadcast_to(jnp.arange(GRID_W), (rows, GRID_W)).reshape(-1).astype(F32)

    def cs(pos):
        ang = pos[:, None] * inv[None, :]
        ang = jnp.concatenate([ang, ang], axis=-1)
        return jnp.cos(ang), jnp.sin(ang)

    cos_r, sin_r = cs(row_pos)
    cos_c, sin_c = cs(col_pos)
    zeros = lambda n: jnp.zeros((seq, n), F32)
    ones = lambda n: jnp.ones((seq, n), F32)
    tail = HEAD_PAD - MLA_DQK
    cos = jnp.concatenate([ones(MLA_NOPE), cos_r, cos_c, ones(tail)], axis=-1)
    sin = jnp.concatenate([zeros(MLA_NOPE), sin_r, sin_c, zeros(tail)], axis=-1)
    ident = lambda a, fill: jnp.concatenate([jnp.full((ctx_len, HEAD_PAD), fill, F32), a], axis=0)
    return ident(cos, 1.0), ident(sin, 0.0)


def kernel(x, c, ctx, c_ctx, ada_w, ada_b, norm1_g, norm2_g, w_in, b_in, ml_conv_w, ml_conv_b, ml_norm_g, hg_lb_logits, hg_norm_g, mla_q_norm_g, mla_w_uq, mla_kv_norm_g, mla_w_ukv, mla_q_qk_g, mla_k_qk_g, w_out, router_w, ex_w_gate, ex_w_up, ex_w_down):
    bsz, seq, d = x.shape
    ctx_len = ctx.shape[1]
    depth = ada_w.shape[0]
    t = ctx_len + seq
    tm = TOKEN_TILE if ctx_len % TOKEN_TILE == 0 else TOKEN_TILE // 2
    assert ctx_len % tm == 0 and seq % tm == 0 and ctx_len % ML_CHUNK == 0 and seq % ML_CHUNK == 0 and seq % GRID_W == 0
    nct = ctx_len // tm

    rows = -(-(bsz + 1) // 8) * 8
    cc = jnp.concatenate([c, c_ctx[None, :], jnp.zeros((rows - bsz - 1, d), F32)], axis=0)
    mod = _modulation(cc, ada_w, ada_b)
    cos, sin = _rope_tables(ctx_len, seq)
    bb_ctx = max(bb for bb in (8, 4, 2, 1) if bsz % bb == 0)

    wg, wu, wd = (w.astype(BF16) for w in (ex_w_gate, ex_w_up, ex_w_down))
    lat = x
    for layer in range(depth):
        need_ctx = layer < depth - 1
        m = mod[layer]
        modtok = jnp.stack([jnp.broadcast_to(m[bsz], (bsz, 6 * d)), m[:bsz]], axis=1).reshape(bsz, 2, 6, d)
        oml, ohqv, ohf, ogate, omla, ogt = _in_proj(ctx, lat, modtok, norm1_g[layer], w_in[layer], b_in[layer], tm, nct)
        gates_t = ogt.reshape(bsz, 16, t // ML_CHUNK, ML_CHUNK).transpose(0, 2, 1, 3)
        ml = _mlstm(oml, omla, gates_t, ml_conv_w[layer], ml_conv_b[layer], ctx_len, tm)
        hg = _hgrn(ohqv, ohf, hg_lb_logits, ctx_len, layer)
        q, k, v = _mla_prep(omla, mla_q_norm_g[layer], mla_kv_norm_g[layer], mla_w_uq[layer], mla_w_ukv[layer],
                            mla_q_qk_g[layer], mla_k_qk_g[layer], cos, sin, tm)
        at_j0 = 0 if need_ctx else nct
        at = _attention(q, k, v, ctx_len, tm, at_j0)
        g2mod = modtok[:, :, 5:6, :]

        def post(xs, j0, sel, bb):
            x1, h2, aff = _merge(ml, hg, at, ogate, xs, modtok, ml_norm_g[layer], hg_norm_g[layer], w_out[layer],
                                 norm2_g[layer], router_w[layer], tm, j0, sel, at_j0)
            cap = EC_CAPACITY * xs.shape[1] // N_EXPERTS
            rank = _route(aff, cap)
            return _ffn(rank, aff, h2, x1, g2mod[:, sel], wg, wu, wd, layer, cap, bb)

        new_lat = post(lat, nct, 1, 1)
        if need_ctx:
            ctx = post(ctx, 0, 0, bb_ctx)
        lat = new_lat
    return lat
```

```python
import functools

import numpy as np
import jax
import jax.numpy as jnp
from jax import lax
from jax.experimental import pallas as pl
from jax.experimental.pallas import tpu as pltpu

F32 = jnp.float32
BF16 = jnp.bfloat16

GRID_W = 64
ML_HEADS = 4
ML_DH = 64
ML_W = 256
HG_HEADS = 4
HG_W = 256
MLA_HEADS = 8
MLA_NOPE = 64
MLA_ROPE = 32
MLA_DQK = 96
MLA_DV = 64
MLA_Q_RANK = 256
MLA_KV_RANK = 128
N_EXPERTS = 16
EC_CAPACITY = 2
CHUNK = 64
ML_CHUNK = 128
ROPE_THETA = 10000.0
EPS = 1e-6
HEAD_PAD = 128
LANES = 128
N_LEVELS = 6
N_MM_LEVELS = 3
SLAB = HG_HEADS * CHUNK
ML_SLAB = ML_HEADS * ML_CHUNK
VMEM_LIMIT = 52 * 1024 * 1024

_O_QK, _O_V, _O_OG, _O_GATES, _O_HQ, _O_HI, _O_HGATE, _O_HF, _O_CQ, _O_CKV, _O_KR, _O_END = (
    0, 512, 768, 1024, 1040, 1296, 1552, 1808, 2320, 2576, 2704, 2736)
_W_ML = 2 * ML_W + ML_W
_W_HQV = 2 * HG_W
_W_HF = 2 * HG_W
_W_GATE = ML_W + HG_W
_W_MLA = MLA_Q_RANK + MLA_KV_RANK + LANES
_C_ML, _C_HQV, _C_HF, _C_GATE, _C_MLA, _C_END = np.cumsum([0, _W_ML, _W_HQV, _W_HF, _W_GATE, _W_MLA]).tolist()
MISC_BLOCK = (MLA_Q_RANK + MLA_KV_RANK) // LANES
MOD_TILE = 1024
TOKEN_TILE = 256
SCATTER_COLS = 256


def _nt(a, b, **kw):
    return lax.dot_general(a, b, (((1,), (1,)), ((), ())), preferred_element_type=F32, **kw)


def _tn(a, b, **kw):
    return lax.dot_general(a, b, (((0,), (0,)), ((), ())), preferred_element_type=F32, **kw)


def _mm(a, b, **kw):
    return jnp.dot(a, b, preferred_element_type=F32, **kw)


def _split(x):
    hi = x.astype(BF16)
    return hi, (x - hi.astype(F32)).astype(BF16)


def _silu(x):
    return x * jax.nn.sigmoid(x)


def _log_sigmoid(x):
    return jnp.minimum(x, 0.0) - jnp.log1p(jnp.exp(-jnp.abs(x)))


def _rms_rows(x, g):
    return x * lax.rsqrt(jnp.mean(x * x, axis=-1, keepdims=True) + EPS) * g


def _params(sem=None):
    return pltpu.CompilerParams(dimension_semantics=sem, vmem_limit_bytes=VMEM_LIMIT)


def _const_spec(a):
    return pl.BlockSpec(a.shape, lambda *_: (0,) * a.ndim)


def _mod_kernel(c_ref, w_ref, b_ref, o_ref):
    s = _silu(c_ref[...]).astype(BF16)
    o_ref[0] = _mm(s, w_ref[0].astype(BF16)) + b_ref[0]


def _modulation(cc, ada_w, ada_b):
    n_layers, d, n6 = ada_w.shape
    rows = cc.shape[0]
    tn = MOD_TILE
    return pl.pallas_call(
        _mod_kernel,
        grid=(n_layers, n6 // tn),
        in_specs=[pl.BlockSpec((rows, d), lambda l, j: (0, 0)),
                  pl.BlockSpec((1, d, tn), lambda l, j: (l, 0, j)),
                  pl.BlockSpec((1, 1, tn), lambda l, j: (l, 0, j))],
        out_specs=pl.BlockSpec((1, rows, tn), lambda l, j: (l, 0, j)),
        out_shape=jax.ShapeDtypeStruct((n_layers, rows, n6), F32),
        compiler_params=_params(("arbitrary", "arbitrary")),
        name="modulation",
    )(cc, ada_w, ada_b.reshape(n_layers, 1, n6))


def _in_kernel(c_ref, l_ref, mod_ref, g_ref, w_ref, b_ref, wgt_ref, bgt_ref,
               oml_ref, ohqv_ref, ohf_ref, ogate_ref, omla_ref, ogt_ref, *, nct):
    x = jnp.where(pl.program_id(1) < nct, c_ref[0], l_ref[0])
    sh = mod_ref[0, 0, 0:1, :]
    sc = mod_ref[0, 0, 1:2, :]
    h = (_rms_rows(x, g_ref[...]) * (1.0 + sc) + sh).astype(BF16)
    proj = lambda lo, hi: _mm(h, w_ref[:, lo:hi]) + b_ref[:, lo:hi]
    oml_ref[0] = proj(_C_ML, _C_HQV).astype(BF16)
    ohqv_ref[0] = proj(_C_HQV, _C_HF).astype(BF16)
    ohf_ref[0] = proj(_C_HF, _C_GATE)
    ogate_ref[0] = proj(_C_GATE, _C_MLA)
    omla_ref[0] = proj(_C_MLA, _C_END)
    ogt_ref[0] = _nt(wgt_ref[...], h) + bgt_ref[...]


def _in_proj(ctx_x, lat_x, modtok, g1, w_in, b_in, tm, nct):
    bsz, seq, d = lat_x.shape
    t = ctx_x.shape[1] + seq
    src, sign = _rotate_half_columns(MLA_ROPE)
    zeros = lambda n: jnp.zeros((d, n), F32)
    w = jnp.concatenate([
        w_in[:, _O_QK:_O_OG], w_in[:, _O_HQ:_O_HGATE], w_in[:, _O_HF:_O_CQ],
        w_in[:, _O_OG:_O_GATES], w_in[:, _O_HGATE:_O_HF], w_in[:, _O_CQ:_O_KR],
        w_in[:, _O_GATES:_O_HQ], zeros(48), w_in[:, _O_KR:_O_END],
        w_in[:, _O_KR:_O_END][:, src] * sign], axis=1).astype(BF16)
    zb = lambda n: jnp.zeros((n,), F32)
    b = jnp.concatenate([
        b_in[_O_QK:_O_OG], b_in[_O_HQ:_O_HGATE], b_in[_O_HF:_O_CQ],
        b_in[_O_OG:_O_GATES], b_in[_O_HGATE:_O_HF], b_in[_O_CQ:_O_KR],
        b_in[_O_GATES:_O_HQ], zb(48), b_in[_O_KR:_O_END], b_in[_O_KR:_O_END][src] * sign])[None, :]
    wgt = w_in[:, _O_GATES:_O_HQ].T.astype(BF16)
    bgt = b_in[_O_GATES:_O_HQ][:, None]
    nw = w.shape[1]
    tok = lambda width: pl.BlockSpec((1, tm, width), lambda i, j: (i, j, 0))
    full = lambda shape: pl.BlockSpec(shape, lambda i, j: (0,) * len(shape))
    return pl.pallas_call(
        functools.partial(_in_kernel, nct=nct),
        grid=(bsz, t // tm),
        in_specs=[pl.BlockSpec((1, tm, d), lambda i, j: (i, jnp.minimum(j, nct - 1), 0)),
                  pl.BlockSpec((1, tm, d), lambda i, j: (i, jnp.maximum(j - nct, 0), 0)),
                  pl.BlockSpec((1, 1, 6, d), lambda i, j: (i, jnp.where(j >= nct, 1, 0), 0, 0)),
                  full((1, d)), full((d, nw)), full((1, nw)), full((16, d)), full((16, 1))],
        out_specs=[tok(_W_ML), tok(_W_HQV), tok(_W_HF), tok(_W_GATE), tok(_W_MLA),
                   pl.BlockSpec((1, 16, tm), lambda i, j: (i, 0, j))],
        out_shape=[jax.ShapeDtypeStruct((bsz, t, _W_ML), BF16),
                   jax.ShapeDtypeStruct((bsz, t, _W_HQV), BF16),
                   jax.ShapeDtypeStruct((bsz, t, _W_HF), F32),
                   jax.ShapeDtypeStruct((bsz, t, _W_GATE), F32),
                   jax.ShapeDtypeStruct((bsz, t, _W_MLA), F32),
                   jax.ShapeDtypeStruct((bsz, 16, t), F32)],
        compiler_params=_params(("arbitrary", "arbitrary")),
        name="in_proj",
    )(ctx_x, lat_x, modtok, g1[None, :], w, b, wgt, bgt)


def _dir1_chunk(i, n_ctx_chunks, n_chunks):
    return jnp.where(i < n_ctx_chunks, n_ctx_chunks - 1 - i, n_chunks - 1 - (i - n_ctx_chunks))


def _head_tables(chunk):
    feat_head = np.arange(ML_W) // ML_DH
    stack = (np.arange(ML_HEADS * chunk)[:, None] // chunk == feat_head[None, :]).astype(np.float32)
    square = (feat_head[:, None] == feat_head[None, :]).astype(np.float32)
    return jnp.asarray(stack, BF16), jnp.asarray(square, F32)


def _stack_heads(x_bf, same_bf):
    return jnp.concatenate([x_bf] * ML_HEADS, axis=0) * same_bf


class _Chain:
    def __init__(self, **kw):
        self.__dict__.update(kw)


def _samples_per_step(bsz):
    return 2 if bsz % 2 == 0 else 1


def _mm_split(a, b):
    hi, lo = _split(a)
    return _mm(hi, b) + _mm(lo, b)


def _mlstm_tables():
    t = np.arange(ML_CHUNK)
    lower = (t[:, None] >= t[None, :]).astype(np.float32)
    tri = np.stack([lower, lower.T])
    neg = np.stack([np.tile(np.where(m > 0, 0.0, -np.inf), (1, ML_HEADS)) for m in (lower, lower.T)]).astype(np.float32)
    spread = np.zeros((2, LANES, ML_W), np.float32)
    spread_s = np.zeros((2, LANES, ML_SLAB), np.float32)
    for d in range(2):
        for h in range(ML_HEADS):
            spread[d, d * 8 + 4 + h, h * ML_DH:(h + 1) * ML_DH] = 1.0
            spread_s[d, d * 8 + 4 + h, h * ML_CHUNK:(h + 1) * ML_CHUNK] = 1.0
    as_bf = lambda a: jnp.asarray(a, BF16)
    return (as_bf(tri), jnp.asarray(neg), as_bf(spread), as_bf(spread.transpose(0, 2, 1)),
            as_bf(spread_s), as_bf(spread_s.transpose(0, 2, 1)))


def _mlstm_kernel(ml_ref, gc_ref, gt_ref, cw_ref, cb_ref, tri_ref, neg_ref, spread_ref, gather_ref, spread_s_ref,
                  gather_s_ref, sameb_ref, samef_ref, o_ref, qk_s, c_s, n_s, m_s, *, t, ctx_len, tm):
    n_chunks = t // ML_CHUNK
    n_ctx_chunks = ctx_len // ML_CHUNK
    n_samples = ml_ref.shape[0]
    cw = cw_ref[...]
    cb = cb_ref[...]
    rid = lax.broadcasted_iota(jnp.int32, (tm, 2 * ML_W), 0)
    for b in range(n_samples):
        for j in range(t // tm):
            r0 = j * tm
            cur = ml_ref[b, r0:r0 + tm, 0:2 * ML_W].astype(F32)
            up = pltpu.roll(cur, 1, 0)
            if r0 in (0, ctx_len):
                up = jnp.where(rid == 0, 0.0, up)
            else:
                up = jnp.where(rid == 0, ml_ref[b, r0 - 16:r0, 0:2 * ML_W].astype(F32)[15:16], up)
            dn = pltpu.roll(cur, tm - 1, 0)
            if r0 + tm in (ctx_len, t):
                dn = jnp.where(rid == tm - 1, 0.0, dn)
            else:
                dn = jnp.where(rid == tm - 1, ml_ref[b, r0 + tm:r0 + tm + 16, 0:2 * ML_W].astype(F32)[0:1], dn)
            y = cw[0:1, :] * up + cw[1:2, :] * cur + cw[2:3, :] * dn + cb
            qk_s[b, r0:r0 + tm, :] = _silu(y)

    o_ref[...] = jnp.zeros_like(o_ref)
    c_s[...] = jnp.zeros_like(c_s)
    n_s[...] = jnp.zeros_like(n_s)
    m_s[...] = jnp.zeros_like(m_s)
    gate_lane = lax.broadcasted_iota(jnp.int32, (ML_CHUNK, LANES), 1) < 16
    tok = lax.broadcasted_iota(jnp.int32, (ML_CHUNK, LANES), 0)

    def load(ch):
        d, b = ch.d, ch.b
        ch.rows = pl.ds(pl.multiple_of(ch.c * ML_CHUNK, ML_CHUNK), ML_CHUNK)
        qk = qk_s[b, ch.rows, :]
        ch.qb = qk[:, 0:ML_W].astype(BF16)
        ch.k = qk[:, ML_W:2 * ML_W] * (ML_DH ** -0.5)
        ch.vb = ml_ref[b, ch.rows, 2 * ML_W:3 * ML_W]
        ch.gc = jnp.where(gate_lane, gc_ref[b, ch.rows, :], 0.0)
        ch.gt = gt_ref[b, ch.c]
        hi, lo = _split(_log_sigmoid(ch.gc))
        ch.b_c = _mm(tri_ref[d], hi) + _mm(tri_ref[d], lo)
        ch.b_r = _mm_split(_log_sigmoid(ch.gt), tri_ref[1 - d])
        ch.s_raw = _nt(ch.qb, _stack_heads(ch.k.astype(BF16), sameb_ref[...]))
        ch.qn = _mm((qk[:, 0:ML_W] * n_s[b, d][0:1, :]).astype(BF16), gather_ref[d])
        ch.c_mat = c_s[b, d]
        ch.qc = _mm(ch.qb, ch.c_mat.astype(BF16))

    def stabilise(ch):
        d = ch.d
        ch.u = pltpu.roll(ch.gc, 4, 1) - ch.b_c
        cmax = ch.u
        for step in [1 << i for i in range(ML_CHUNK.bit_length() - 1)]:
            if d == 0:
                moved = jnp.where(tok >= step, pltpu.roll(cmax, step, 0), -jnp.inf)
            else:
                moved = jnp.where(tok < ML_CHUNK - step, pltpu.roll(cmax, ML_CHUNK - step, 0), -jnp.inf)
            cmax = jnp.maximum(cmax, moved)
        ch.m = m_s[ch.b, d][0:1, :]
        inter = ch.b_c + ch.m
        ch.m_t = jnp.maximum(inter, ch.b_c + cmax)
        ch.w_st = jnp.exp(inter - ch.m_t)
        last = ML_CHUNK - 1 if d == 0 else 0
        ch.bend = ch.b_c[last:last + 1, :]
        ch.m_new = jnp.maximum(ch.bend + ch.m, ch.bend + cmax[last:last + 1, :])
        fcols = [d * 8 + 4 + h for h in range(ML_HEADS)]
        icols = [d * 8 + h for h in range(ML_HEADS)]
        ch.u_row = jnp.concatenate([ch.gt[i:i + 1, :] - ch.b_r[f:f + 1, :] for i, f in zip(icols, fcols)], axis=1)
        ch.bm_wide = _mm_split(ch.b_c - ch.m_t, spread_s_ref[d])
        ch.wk_wide = _mm(jnp.exp(ch.bend + ch.u - ch.m_new).astype(BF16), spread_ref[d])
        ch.w_old = jnp.broadcast_to(jnp.exp(ch.bend + ch.m - ch.m_new), (8, LANES))
        ch.wold_wide = _mm_split(ch.w_old, spread_ref[d])

    def weigh(ch):
        d = ch.d
        s = ch.s_raw * jnp.exp(ch.bm_wide + ch.u_row + neg_ref[d])
        sb = s.astype(BF16)
        ch.intra = _mm(sb, _stack_heads(ch.vb, sameb_ref[...]))
        ch.rowsum = _mm(sb, gather_s_ref[d])
        kw = ch.k * ch.wk_wide
        ch.dc = _tn(kw.astype(BF16), ch.vb)
        ch.dn = jnp.sum(kw, axis=0, keepdims=True)

    def normalise(ch):
        d = ch.d
        den = ch.rowsum + ch.w_st * ch.qn
        r = 1.0 / jnp.maximum(jnp.abs(den), jnp.exp(-ch.m_t))
        ch.r_wide = _mm(r.astype(BF16), spread_ref[d])
        ch.wr_wide = _mm((ch.w_st * r).astype(BF16), spread_ref[d])

    def store(ch):
        b, d = ch.b, ch.d
        o_ref[b, ch.rows, :] += ch.intra * ch.r_wide + ch.qc * ch.wr_wide
        w_old = ch.wold_wide[0:1, :]
        c_s[b, d] = w_old * ch.c_mat + ch.dc * samef_ref[...]
        n_s[b, d] = jnp.broadcast_to(w_old * n_s[b, d][0:1, :] + ch.dn, (8, ML_W))
        m_s[b, d] = jnp.broadcast_to(ch.m_new, (8, LANES))

    def body(i, carry):
        chunk = (i, _dir1_chunk(i, n_ctx_chunks, n_chunks))
        chains = [_Chain(b=b, d=d, c=chunk[d]) for b in range(n_samples) for d in range(2)]
        for stage in (load, stabilise, weigh, normalise, store):
            for ch in chains:
                stage(ch)
        return carry

    lax.fori_loop(0, n_chunks, body, 0)


def _mlstm(oml, omla, gates_t, conv_w, conv_b, ctx_len, tm):
    bsz, t, _ = oml.shape
    n_chunks = t // ML_CHUNK
    kern = functools.partial(_mlstm_kernel, t=t, ctx_len=ctx_len, tm=tm)
    consts = _mlstm_tables() + _head_tables(ML_CHUNK)
    sps = _samples_per_step(bsz)
    return pl.pallas_call(
        kern,
        grid=(bsz // sps,),
        in_specs=[pl.BlockSpec((sps, t, _W_ML), lambda i: (i, 0, 0)),
                  pl.BlockSpec((sps, t, LANES), lambda i: (i, 0, MISC_BLOCK)),
                  pl.BlockSpec((sps, n_chunks, 16, ML_CHUNK), lambda i: (i, 0, 0, 0)),
                  pl.BlockSpec((3, 2 * ML_W), lambda i: (0, 0)),
                  pl.BlockSpec((1, 2 * ML_W), lambda i: (0, 0))] + [_const_spec(a) for a in consts],
        out_specs=pl.BlockSpec((sps, t, ML_W), lambda i: (i, 0, 0)),
        out_shape=jax.ShapeDtypeStruct((bsz, t, ML_W), F32),
        scratch_shapes=[pltpu.VMEM((sps, t, 2 * ML_W), F32),
                        pltpu.VMEM((sps, 2, ML_W, ML_W), F32),
                        pltpu.VMEM((sps, 2, 8, ML_W), F32),
                        pltpu.VMEM((sps, 2, 8, LANES), F32)],
        compiler_params=_params(("arbitrary",)),
        name="mlstm",
    )(oml, omla, gates_t, conv_w, conv_b[None, :], *consts)


def _hgrn_tables():
    sums = np.zeros((2, (N_MM_LEVELS + 1) * CHUNK, CHUNK), np.float32)
    pairs = np.zeros((2, N_LEVELS + 1, CHUNK, CHUNK), np.float32)
    sign = np.zeros((2, N_LEVELS - N_MM_LEVELS, CHUNK, 1), np.float32)
    for d in range(2):
        p = (lambda a: a) if d == 0 else (lambda a: CHUNK - 1 - a)
        for t in range(CHUNK):
            pairs[d, 0, p(t), p(t)] = 1.0
            for s in range(t + 1):
                sums[d, p(t), p(s)] = 1.0
        for l in range(N_LEVELS):
            w = 1 << l
            for t in range(CHUNK):
                ref = t - (t % (2 * w)) + w - 1
                if l < N_MM_LEVELS:
                    lo, hi = (ref + 1, t) if t > ref else (t + 1, ref)
                    for s in range(lo, hi + 1):
                        sums[d, (l + 1) * CHUNK + p(t), p(s)] = 1.0
                else:
                    sign[d, l - N_MM_LEVELS, p(t), 0] = 1.0 if t > ref else -1.0
                if t > ref:
                    for s in range(ref - w + 1, ref + 1):
                        pairs[d, l + 1, p(t), p(s)] = 1.0
    pairs = np.tile(pairs, (1, 1, 1, HG_HEADS))
    sign = np.broadcast_to(sign, sign.shape[:3] + (HG_W,))
    return jnp.asarray(sums, BF16), jnp.asarray(sign), jnp.asarray(pairs)


def _hgrn_ref_rows(cum, d, l):
    w = 1 << l
    parts = []
    for base in range(0, CHUNK, 2 * w):
        r = base + w - 1 if d == 0 else base + w
        parts.append(jnp.broadcast_to(cum[r:r + 1, :], (2 * w, cum.shape[1])))
    return parts[0] if len(parts) == 1 else jnp.concatenate(parts, axis=0)


def _hgrn_kernel(hqv_ref, hf_ref, lbl_ref, sum_ref, sign_ref, pair_ref, sameb_ref, samef_ref, o_ref, st_s,
                 *, t, ctx_len, layer):
    n_chunks = t // CHUNK
    n_ctx_chunks = ctx_len // CHUNK
    w_all = HG_W

    if layer > 0:
        logits = lbl_ref[...]
        n_layers = logits.shape[0]
        mx = logits[0]
        for l in range(1, n_layers):
            mx = jnp.maximum(mx, logits[l])
        ex = [jnp.exp(logits[l] - mx) for l in range(n_layers)]
        tot = ex[0]
        for l in range(1, n_layers):
            tot = tot + ex[l]
        low_all = ex[1] / tot
        for l in range(2, layer + 1):
            low_all = low_all + ex[l] / tot
        log_low = jnp.log(low_all)
        log_1m_low = jnp.log1p(-low_all)

    o_ref[...] = jnp.zeros_like(o_ref)
    st_s[...] = jnp.zeros_like(st_s)

    def gates(ch):
        b, d = ch.b, ch.d
        ch.rows = pl.ds(pl.multiple_of(ch.c * CHUNK, CHUNK), CHUNK)
        ch.q = _silu(hqv_ref[b, ch.rows, 0:w_all].astype(F32))
        ch.vb = hqv_ref[b, ch.rows, w_all:2 * w_all]
        pre = hf_ref[b, ch.rows, d * w_all:(d + 1) * w_all]
        ls = _log_sigmoid(pre)
        if layer == 0:
            log_f = ls
            ch.key = jnp.exp(ls - pre)
        else:
            lo_, hi_ = log_low[d:d + 1, :], log_1m_low[d:d + 1, :] + ls
            log_f = jnp.maximum(lo_, hi_) + jnp.log1p(jnp.exp(-jnp.abs(lo_ - hi_)))
            ch.key = (1.0 - low_all[d:d + 1, :]) * jnp.exp(ls - pre)
        hi, lo = _split(log_f)
        ch.xb = _mm(sum_ref[d], hi) + _mm(sum_ref[d], lo)
        ch.st = st_s[b, d]
        ch.amat = _nt(ch.q.astype(BF16), _stack_heads(ch.key.astype(BF16), sameb_ref[...])) * pair_ref[d, 0]

    def levels(ch):
        d = ch.d
        cum = ch.xb[0:CHUNK]
        for l in range(N_LEVELS):
            if l < N_MM_LEVELS:
                expo = ch.xb[(l + 1) * CHUNK:(l + 2) * CHUNK]
            else:
                expo = (cum - _hgrn_ref_rows(cum, d, l)) * sign_ref[d, l - N_MM_LEVELS]
            fac = jnp.exp(expo)
            qt = (ch.q * fac).astype(BF16)
            kt = (ch.key * fac).astype(BF16)
            ch.amat = ch.amat + _nt(qt, _stack_heads(kt, sameb_ref[...])) * pair_ref[d, l + 1]
        last = CHUNK - 1 if d == 0 else 0
        bend = cum[last:last + 1, :]
        ch.inter = _nt((ch.q * jnp.exp(cum)).astype(BF16), ch.st.astype(BF16))
        ch.dst = _tn(ch.vb, (ch.key * jnp.exp(bend - cum)).astype(BF16))
        ch.decay = jnp.exp(bend)

    def readout(ch):
        ch.intra = _mm(ch.amat.astype(BF16), _stack_heads(ch.vb, sameb_ref[...]))

    def store(ch):
        o_ref[ch.b, ch.rows, :] += ch.intra + ch.inter
        st_s[ch.b, ch.d] = ch.decay * ch.st + ch.dst * samef_ref[...]

    def body(i, carry):
        chunk = (i, _dir1_chunk(i, n_ctx_chunks, n_chunks))
        chains = [_Chain(b=b, d=d, c=chunk[d]) for b in range(hf_ref.shape[0]) for d in range(2)]
        for stage in (gates, levels, readout, store):
            for ch in chains:
                stage(ch)
        return carry

    lax.fori_loop(0, n_chunks, body, 0)


def _hgrn(ohqv, ohf, lb_logits, ctx_len, layer):
    bsz, t, _ = ohf.shape
    kern = functools.partial(_hgrn_kernel, t=t, ctx_len=ctx_len, layer=layer)
    consts = (lb_logits,) + _hgrn_tables() + _head_tables(CHUNK)
    sps = _samples_per_step(bsz)
    return pl.pallas_call(
        kern,
        grid=(bsz // sps,),
        in_specs=[pl.BlockSpec((sps, t, 2 * HG_W), lambda i: (i, 0, 0)),
                  pl.BlockSpec((sps, t, 2 * HG_W), lambda i: (i, 0, 0))] + [_const_spec(a) for a in consts],
        out_specs=pl.BlockSpec((sps, t, HG_W), lambda i: (i, 0, 0)),
        out_shape=jax.ShapeDtypeStruct((bsz, t, HG_W), F32),
        scratch_shapes=[pltpu.VMEM((sps, 2, HG_W, HG_W), F32)],
        compiler_params=_params(("arbitrary",)),
        name="hgrn",
    )(ohqv, ohf, *consts)


def _mla_prep_kernel(x_ref, gq_ref, gkv_ref, wq_ref, wqr_ref, wk_ref, wv_ref, gqq_ref, gqr_ref, gkk_ref, gkr_ref,
                     cos_ref, sin_ref, q_ref, k_ref, v_ref):
    x = x_ref[0]
    cq = _rms_rows(x[:, 0:MLA_Q_RANK], gq_ref[...]).astype(BF16)
    ckv = _rms_rows(x[:, MLA_Q_RANK:MLA_Q_RANK + MLA_KV_RANK], gkv_ref[...]).astype(BF16)
    misc = x[:, MLA_Q_RANK + MLA_KV_RANK:]
    lane = lax.broadcasted_iota(jnp.int32, misc.shape, 1)
    rope_lane = (lane >= MLA_NOPE) & (lane < MLA_DQK)
    k_rope = jnp.where(rope_lane, misc, 0.0)
    k_rope_rot = jnp.where(rope_lane, pltpu.roll(misc, HEAD_PAD - MLA_ROPE, 1), 0.0)
    q_raw = _mm(cq, wq_ref[...])
    q_rot = _mm(cq, wqr_ref[...])
    k_raw = _mm(ckv, wk_ref[...])
    v_all = _mm(ckv, wv_ref[...]).astype(BF16)
    cos, sin = cos_ref[...], sin_ref[...]

    def norm_rope(xh, xr, g, gr):
        ms = jnp.sum(xh * xh, axis=-1, keepdims=True) * (1.0 / MLA_DQK)
        return (xh * (g * cos) + xr * (gr * sin)) * lax.rsqrt(ms + EPS)

    for h in range(MLA_HEADS):
        sl = slice(h * HEAD_PAD, (h + 1) * HEAD_PAD)
        q = norm_rope(q_raw[:, sl], q_rot[:, sl], gqq_ref[...], gqr_ref[...])
        q_ref[0, h] = (q * (MLA_DQK ** -0.5)).astype(BF16)
        k_ref[0, h] = norm_rope(k_raw[:, sl] + k_rope, k_rope_rot, gkk_ref[...], gkr_ref[...]).astype(BF16)
        v_ref[0, h] = v_all[:, h * MLA_DV:(h + 1) * MLA_DV]


def _rotate_half_columns(n):
    j = np.arange(n)
    first = j % 16 < 8
    return np.where(first, j + 8, j - 8), np.where(first, -1.0, 1.0).astype(np.float32)


def _mla_prep(omla, gq, gkv, w_uq, w_ukv, g_qq, g_kk, cos, sin, tm):
    bsz, t, _ = omla.shape
    hw = MLA_HEADS * HEAD_PAD
    src, sign = _rotate_half_columns(MLA_ROPE)
    src_pad = np.arange(HEAD_PAD)
    src_pad[MLA_NOPE:MLA_DQK] = MLA_NOPE + src
    sign_pad = np.zeros((HEAD_PAD,), np.float32)
    sign_pad[MLA_NOPE:MLA_DQK] = sign
    wq = w_uq.reshape(MLA_Q_RANK, MLA_HEADS, MLA_DQK)
    wq = jnp.pad(wq, ((0, 0), (0, 0), (0, HEAD_PAD - MLA_DQK)))
    wq_rot = (wq[:, :, src_pad] * sign_pad).reshape(MLA_Q_RANK, hw).astype(BF16)
    wq = wq.reshape(MLA_Q_RANK, hw).astype(BF16)
    wkv = w_ukv.reshape(MLA_KV_RANK, MLA_HEADS, MLA_NOPE + MLA_DV)
    wk = jnp.pad(wkv[:, :, :MLA_NOPE], ((0, 0), (0, 0), (0, HEAD_PAD - MLA_NOPE))).reshape(MLA_KV_RANK, hw).astype(BF16)
    wv = wkv[:, :, MLA_NOPE:].reshape(MLA_KV_RANK, MLA_HEADS * MLA_DV).astype(BF16)
    padg = lambda g: jnp.pad(g, (0, HEAD_PAD - MLA_DQK))
    gains = [padg(g_qq)[None, :], (padg(g_qq)[src_pad] * jnp.abs(sign_pad))[None, :],
             padg(g_kk)[None, :], (padg(g_kk)[src_pad] * jnp.abs(sign_pad))[None, :]]
    full = lambda shape: pl.BlockSpec(shape, lambda i, j: (0,) * len(shape))
    rope_spec = pl.BlockSpec((tm, HEAD_PAD), lambda i, j: (j, 0))
    return pl.pallas_call(
        _mla_prep_kernel,
        grid=(bsz, t // tm),
        in_specs=[pl.BlockSpec((1, tm, _W_MLA), lambda i, j: (i, j, 0)),
                  full((1, MLA_Q_RANK)), full((1, MLA_KV_RANK)),
                  full((MLA_Q_RANK, hw)), full((MLA_Q_RANK, hw)), full((MLA_KV_RANK, hw)),
                  full((MLA_KV_RANK, MLA_HEADS * MLA_DV))] + [full((1, HEAD_PAD))] * 4 + [rope_spec, rope_spec],
        out_specs=[pl.BlockSpec((1, MLA_HEADS, tm, HEAD_PAD), lambda i, j: (i, 0, j, 0)),
                   pl.BlockSpec((1, MLA_HEADS, tm, HEAD_PAD), lambda i, j: (i, 0, j, 0)),
                   pl.BlockSpec((1, MLA_HEADS, tm, MLA_DV), lambda i, j: (i, 0, j, 0))],
        out_shape=[jax.ShapeDtypeStruct((bsz, MLA_HEADS, t, HEAD_PAD), BF16),
                   jax.ShapeDtypeStruct((bsz, MLA_HEADS, t, HEAD_PAD), BF16),
                   jax.ShapeDtypeStruct((bsz, MLA_HEADS, t, MLA_DV), BF16)],
        compiler_params=_params(("arbitrary", "arbitrary")),
        name="mla_prep",
    )(omla, gq[None, :], gkv[None, :], wq, wq_rot, wk, wv, *gains, cos, sin)


def _attn_kernel(q_ref, k_ref, v_ref, o_ref, *, ctx_len, nct, j0):
    is_ctx = pl.program_id(1) + j0 < nct

    def attend(n_keys):
        scores = lambda h: _nt(q_ref[0, h], k_ref[0, h, 0:n_keys, :])
        s_next = scores(0)
        for h in range(MLA_HEADS):
            s, s_next = s_next, (scores(h + 1) if h + 1 < MLA_HEADS else None)
            p = jnp.exp(s - jnp.max(s, axis=-1, keepdims=True))
            o = _mm(p.astype(BF16), v_ref[0, h, 0:n_keys, :]) / jnp.sum(p, axis=-1, keepdims=True)
            o_ref[0, :, h * MLA_DV:(h + 1) * MLA_DV] = o

    @pl.when(is_ctx)
    def _():
        attend(ctx_len)

    @pl.when(jnp.logical_not(is_ctx))
    def _():
        attend(k_ref.shape[2])


def _attention(q, k, v, ctx_len, tm, j0):
    bsz, _, t, _ = q.shape
    nct = ctx_len // tm
    nq = t // tm - j0
    kern = functools.partial(_attn_kernel, ctx_len=ctx_len, nct=nct, j0=j0)
    return pl.pallas_call(
        kern,
        grid=(bsz, nq),
        in_specs=[pl.BlockSpec((1, MLA_HEADS, tm, HEAD_PAD), lambda i, j: (i, 0, j + j0, 0)),
                  pl.BlockSpec((1, MLA_HEADS, t, HEAD_PAD), lambda i, j: (i, 0, 0, 0)),
                  pl.BlockSpec((1, MLA_HEADS, t, MLA_DV), lambda i, j: (i, 0, 0, 0))],
        out_specs=pl.BlockSpec((1, tm, MLA_HEADS * MLA_DV), lambda i, j: (i, j, 0)),
        out_shape=jax.ShapeDtypeStruct((bsz, nq * tm, MLA_HEADS * MLA_DV), F32),
        compiler_params=_params(("arbitrary", "arbitrary")),
        name="attention",
    )(q, k, v)


def _merge_kernel(ml_ref, hg_ref, at_ref, gate_ref, x_ref, mod_ref, gml_ref, ghg_ref, wo_ref,
                  g2_ref, rwt_ref, same_ref, x1_ref, h2_ref, aff_ref):
    head_mean = same_ref[...] * (1.0 / ML_DH)
    w_hi, w_lo = _split(rwt_ref[...])
    half = x_ref.shape[1] // 2

    def mean_squares(ch):
        ch.ml, ch.hg = ml_ref[0, ch.rows, :], hg_ref[0, ch.rows, :]
        hi, lo = _split(ch.ml * ch.ml)
        ch.ms_ml = _mm(hi, head_mean) + _mm(lo, head_mean)
        hi, lo = _split(ch.hg * ch.hg)
        ch.ms_hg = _mm(hi, head_mean) + _mm(lo, head_mean)

    def project(ch):
        gates = gate_ref[0, ch.rows, :]
        y_ml = ch.ml * lax.rsqrt(ch.ms_ml + EPS) * gml_ref[...] * jax.nn.sigmoid(gates[:, 0:ML_W])
        y_hg = ch.hg * lax.rsqrt(ch.ms_hg + EPS) * ghg_ref[...] * _silu(gates[:, ML_W:ML_W + HG_W])
        ch.y = (_mm(y_ml.astype(BF16), wo_ref[0:ML_W, :])
                + _mm(y_hg.astype(BF16), wo_ref[ML_W:ML_W + HG_W, :])
                + _mm(at_ref[0, ch.rows, :].astype(BF16), wo_ref[ML_W + HG_W:, :]))

    def route(ch):
        x1 = x_ref[0, ch.rows, :] + mod_ref[0, 0, 2:3, :] * ch.y
        x1_ref[0, ch.rows, :] = x1
        h2 = _rms_rows(x1, g2_ref[...]) * (1.0 + mod_ref[0, 0, 4:5, :]) + mod_ref[0, 0, 3:4, :]
        h2_ref[0, ch.rows, :] = h2.astype(BF16)
        h_hi, h_lo = _split(h2)
        ch.logits = _nt(w_hi, h_hi) + _nt(w_hi, h_lo) + _nt(w_lo, h_hi)

    def affinity(ch):
        e = jnp.exp(ch.logits - jnp.max(ch.logits, axis=0, keepdims=True))
        aff_ref[0, :, ch.rows] = e / jnp.sum(e, axis=0, keepdims=True)

    chains = [_Chain(rows=slice(i * half, (i + 1) * half)) for i in range(2)]
    for stage in (mean_squares, project, route, affinity):
        for ch in chains:
            stage(ch)


def _merge(ml, hg, at, ogate, xs, modtok, g_ml, g_hg, w_out, g2, router_w, tm, j0, sel, at_j0):
    bsz, n, d = xs.shape
    n_tiles = n // tm
    tok = lambda width, off: pl.BlockSpec((1, tm, width), lambda i, j: (i, j + off, 0))
    full = lambda shape: pl.BlockSpec(shape, lambda i, j: (0,) * len(shape))
    same_b = _head_tables(CHUNK)[1].astype(BF16)
    return pl.pallas_call(
        _merge_kernel,
        grid=(bsz, n_tiles),
        in_specs=[tok(ML_W, j0), tok(HG_W, j0), tok(MLA_HEADS * MLA_DV, j0 - at_j0), tok(_W_GATE, j0), tok(d, 0),
                  pl.BlockSpec((1, 1, 6, d), lambda i, j: (i, sel, 0, 0)),
                  full((1, ML_W)), full((1, HG_W)), full((d, d)), full((1, d)), full((N_EXPERTS, d)),
                  _const_spec(same_b)],
        out_specs=[tok(d, 0), tok(d, 0), pl.BlockSpec((1, N_EXPERTS, tm), lambda i, j: (i, 0, j))],
        out_shape=[jax.ShapeDtypeStruct((bsz, n, d), F32),
                   jax.ShapeDtypeStruct((bsz, n, d), BF16),
                   jax.ShapeDtypeStruct((bsz, N_EXPERTS, n), F32)],
        compiler_params=_params(("arbitrary", "arbitrary")),
        name="merge",
    )(ml, hg, at, ogate, xs, modtok, g_ml[None, :], g_hg[None, :], w_out.astype(BF16), g2[None, :],
      router_w.T, same_b)


def _route_kernel(aff_ref, rank_ref, *, cap, n):
    aff = aff_ref[0]

    def search(i, thr):
        cand = thr | (jnp.int32(1) << (30 - i))
        cnt = jnp.sum(jnp.where(aff >= pltpu.bitcast(cand, F32), 1, 0), axis=-1, keepdims=True)
        return jnp.where(cnt >= cap, cand, thr)

    thr = lax.fori_loop(0, 31, search, jnp.zeros((N_EXPERTS, 1), jnp.int32))
    above = aff >= pltpu.bitcast(thr + 1, F32)
    tied = jnp.logical_and(aff >= pltpu.bitcast(thr, F32), jnp.logical_not(above))
    need = cap - jnp.sum(jnp.where(above, 1, 0), axis=-1, keepdims=True)

    tl = min(n, TOKEN_TILE)
    r = lax.broadcasted_iota(jnp.int32, (tl, tl), 0)
    c = lax.broadcasted_iota(jnp.int32, (tl, tl), 1)
    before = jnp.where(r < c, 1.0, 0.0).astype(BF16)

    def excl_cumsum(mask):
        parts, carry = [], jnp.zeros((N_EXPERTS, 1), F32)
        for j in range(n // tl):
            m = jnp.where(mask[:, j * tl:(j + 1) * tl], 1.0, 0.0)
            parts.append(_mm(m.astype(BF16), before) + carry)
            carry = carry + jnp.sum(m, axis=-1, keepdims=True)
        return jnp.concatenate(parts, axis=-1).astype(jnp.int32)

    keep = jnp.logical_or(above, jnp.logical_and(tied, excl_cumsum(tied) < need))
    rank_ref[0] = jnp.where(keep, excl_cumsum(keep), -1)


def _route(aff_t, cap):
    bsz, e, n = aff_t.shape
    kern = functools.partial(_route_kernel, cap=cap, n=n)
    return pl.pallas_call(
        kern,
        grid=(bsz,),
        in_specs=[pl.BlockSpec((1, e, n), lambda i: (i, 0, 0))],
        out_specs=pl.BlockSpec((1, e, n), lambda i: (i, 0, 0)),
        out_shape=jax.ShapeDtypeStruct((bsz, e, n), jnp.int32),
        compiler_params=_params(("arbitrary",)),
        name="route",
    )(aff_t)


def _pick_and_gate(rank_row, aff_row, cap):
    slot = lax.broadcasted_iota(jnp.int32, (cap, rank_row.shape[1]), 0)
    chosen = rank_row == slot
    gate = jnp.sum(jnp.where(chosen, aff_row, 0.0), axis=-1, keepdims=True)
    return jnp.where(chosen, 1.0, 0.0).astype(BF16), gate


def _ffn_experts(xs, picks, gates, x1_ref, g2_ref, wg_ref, wu_ref, wd_ref, o_ref, cap):
    e = pl.program_id(1)
    hid = (_silu(_mm(xs, wg_ref[0, 0])) * _mm(xs, wu_ref[0, 0])).astype(BF16)
    out = _mm(hid, wd_ref[0, 0])

    @pl.when(e == 0)
    def _():
        o_ref[...] = jnp.zeros_like(o_ref)

    d = out.shape[1]
    for b, (pick, gate) in enumerate(zip(picks, gates)):
        out_b = (out[b * cap:(b + 1) * cap] * gate).astype(BF16)
        for j in range(d // SCATTER_COLS):
            cols = slice(j * SCATTER_COLS, (j + 1) * SCATTER_COLS)
            o_ref[b, :, cols] += _tn(pick, out_b[:, cols])

    @pl.when(e == pl.num_programs(1) - 1)
    def _():
        o_ref[...] = x1_ref[...] + g2_ref[...] * o_ref[...]


def _ffn_kernel(rank_ref, aff_ref, h_ref, x1_ref, g2_ref, wg_ref, wu_ref, wd_ref, o_ref, *, cap, bb):
    picks, gates, xs = [], [], []
    for b in range(bb):
        pick, gate = _pick_and_gate(rank_ref[b, 0], aff_ref[b, 0], cap)
        picks.append(pick)
        gates.append(gate)
        xs.append(_mm(pick, h_ref[b].astype(BF16)).astype(BF16))
    _ffn_experts(jnp.concatenate(xs, axis=0), picks, gates, x1_ref, g2_ref, wg_ref, wu_ref, wd_ref, o_ref, cap)


def _ffn(rank, aff_t, h2, x1, g2mod, wg, wu, wd, layer, cap, bb):
    bsz, n, d = h2.shape
    n_exp = wg.shape[1]
    once = dict(pipeline_mode=pl.Buffered(1))
    per_expert = lambda width: pl.BlockSpec((bb, 1, 1, width), lambda i, e: (i, e, 0, 0))
    common = [pl.BlockSpec((bb, n, d), lambda i, e: (i, 0, 0), **once),
              pl.BlockSpec((bb, 1, d), lambda i, e: (i, 0, 0)),
              pl.BlockSpec((1, 1, d, d), lambda i, e: (layer, e, 0, 0)),
              pl.BlockSpec((1, 1, d, d), lambda i, e: (layer, e, 0, 0)),
              pl.BlockSpec((1, 1, d, d), lambda i, e: (layer, e, 0, 0))]
    out_spec = pl.BlockSpec((bb, n, d), lambda i, e: (i, 0, 0))
    out_shape = jax.ShapeDtypeStruct((bsz, n, d), F32)
    rank4, aff4 = rank.reshape(bsz, n_exp, 1, n), aff_t.reshape(bsz, n_exp, 1, n)
    return pl.pallas_call(
        functools.partial(_ffn_kernel, cap=cap, bb=bb),
        grid=(bsz // bb, n_exp),
        in_specs=[per_expert(n), per_expert(n), pl.BlockSpec((bb, n, d), lambda i, e: (i, 0, 0), **once)] + common,
        out_specs=out_spec, out_shape=out_shape,
        compiler_params=_params(("arbitrary", "arbitrary")),
        name="expert_ffn",
    )(rank4, aff4, h2, x1, g2mod, wg, wu, wd)


def _rope_tables(ctx_len, seq):
    half = MLA_ROPE // 2
    inv = ROPE_THETA ** (-jnp.arange(0, half, 2, dtype=F32) / half)
    rows = seq // GRID_W
    row_pos = jnp.repeat(jnp.arange(rows), GRID_W).astype(F32)
    col_pos = jnp.broadcast_to(jnp.arange(GRID_W), (rows, GRID_W)).reshape(-1).astype(F32)

    def cs(pos):
        ang = pos[:, None] * inv[None, :]
        ang = jnp.concatenate([ang, ang], axis=-1)
        return jnp.cos(ang), jnp.sin(ang)

    cos_r, sin_r = cs(row_pos)
    cos_c, sin_c = cs(col_pos)
    zeros = lambda n: jnp.zeros((seq, n), F32)
    ones = lambda n: jnp.ones((seq, n), F32)
    tail = HEAD_PAD - MLA_DQK
    cos = jnp.concatenate([ones(MLA_NOPE), cos_r, cos_c, ones(tail)], axis=-1)
    sin = jnp.concatenate([zeros(MLA_NOPE), sin_r, sin_c, zeros(tail)], axis=-1)
    ident = lambda a, fill: jnp.concatenate([jnp.full((ctx_len, HEAD_PAD), fill, F32), a], axis=0)
    return ident(cos, 1.0), ident(sin, 0.0)


def kernel(x, c, ctx, c_ctx, ada_w, ada_b, norm1_g, norm2_g, w_in, b_in, ml_conv_w, ml_conv_b, ml_norm_g, hg_lb_logits, hg_norm_g, mla_q_norm_g, mla_w_uq, mla_kv_norm_g, mla_w_ukv, mla_q_qk_g, mla_k_qk_g, w_out, router_w, ex_w_gate, ex_w_up, ex_w_down):
    bsz, seq, d = x.shape
    ctx_len = ctx.shape[1]
    depth = ada_w.shape[0]
    t = ctx_len + seq
    tm = TOKEN_TILE if ctx_len % TOKEN_TILE == 0 else TOKEN_TILE // 2
    assert ctx_len % tm == 0 and seq % tm == 0 and ctx_len % ML_CHUNK == 0 and seq % ML_CHUNK == 0 and seq % GRID_W == 0
    nct = ctx_len // tm

    rows = -(-(bsz + 1) // 8) * 8
    cc = jnp.concatenate([c, c_ctx[None, :], jnp.zeros((rows - bsz - 1, d), F32)], axis=0)
    mod = _modulation(cc, ada_w, ada_b)
    cos, sin = _rope_tables(ctx_len, seq)
    bb_ctx = max(bb for bb in (8, 4, 2, 1) if bsz % bb == 0)

    wg, wu, wd = (w.astype(BF16) for w in (ex_w_gate, ex_w_up, ex_w_down))
    lat = x
    for layer in range(depth):
        need_ctx = layer < depth - 1
        m = mod[layer]
        modtok = jnp.stack([jnp.broadcast_to(m[bsz], (bsz, 6 * d)), m[:bsz]], axis=1).reshape(bsz, 2, 6, d)
        oml, ohqv, ohf, ogate, omla, ogt = _in_proj(ctx, lat, modtok, norm1_g[layer], w_in[layer], b_in[layer], tm, nct)
        gates_t = ogt.reshape(bsz, 16, t // ML_CHUNK, ML_CHUNK).transpose(0, 2, 1, 3)
        ml = _mlstm(oml, omla, gates_t, ml_conv_w[layer], ml_conv_b[layer], ctx_len, tm)
        hg = _hgrn(ohqv, ohf, hg_lb_logits, ctx_len, layer)
        q, k, v = _mla_prep(omla, mla_q_norm_g[layer], mla_kv_norm_g[layer], mla_w_uq[layer], mla_w_ukv[layer],
                            mla_q_qk_g[layer], mla_k_qk_g[layer], cos, sin, tm)
        at_j0 = 0 if need_ctx else nct
        at = _attention(q, k, v, ctx_len, tm, at_j0)
        g2mod = modtok[:, :, 5:6, :]

        def post(xs, j0, sel, bb):
            x1, h2, aff = _merge(ml, hg, at, ogate, xs, modtok, ml_norm_g[layer], hg_norm_g[layer], w_out[layer],
                                 norm2_g[layer], router_w[layer], tm, j0, sel, at_j0)
            cap = EC_CAPACITY * xs.shape[1] // N_EXPERTS
            rank = _route(aff, cap)
            return _ffn(rank, aff, h2, x1, g2mod[:, sel], wg, wu, wd, layer, cap, bb)

        new_lat = post(lat, nct, 1, 1)
        if need_ctx:
            ctx = post(ctx, 0, 0, bb_ctx)
        lat = new_lat
    return lat
```

```python
import functools

import numpy as np
import jax
import jax.numpy as jnp
from jax import lax
from jax.experimental import pallas as pl
from jax.experimental.pallas import tpu as pltpu

F32 = jnp.float32
BF16 = jnp.bfloat16

GRID_W = 64
ML_HEADS = 4
ML_DH = 64
ML_W = 256
HG_HEADS = 4
HG_W = 256
MLA_HEADS = 8
MLA_NOPE = 64
MLA_ROPE = 32
MLA_DQK = 96
MLA_DV = 64
MLA_Q_RANK = 256
MLA_KV_RANK = 128
N_EXPERTS = 16
EC_CAPACITY = 2
CHUNK = 64
ML_CHUNK = 128
ROPE_THETA = 10000.0
EPS = 1e-6
HEAD_PAD = 128
LANES = 128
N_LEVELS = 6
N_MM_LEVELS = 3
SLAB = HG_HEADS * CHUNK
ML_SLAB = ML_HEADS * ML_CHUNK
VMEM_LIMIT = 52 * 1024 * 1024

_O_QK, _O_V, _O_OG, _O_GATES, _O_HQ, _O_HI, _O_HGATE, _O_HF, _O_CQ, _O_CKV, _O_KR, _O_END = (
    0, 512, 768, 1024, 1040, 1296, 1552, 1808, 2320, 2576, 2704, 2736)
_W_ML = 2 * ML_W + ML_W
_W_HQV = 2 * HG_W
_W_HF = 2 * HG_W
_W_GATE = ML_W + HG_W
_W_MLA = MLA_Q_RANK + MLA_KV_RANK + LANES
_C_ML, _C_HQV, _C_HF, _C_GATE, _C_MLA, _C_END = np.cumsum([0, _W_ML, _W_HQV, _W_HF, _W_GATE, _W_MLA]).tolist()
MISC_BLOCK = (MLA_Q_RANK + MLA_KV_RANK) // LANES
MOD_TILE = 1024
TOKEN_TILE = 256
SCATTER_COLS = 256


def _nt(a, b, **kw):
    return lax.dot_general(a, b, (((1,), (1,)), ((), ())), preferred_element_type=F32, **kw)


def _tn(a, b, **kw):
    return lax.dot_general(a, b, (((0,), (0,)), ((), ())), preferred_element_type=F32, **kw)


def _mm(a, b, **kw):
    return jnp.dot(a, b, preferred_element_type=F32, **kw)


def _split(x):
    hi = x.astype(BF16)
    return hi, (x - hi.astype(F32)).astype(BF16)


def _silu(x):
    return x * jax.nn.sigmoid(x)


def _log_sigmoid(x):
    return jnp.minimum(x, 0.0) - jnp.log1p(jnp.exp(-jnp.abs(x)))


def _rms_rows(x, g):
    return x * lax.rsqrt(jnp.mean(x * x, axis=-1, keepdims=True) + EPS) * g


def _params(sem=None):
    return pltpu.CompilerParams(dimension_semantics=sem, vmem_limit_bytes=VMEM_LIMIT)


def _const_spec(a):
    return pl.BlockSpec(a.shape, lambda *_: (0,) * a.ndim)


def _mod_kernel(c_ref, w_ref, b_ref, o_ref):
    s = _silu(c_ref[...]).astype(BF16)
    o_ref[0] = _mm(s, w_ref[0].astype(BF16)) + b_ref[0]


def _modulation(cc, ada_w, ada_b):
    n_layers, d, n6 = ada_w.shape
    rows = cc.shape[0]
    tn = MOD_TILE
    return pl.pallas_call(
        _mod_kernel,
        grid=(n_layers, n6 // tn),
        in_specs=[pl.BlockSpec((rows, d), lambda l, j: (0, 0)),
                  pl.BlockSpec((1, d, tn), lambda l, j: (l, 0, j)),
                  pl.BlockSpec((1, 1, tn), lambda l, j: (l, 0, j))],
        out_specs=pl.BlockSpec((1, rows, tn), lambda l, j: (l, 0, j)),
        out_shape=jax.ShapeDtypeStruct((n_layers, rows, n6), F32),
        compiler_params=_params(("arbitrary", "arbitrary")),
        name="modulation",
    )(cc, ada_w, ada_b.reshape(n_layers, 1, n6))


def _in_kernel(c_ref, l_ref, mod_ref, g_ref, w_ref, b_ref, wgt_ref, bgt_ref, *rest, nct):
    mla_refs, (oml_ref, ohqv_ref, ohf_ref, ogate_ref, omisc_ref, ogt_ref, q_ref, k_ref, v_ref) = rest[:12], rest[12:]
    x = jnp.where(pl.program_id(1) < nct, c_ref[0], l_ref[0])
    sh = mod_ref[0, 0, 0:1, :]
    sc = mod_ref[0, 0, 1:2, :]
    h = (_rms_rows(x, g_ref[...]) * (1.0 + sc) + sh).astype(BF16)
    proj = lambda lo, hi: _mm(h, w_ref[:, lo:hi]) + b_ref[:, lo:hi]
    mla = proj(_C_MLA, _C_END)
    omisc_ref[0] = mla[:, MISC_BLOCK * LANES:]
    _mla_project(mla, *mla_refs, q_ref, k_ref, v_ref)
    oml_ref[0] = proj(_C_ML, _C_HQV).astype(BF16)
    ohqv_ref[0] = proj(_C_HQV, _C_HF).astype(BF16)
    ohf_ref[0] = proj(_C_HF, _C_GATE)
    ogate_ref[0] = proj(_C_GATE, _C_MLA)
    ogt_ref[0] = _nt(wgt_ref[...], h) + bgt_ref[...]


def _in_proj(ctx_x, lat_x, modtok, g1, w_in, b_in, mla_operands, mla_specs, tm, nct):
    bsz, seq, d = lat_x.shape
    t = ctx_x.shape[1] + seq
    src, sign = _rotate_half_columns(MLA_ROPE)
    zeros = lambda n: jnp.zeros((d, n), F32)
    w = jnp.concatenate([
        w_in[:, _O_QK:_O_OG], w_in[:, _O_HQ:_O_HGATE], w_in[:, _O_HF:_O_CQ],
        w_in[:, _O_OG:_O_GATES], w_in[:, _O_HGATE:_O_HF], w_in[:, _O_CQ:_O_KR],
        w_in[:, _O_GATES:_O_HQ], zeros(48), w_in[:, _O_KR:_O_END],
        w_in[:, _O_KR:_O_END][:, src] * sign], axis=1).astype(BF16)
    zb = lambda n: jnp.zeros((n,), F32)
    b = jnp.concatenate([
        b_in[_O_QK:_O_OG], b_in[_O_HQ:_O_HGATE], b_in[_O_HF:_O_CQ],
        b_in[_O_OG:_O_GATES], b_in[_O_HGATE:_O_HF], b_in[_O_CQ:_O_KR],
        b_in[_O_GATES:_O_HQ], zb(48), b_in[_O_KR:_O_END], b_in[_O_KR:_O_END][src] * sign])[None, :]
    wgt = w_in[:, _O_GATES:_O_HQ].T.astype(BF16)
    bgt = b_in[_O_GATES:_O_HQ][:, None]
    nw = w.shape[1]
    tok = lambda width: pl.BlockSpec((1, tm, width), lambda i, j: (i, j, 0))
    full = lambda shape: pl.BlockSpec(shape, lambda i, j: (0,) * len(shape))
    return pl.pallas_call(
        functools.partial(_in_kernel, nct=nct),
        grid=(bsz, t // tm),
        in_specs=[pl.BlockSpec((1, tm, d), lambda i, j: (i, jnp.minimum(j, nct - 1), 0)),
                  pl.BlockSpec((1, tm, d), lambda i, j: (i, jnp.maximum(j - nct, 0), 0)),
                  pl.BlockSpec((1, 1, 6, d), lambda i, j: (i, jnp.where(j >= nct, 1, 0), 0, 0)),
                  full((1, d)), full((d, nw)), full((1, nw)), full((16, d)), full((16, 1))] + mla_specs,
        out_specs=[tok(_W_ML), tok(_W_HQV), tok(_W_HF), tok(_W_GATE), tok(LANES),
                   pl.BlockSpec((1, 16, tm), lambda i, j: (i, 0, j)),
                   pl.BlockSpec((1, MLA_HEADS, tm, HEAD_PAD), lambda i, j: (i, 0, j, 0)),
                   pl.BlockSpec((1, MLA_HEADS, tm, HEAD_PAD), lambda i, j: (i, 0, j, 0)),
                   pl.BlockSpec((1, MLA_HEADS, tm, MLA_DV), lambda i, j: (i, 0, j, 0))],
        out_shape=[jax.ShapeDtypeStruct((bsz, t, _W_ML), BF16),
                   jax.ShapeDtypeStruct((bsz, t, _W_HQV), BF16),
                   jax.ShapeDtypeStruct((bsz, t, _W_HF), F32),
                   jax.ShapeDtypeStruct((bsz, t, _W_GATE), F32),
                   jax.ShapeDtypeStruct((bsz, t, LANES), F32),
                   jax.ShapeDtypeStruct((bsz, 16, t), F32),
                   jax.ShapeDtypeStruct((bsz, MLA_HEADS, t, HEAD_PAD), BF16),
                   jax.ShapeDtypeStruct((bsz, MLA_HEADS, t, HEAD_PAD), BF16),
                   jax.ShapeDtypeStruct((bsz, MLA_HEADS, t, MLA_DV), BF16)],
        compiler_params=_params(("arbitrary", "arbitrary")),
        name="in_proj",
    )(ctx_x, lat_x, modtok, g1[None, :], w, b, wgt, bgt, *mla_operands)


def _dir1_chunk(i, n_ctx_chunks, n_chunks):
    return jnp.where(i < n_ctx_chunks, n_ctx_chunks - 1 - i, n_chunks - 1 - (i - n_ctx_chunks))


def _head_tables(chunk):
    feat_head = np.arange(ML_W) // ML_DH
    stack = (np.arange(ML_HEADS * chunk)[:, None] // chunk == feat_head[None, :]).astype(np.float32)
    square = (feat_head[:, None] == feat_head[None, :]).astype(np.float32)
    return jnp.asarray(stack, BF16), jnp.asarray(square, F32)


def _stack_heads(x_bf, same_bf):
    return jnp.concatenate([x_bf] * ML_HEADS, axis=0) * same_bf


class _Chain:
    def __init__(self, **kw):
        self.__dict__.update(kw)


def _samples_per_step(bsz):
    return 2 if bsz % 2 == 0 else 1


def _mm_split(a, b):
    hi, lo = _split(a)
    return _mm(hi, b) + _mm(lo, b)


def _mlstm_tables():
    t = np.arange(ML_CHUNK)
    lower = (t[:, None] >= t[None, :]).astype(np.float32)
    tri = np.stack([lower, lower.T])
    neg = np.stack([np.tile(np.where(m > 0, 0.0, -np.inf), (1, ML_HEADS)) for m in (lower, lower.T)]).astype(np.float32)
    spread = np.zeros((2, LANES, ML_W), np.float32)
    spread_s = np.zeros((2, LANES, ML_SLAB), np.float32)
    for d in range(2):
        for h in range(ML_HEADS):
            spread[d, d * 8 + 4 + h, h * ML_DH:(h + 1) * ML_DH] = 1.0
            spread_s[d, d * 8 + 4 + h, h * ML_CHUNK:(h + 1) * ML_CHUNK] = 1.0
    as_bf = lambda a: jnp.asarray(a, BF16)
    return (as_bf(tri), jnp.asarray(neg), as_bf(spread), as_bf(spread.transpose(0, 2, 1)),
            as_bf(spread_s), as_bf(spread_s.transpose(0, 2, 1)))


def _mlstm_kernel(ml_ref, gc_ref, gt_ref, cw_ref, cb_ref, tri_ref, neg_ref, spread_ref, gather_ref, spread_s_ref,
                  gather_s_ref, sameb_ref, samef_ref, o_ref, qk_s, c_s, n_s, m_s, *, t, ctx_len, tm):
    n_chunks = t // ML_CHUNK
    n_ctx_chunks = ctx_len // ML_CHUNK
    n_samples = ml_ref.shape[0]
    cw = cw_ref[...]
    cb = cb_ref[...]
    rid = lax.broadcasted_iota(jnp.int32, (tm, 2 * ML_W), 0)
    for b in range(n_samples):
        for j in range(t // tm):
            r0 = j * tm
            cur = ml_ref[b, r0:r0 + tm, 0:2 * ML_W].astype(F32)
            up = pltpu.roll(cur, 1, 0)
            if r0 in (0, ctx_len):
                up = jnp.where(rid == 0, 0.0, up)
            else:
                up = jnp.where(rid == 0, ml_ref[b, r0 - 16:r0, 0:2 * ML_W].astype(F32)[15:16], up)
            dn = pltpu.roll(cur, tm - 1, 0)
            if r0 + tm in (ctx_len, t):
                dn = jnp.where(rid == tm - 1, 0.0, dn)
            else:
                dn = jnp.where(rid == tm - 1, ml_ref[b, r0 + tm:r0 + tm + 16, 0:2 * ML_W].astype(F32)[0:1], dn)
            y = cw[0:1, :] * up + cw[1:2, :] * cur + cw[2:3, :] * dn + cb
            qk_s[b, r0:r0 + tm, :] = _silu(y)

    o_ref[...] = jnp.zeros_like(o_ref)
    c_s[...] = jnp.zeros_like(c_s)
    n_s[...] = jnp.zeros_like(n_s)
    m_s[...] = jnp.zeros_like(m_s)
    gate_lane = lax.broadcasted_iota(jnp.int32, (ML_CHUNK, LANES), 1) < 16
    tok = lax.broadcasted_iota(jnp.int32, (ML_CHUNK, LANES), 0)

    def load(ch):
        d, b = ch.d, ch.b
        ch.rows = pl.ds(pl.multiple_of(ch.c * ML_CHUNK, ML_CHUNK), ML_CHUNK)
        qk = qk_s[b, ch.rows, :]
        ch.qb = qk[:, 0:ML_W].astype(BF16)
        ch.k = qk[:, ML_W:2 * ML_W] * (ML_DH ** -0.5)
        ch.vb = ml_ref[b, ch.rows, 2 * ML_W:3 * ML_W]
        ch.gc = jnp.where(gate_lane, gc_ref[b, ch.rows, :], 0.0)
        ch.gt = gt_ref[b, ch.c]
        hi, lo = _split(_log_sigmoid(ch.gc))
        ch.b_c = _mm(tri_ref[d], hi) + _mm(tri_ref[d], lo)
        ch.b_r = _mm_split(_log_sigmoid(ch.gt), tri_ref[1 - d])
        ch.s_raw = _nt(ch.qb, _stack_heads(ch.k.astype(BF16), sameb_ref[...]))
        ch.qn = _mm((qk[:, 0:ML_W] * n_s[b, d][0:1, :]).astype(BF16), gather_ref[d])
        ch.c_mat = c_s[b, d]
        ch.qc = _mm(ch.qb, ch.c_mat.astype(BF16))

    def stabilise(ch):
        d = ch.d
        ch.u = pltpu.roll(ch.gc, 4, 1) - ch.b_c
        cmax = ch.u
        for step in [1 << i for i in range(ML_CHUNK.bit_length() - 1)]:
            if d == 0:
                moved = jnp.where(tok >= step, pltpu.roll(cmax, step, 0), -jnp.inf)
            else:
                moved = jnp.where(tok < ML_CHUNK - step, pltpu.roll(cmax, ML_CHUNK - step, 0), -jnp.inf)
            cmax = jnp.maximum(cmax, moved)
        ch.m = m_s[ch.b, d][0:1, :]
        inter = ch.b_c + ch.m
        ch.m_t = jnp.maximum(inter, ch.b_c + cmax)
        ch.w_st = jnp.exp(inter - ch.m_t)
        last = ML_CHUNK - 1 if d == 0 else 0
        ch.bend = ch.b_c[last:last + 1, :]
        ch.m_new = jnp.maximum(ch.bend + ch.m, ch.bend + cmax[last:last + 1, :])
        fcols = [d * 8 + 4 + h for h in range(ML_HEADS)]
        icols = [d * 8 + h for h in range(ML_HEADS)]
        ch.u_row = jnp.concatenate([ch.gt[i:i + 1, :] - ch.b_r[f:f + 1, :] for i, f in zip(icols, fcols)], axis=1)
        ch.bm_wide = _mm_split(ch.b_c - ch.m_t, spread_s_ref[d])
        ch.wk_wide = _mm(jnp.exp(ch.bend + ch.u - ch.m_new).astype(BF16), spread_ref[d])
        ch.w_old = jnp.broadcast_to(jnp.exp(ch.bend + ch.m - ch.m_new), (8, LANES))
        ch.wold_wide = _mm_split(ch.w_old, spread_ref[d])

    def weigh(ch):
        d = ch.d
        s = ch.s_raw * jnp.exp(ch.bm_wide + ch.u_row + neg_ref[d])
        sb = s.astype(BF16)
        ch.intra = _mm(sb, _stack_heads(ch.vb, sameb_ref[...]))
        ch.rowsum = _mm(sb, gather_s_ref[d])
        kw = ch.k * ch.wk_wide
        ch.dc = _tn(kw.astype(BF16), ch.vb)
        ch.dn = jnp.sum(kw, axis=0, keepdims=True)

    def normalise(ch):
        d = ch.d
        den = ch.rowsum + ch.w_st * ch.qn
        r = 1.0 / jnp.maximum(jnp.abs(den), jnp.exp(-ch.m_t))
        ch.r_wide = _mm(r.astype(BF16), spread_ref[d])
        ch.wr_wide = _mm((ch.w_st * r).astype(BF16), spread_ref[d])

    def store(ch):
        b, d = ch.b, ch.d
        o_ref[b, ch.rows, :] += ch.intra * ch.r_wide + ch.qc * ch.wr_wide
        w_old = ch.wold_wide[0:1, :]
        c_s[b, d] = w_old * ch.c_mat + ch.dc * samef_ref[...]
        n_s[b, d] = jnp.broadcast_to(w_old * n_s[b, d][0:1, :] + ch.dn, (8, ML_W))
        m_s[b, d] = jnp.broadcast_to(ch.m_new, (8, LANES))

    def body(i, carry):
        chunk = (i, _dir1_chunk(i, n_ctx_chunks, n_chunks))
        chains = [_Chain(b=b, d=d, c=chunk[d]) for b in range(n_samples) for d in range(2)]
        for stage in (load, stabilise, weigh, normalise, store):
            for ch in chains:
                stage(ch)
        return carry

    lax.fori_loop(0, n_chunks, body, 0)


def _mlstm(oml, omisc, gates_t, conv_w, conv_b, ctx_len, tm):
    bsz, t, _ = oml.shape
    n_chunks = t // ML_CHUNK
    kern = functools.partial(_mlstm_kernel, t=t, ctx_len=ctx_len, tm=tm)
    consts = _mlstm_tables() + _head_tables(ML_CHUNK)
    sps = _samples_per_step(bsz)
    return pl.pallas_call(
        kern,
        grid=(bsz // sps,),
        in_specs=[pl.BlockSpec((sps, t, _W_ML), lambda i: (i, 0, 0)),
                  pl.BlockSpec((sps, t, LANES), lambda i: (i, 0, 0)),
                  pl.BlockSpec((sps, n_chunks, 16, ML_CHUNK), lambda i: (i, 0, 0, 0)),
                  pl.BlockSpec((3, 2 * ML_W), lambda i: (0, 0)),
                  pl.BlockSpec((1, 2 * ML_W), lambda i: (0, 0))] + [_const_spec(a) for a in consts],
        out_specs=pl.BlockSpec((sps, t, ML_W), lambda i: (i, 0, 0)),
        out_shape=jax.ShapeDtypeStruct((bsz, t, ML_W), F32),
        scratch_shapes=[pltpu.VMEM((sps, t, 2 * ML_W), F32),
                        pltpu.VMEM((sps, 2, ML_W, ML_W), F32),
                        pltpu.VMEM((sps, 2, 8, ML_W), F32),
                        pltpu.VMEM((sps, 2, 8, LANES), F32)],
        compiler_params=_params(("arbitrary",)),
        name="mlstm",
    )(oml, omisc, gates_t, conv_w, conv_b[None, :], *consts)


def _hgrn_tables():
    sums = np.zeros((2, (N_MM_LEVELS + 1) * CHUNK, CHUNK), np.float32)
    pairs = np.zeros((2, N_LEVELS + 1, CHUNK, CHUNK), np.float32)
    sign = np.zeros((2, N_LEVELS - N_MM_LEVELS, CHUNK, 1), np.float32)
    for d in range(2):
        p = (lambda a: a) if d == 0 else (lambda a: CHUNK - 1 - a)
        for t in range(CHUNK):
            pairs[d, 0, p(t), p(t)] = 1.0
            for s in range(t + 1):
                sums[d, p(t), p(s)] = 1.0
        for l in range(N_LEVELS):
            w = 1 << l
            for t in range(CHUNK):
                ref = t - (t % (2 * w)) + w - 1
                if l < N_MM_LEVELS:
                    lo, hi = (ref + 1, t) if t > ref else (t + 1, ref)
                    for s in range(lo, hi + 1):
                        sums[d, (l + 1) * CHUNK + p(t), p(s)] = 1.0
                else:
                    sign[d, l - N_MM_LEVELS, p(t), 0] = 1.0 if t > ref else -1.0
                if t > ref:
                    for s in range(ref - w + 1, ref + 1):
                        pairs[d, l + 1, p(t), p(s)] = 1.0
    pairs = np.tile(pairs, (1, 1, 1, HG_HEADS))
    sign = np.broadcast_to(sign, sign.shape[:3] + (HG_W,))
    return jnp.asarray(sums, BF16), jnp.asarray(sign), jnp.asarray(pairs)


def _hgrn_ref_rows(cum, d, l):
    w = 1 << l
    parts = []
    for base in range(0, CHUNK, 2 * w):
        r = base + w - 1 if d == 0 else base + w
        parts.append(jnp.broadcast_to(cum[r:r + 1, :], (2 * w, cum.shape[1])))
    return parts[0] if len(parts) == 1 else jnp.concatenate(parts, axis=0)


def _hgrn_kernel(hqv_ref, hf_ref, lbl_ref, sum_ref, sign_ref, pair_ref, sameb_ref, samef_ref, o_ref, st_s,
                 *, t, ctx_len, layer):
    n_chunks = t // CHUNK
    n_ctx_chunks = ctx_len // CHUNK
    w_all = HG_W

    if layer > 0:
        logits = lbl_ref[...]
        n_layers = logits.shape[0]
        mx = logits[0]
        for l in range(1, n_layers):
            mx = jnp.maximum(mx, logits[l])
        ex = [jnp.exp(logits[l] - mx) for l in range(n_layers)]
        tot = ex[0]
        for l in range(1, n_layers):
            tot = tot + ex[l]
        low_all = ex[1] / tot
        for l in range(2, layer + 1):
            low_all = low_all + ex[l] / tot
        log_low = jnp.log(low_all)
        log_1m_low = jnp.log1p(-low_all)

    o_ref[...] = jnp.zeros_like(o_ref)
    st_s[...] = jnp.zeros_like(st_s)

    def gates(ch):
        b, d = ch.b, ch.d
        ch.rows = pl.ds(pl.multiple_of(ch.c * CHUNK, CHUNK), CHUNK)
        ch.q = _silu(hqv_ref[b, ch.rows, 0:w_all].astype(F32))
        ch.vb = hqv_ref[b, ch.rows, w_all:2 * w_all]
        pre = hf_ref[b, ch.rows, d * w_all:(d + 1) * w_all]
        ls = _log_sigmoid(pre)
        if layer == 0:
            log_f = ls
            ch.key = jnp.exp(ls - pre)
        else:
            lo_, hi_ = log_low[d:d + 1, :], log_1m_low[d:d + 1, :] + ls
            log_f = jnp.maximum(lo_, hi_) + jnp.log1p(jnp.exp(-jnp.abs(lo_ - hi_)))
            ch.key = (1.0 - low_all[d:d + 1, :]) * jnp.exp(ls - pre)
        hi, lo = _split(log_f)
        ch.xb = _mm(sum_ref[d], hi) + _mm(sum_ref[d], lo)
        ch.st = st_s[b, d]
        ch.amat = _nt(ch.q.astype(BF16), _stack_heads(ch.key.astype(BF16), sameb_ref[...])) * pair_ref[d, 0]

    def levels(ch):
        d = ch.d
        cum = ch.xb[0:CHUNK]
        for l in range(N_LEVELS):
            if l < N_MM_LEVELS:
                expo = ch.xb[(l + 1) * CHUNK:(l + 2) * CHUNK]
            else:
                expo = (cum - _hgrn_ref_rows(cum, d, l)) * sign_ref[d, l - N_MM_LEVELS]
            fac = jnp.exp(expo)
            qt = (ch.q * fac).astype(BF16)
            kt = (ch.key * fac).astype(BF16)
            ch.amat = ch.amat + _nt(qt, _stack_heads(kt, sameb_ref[...])) * pair_ref[d, l + 1]
        last = CHUNK - 1 if d == 0 else 0
        bend = cum[last:last + 1, :]
        ch.inter = _nt((ch.q * jnp.exp(cum)).astype(BF16), ch.st.astype(BF16))
        ch.dst = _tn(ch.vb, (ch.key * jnp.exp(bend - cum)).astype(BF16))
        ch.decay = jnp.exp(bend)

    def readout(ch):
        ch.intra = _mm(ch.amat.astype(BF16), _stack_heads(ch.vb, sameb_ref[...]))

    def store(ch):
        o_ref[ch.b, ch.rows, :] += ch.intra + ch.inter
        st_s[ch.b, ch.d] = ch.decay * ch.st + ch.dst * samef_ref[...]

    def body(i, carry):
        chunk = (i, _dir1_chunk(i, n_ctx_chunks, n_chunks))
        chains = [_Chain(b=b, d=d, c=chunk[d]) for b in range(hf_ref.shape[0]) for d in range(2)]
        for stage in (gates, levels, readout, store):
            for ch in chains:
                stage(ch)
        return carry

    lax.fori_loop(0, n_chunks, body, 0)


def _hgrn(ohqv, ohf, lb_logits, ctx_len, layer):
    bsz, t, _ = ohf.shape
    kern = functools.partial(_hgrn_kernel, t=t, ctx_len=ctx_len, layer=layer)
    consts = (lb_logits,) + _hgrn_tables() + _head_tables(CHUNK)
    sps = _samples_per_step(bsz)
    return pl.pallas_call(
        kern,
        grid=(bsz // sps,),
        in_specs=[pl.BlockSpec((sps, t, 2 * HG_W), lambda i: (i, 0, 0)),
                  pl.BlockSpec((sps, t, 2 * HG_W), lambda i: (i, 0, 0))] + [_const_spec(a) for a in consts],
        out_specs=pl.BlockSpec((sps, t, HG_W), lambda i: (i, 0, 0)),
        out_shape=jax.ShapeDtypeStruct((bsz, t, HG_W), F32),
        scratch_shapes=[pltpu.VMEM((sps, 2, HG_W, HG_W), F32)],
        compiler_params=_params(("arbitrary",)),
        name="hgrn",
    )(ohqv, ohf, *consts)


def _mla_project(x, gq_ref, gkv_ref, wq_ref, wqr_ref, wk_ref, wv_ref, gqq_ref, gqr_ref, gkk_ref, gkr_ref,
                 cos_ref, sin_ref, q_ref, k_ref, v_ref):
    cq = _rms_rows(x[:, 0:MLA_Q_RANK], gq_ref[...]).astype(BF16)
    ckv = _rms_rows(x[:, MLA_Q_RANK:MLA_Q_RANK + MLA_KV_RANK], gkv_ref[...]).astype(BF16)
    misc = x[:, MLA_Q_RANK + MLA_KV_RANK:]
    lane = lax.broadcasted_iota(jnp.int32, misc.shape, 1)
    rope_lane = (lane >= MLA_NOPE) & (lane < MLA_DQK)
    k_rope = jnp.where(rope_lane, misc, 0.0)
    k_rope_rot = jnp.where(rope_lane, pltpu.roll(misc, HEAD_PAD - MLA_ROPE, 1), 0.0)
    q_raw = _mm(cq, wq_ref[...])
    q_rot = _mm(cq, wqr_ref[...])
    k_raw = _mm(ckv, wk_ref[...])
    v_all = _mm(ckv, wv_ref[...]).astype(BF16)
    cos, sin = cos_ref[...], sin_ref[...]

    def norm_rope(xh, xr, g, gr):
        ms = jnp.sum(xh * xh, axis=-1, keepdims=True) * (1.0 / MLA_DQK)
        return (xh * (g * cos) + xr * (gr * sin)) * lax.rsqrt(ms + EPS)

    for h in range(MLA_HEADS):
        sl = slice(h * HEAD_PAD, (h + 1) * HEAD_PAD)
        q = norm_rope(q_raw[:, sl], q_rot[:, sl], gqq_ref[...], gqr_ref[...])
        q_ref[0, h] = (q * (MLA_DQK ** -0.5)).astype(BF16)
        k_ref[0, h] = norm_rope(k_raw[:, sl] + k_rope, k_rope_rot, gkk_ref[...], gkr_ref[...]).astype(BF16)
        v_ref[0, h] = v_all[:, h * MLA_DV:(h + 1) * MLA_DV]


def _rotate_half_columns(n):
    j = np.arange(n)
    first = j % 16 < 8
    return np.where(first, j + 8, j - 8), np.where(first, -1.0, 1.0).astype(np.float32)


def _mla_operands(gq, gkv, w_uq, w_ukv, g_qq, g_kk, cos, sin, tm):
    hw = MLA_HEADS * HEAD_PAD
    src, sign = _rotate_half_columns(MLA_ROPE)
    src_pad = np.arange(HEAD_PAD)
    src_pad[MLA_NOPE:MLA_DQK] = MLA_NOPE + src
    sign_pad = np.zeros((HEAD_PAD,), np.float32)
    sign_pad[MLA_NOPE:MLA_DQK] = sign
    wq = w_uq.reshape(MLA_Q_RANK, MLA_HEADS, MLA_DQK)
    wq = jnp.pad(wq, ((0, 0), (0, 0), (0, HEAD_PAD - MLA_DQK)))
    wq_rot = (wq[:, :, src_pad] * sign_pad).reshape(MLA_Q_RANK, hw).astype(BF16)
    wq = wq.reshape(MLA_Q_RANK, hw).astype(BF16)
    wkv = w_ukv.reshape(MLA_KV_RANK, MLA_HEADS, MLA_NOPE + MLA_DV)
    wk = jnp.pad(wkv[:, :, :MLA_NOPE], ((0, 0), (0, 0), (0, HEAD_PAD - MLA_NOPE))).reshape(MLA_KV_RANK, hw).astype(BF16)
    wv = wkv[:, :, MLA_NOPE:].reshape(MLA_KV_RANK, MLA_HEADS * MLA_DV).astype(BF16)
    padg = lambda g: jnp.pad(g, (0, HEAD_PAD - MLA_DQK))
    gains = [padg(g_qq)[None, :], (padg(g_qq)[src_pad] * jnp.abs(sign_pad))[None, :],
             padg(g_kk)[None, :], (padg(g_kk)[src_pad] * jnp.abs(sign_pad))[None, :]]
    operands = [gq[None, :], gkv[None, :], wq, wq_rot, wk, wv] + gains + [cos, sin]
    rope_spec = pl.BlockSpec((tm, HEAD_PAD), lambda i, j: (j, 0))
    specs = [pl.BlockSpec(a.shape, lambda i, j: (0, 0)) for a in operands[:-2]] + [rope_spec, rope_spec]
    return operands, specs


def _attn_kernel(q_ref, k_ref, v_ref, o_ref, *, ctx_len, nct, j0):
    is_ctx = pl.program_id(1) + j0 < nct

    def attend(n_keys):
        scores = lambda h: _nt(q_ref[0, h], k_ref[0, h, 0:n_keys, :])
        s_next = scores(0)
        for h in range(MLA_HEADS):
            s, s_next = s_next, (scores(h + 1) if h + 1 < MLA_HEADS else None)
            p = jnp.exp(s - jnp.max(s, axis=-1, keepdims=True))
            o = _mm(p.astype(BF16), v_ref[0, h, 0:n_keys, :]) / jnp.sum(p, axis=-1, keepdims=True)
            o_ref[0, :, h * MLA_DV:(h + 1) * MLA_DV] = o

    @pl.when(is_ctx)
    def _():
        attend(ctx_len)

    @pl.when(jnp.logical_not(is_ctx))
    def _():
        attend(k_ref.shape[2])


def _attention(q, k, v, ctx_len, tm, j0):
    bsz, _, t, _ = q.shape
    nct = ctx_len // tm
    nq = t // tm - j0
    kern = functools.partial(_attn_kernel, ctx_len=ctx_len, nct=nct, j0=j0)
    return pl.pallas_call(
        kern,
        grid=(bsz, nq),
        in_specs=[pl.BlockSpec((1, MLA_HEADS, tm, HEAD_PAD), lambda i, j: (i, 0, j + j0, 0)),
                  pl.BlockSpec((1, MLA_HEADS, t, HEAD_PAD), lambda i, j: (i, 0, 0, 0)),
                  pl.BlockSpec((1, MLA_HEADS, t, MLA_DV), lambda i, j: (i, 0, 0, 0))],
        out_specs=pl.BlockSpec((1, tm, MLA_HEADS * MLA_DV), lambda i, j: (i, j, 0)),
        out_shape=jax.ShapeDtypeStruct((bsz, nq * tm, MLA_HEADS * MLA_DV), F32),
        compiler_params=_params(("arbitrary", "arbitrary")),
        name="attention",
    )(q, k, v)


def _merge_kernel(ml_ref, hg_ref, at_ref, gate_ref, x_ref, mod_ref, gml_ref, ghg_ref, wo_ref,
                  g2_ref, rwt_ref, same_ref, x1_ref, h2_ref, aff_ref):
    head_mean = same_ref[...] * (1.0 / ML_DH)
    w_hi, w_lo = _split(rwt_ref[...])
    half = x_ref.shape[1] // 2

    def mean_squares(ch):
        ch.ml, ch.hg = ml_ref[0, ch.rows, :], hg_ref[0, ch.rows, :]
        hi, lo = _split(ch.ml * ch.ml)
        ch.ms_ml = _mm(hi, head_mean) + _mm(lo, head_mean)
        hi, lo = _split(ch.hg * ch.hg)
        ch.ms_hg = _mm(hi, head_mean) + _mm(lo, head_mean)

    def project(ch):
        gates = gate_ref[0, ch.rows, :]
        y_ml = ch.ml * lax.rsqrt(ch.ms_ml + EPS) * gml_ref[...] * jax.nn.sigmoid(gates[:, 0:ML_W])
        y_hg = ch.hg * lax.rsqrt(ch.ms_hg + EPS) * ghg_ref[...] * _silu(gates[:, ML_W:ML_W + HG_W])
        ch.y = (_mm(y_ml.astype(BF16), wo_ref[0:ML_W, :])
                + _mm(y_hg.astype(BF16), wo_ref[ML_W:ML_W + HG_W, :])
                + _mm(at_ref[0, ch.rows, :].astype(BF16), wo_ref[ML_W + HG_W:, :]))

    def route(ch):
        x1 = x_ref[0, ch.rows, :] + mod_ref[0, 0, 2:3, :] * ch.y
        x1_ref[0, ch.rows, :] = x1
        h2 = _rms_rows(x1, g2_ref[...]) * (1.0 + mod_ref[0, 0, 4:5, :]) + mod_ref[0, 0, 3:4, :]
        h2_ref[0, ch.rows, :] = h2.astype(BF16)
        h_hi, h_lo = _split(h2)
        ch.logits = _nt(w_hi, h_hi) + _nt(w_hi, h_lo) + _nt(w_lo, h_hi)

    def affinity(ch):
        e = jnp.exp(ch.logits - jnp.max(ch.logits, axis=0, keepdims=True))
        aff_ref[0, :, ch.rows] = e / jnp.sum(e, axis=0, keepdims=True)

    chains = [_Chain(rows=slice(i * half, (i + 1) * half)) for i in range(2)]
    for stage in (mean_squares, project, route, affinity):
        for ch in chains:
            stage(ch)


def _merge(ml, hg, at, ogate, xs, modtok, g_ml, g_hg, w_out, g2, router_w, tm, j0, sel, at_j0):
    bsz, n, d = xs.shape
    n_tiles = n // tm
    tok = lambda width, off: pl.BlockSpec((1, tm, width), lambda i, j: (i, j + off, 0))
    full = lambda shape: pl.BlockSpec(shape, lambda i, j: (0,) * len(shape))
    same_b = _head_tables(CHUNK)[1].astype(BF16)
    return pl.pallas_call(
        _merge_kernel,
        grid=(bsz, n_tiles),
        in_specs=[tok(ML_W, j0), tok(HG_W, j0), tok(MLA_HEADS * MLA_DV, j0 - at_j0), tok(_W_GATE, j0), tok(d, 0),
                  pl.BlockSpec((1, 1, 6, d), lambda i, j: (i, sel, 0, 0)),
                  full((1, ML_W)), full((1, HG_W)), full((d, d)), full((1, d)), full((N_EXPERTS, d)),
                  _const_spec(same_b)],
        out_specs=[tok(d, 0), tok(d, 0), pl.BlockSpec((1, N_EXPERTS, tm), lambda i, j: (i, 0, j))],
        out_shape=[jax.ShapeDtypeStruct((bsz, n, d), F32),
                   jax.ShapeDtypeStruct((bsz, n, d), BF16),
                   jax.ShapeDtypeStruct((bsz, N_EXPERTS, n), F32)],
        compiler_params=_params(("arbitrary", "arbitrary")),
        name="merge",
    )(ml, hg, at, ogate, xs, modtok, g_ml[None, :], g_hg[None, :], w_out.astype(BF16), g2[None, :],
      router_w.T, same_b)


def _route_kernel(aff_ref, rank_ref, *, cap, n):
    aff = aff_ref[0]

    def search(i, thr):
        cand = thr | (jnp.int32(1) << (30 - i))
        cnt = jnp.sum(jnp.where(aff >= pltpu.bitcast(cand, F32), 1, 0), axis=-1, keepdims=True)
        return jnp.where(cnt >= cap, cand, thr)

    thr = lax.fori_loop(0, 31, search, jnp.zeros((N_EXPERTS, 1), jnp.int32))
    above = aff >= pltpu.bitcast(thr + 1, F32)
    tied = jnp.logical_and(aff >= pltpu.bitcast(thr, F32), jnp.logical_not(above))
    need = cap - jnp.sum(jnp.where(above, 1, 0), axis=-1, keepdims=True)

    tl = min(n, TOKEN_TILE)
    r = lax.broadcasted_iota(jnp.int32, (tl, tl), 0)
    c = lax.broadcasted_iota(jnp.int32, (tl, tl), 1)
    before = jnp.where(r < c, 1.0, 0.0).astype(BF16)

    def excl_cumsum(mask):
        parts, carry = [], jnp.zeros((N_EXPERTS, 1), F32)
        for j in range(n // tl):
            m = jnp.where(mask[:, j * tl:(j + 1) * tl], 1.0, 0.0)
            parts.append(_mm(m.astype(BF16), before) + carry)
            carry = carry + jnp.sum(m, axis=-1, keepdims=True)
        return jnp.concatenate(parts, axis=-1).astype(jnp.int32)

    keep = jnp.logical_or(above, jnp.logical_and(tied, excl_cumsum(tied) < need))
    rank_ref[0] = jnp.where(keep, excl_cumsum(keep), -1)


def _route(aff_t, cap):
    bsz, e, n = aff_t.shape
    kern = functools.partial(_route_kernel, cap=cap, n=n)
    return pl.pallas_call(
        kern,
        grid=(bsz,),
        in_specs=[pl.BlockSpec((1, e, n), lambda i: (i, 0, 0))],
        out_specs=pl.BlockSpec((1, e, n), lambda i: (i, 0, 0)),
        out_shape=jax.ShapeDtypeStruct((bsz, e, n), jnp.int32),
        compiler_params=_params(("arbitrary",)),
        name="route",
    )(aff_t)


def _pick_and_gate(rank_row, aff_row, cap):
    slot = lax.broadcasted_iota(jnp.int32, (cap, rank_row.shape[1]), 0)
    chosen = rank_row == slot
    gate = jnp.sum(jnp.where(chosen, aff_row, 0.0), axis=-1, keepdims=True)
    return jnp.where(chosen, 1.0, 0.0).astype(BF16), gate


def _ffn_experts(xs, picks, gates, x1_ref, g2_ref, wg_ref, wu_ref, wd_ref, o_ref, cap):
    e = pl.program_id(1)
    hid = (_silu(_mm(xs, wg_ref[0, 0])) * _mm(xs, wu_ref[0, 0])).astype(BF16)
    out = _mm(hid, wd_ref[0, 0])

    @pl.when(e == 0)
    def _():
        o_ref[...] = jnp.zeros_like(o_ref)

    d = out.shape[1]
    for b, (pick, gate) in enumerate(zip(picks, gates)):
        out_b = (out[b * cap:(b + 1) * cap] * gate).astype(BF16)
        for j in range(d // SCATTER_COLS):
            cols = slice(j * SCATTER_COLS, (j + 1) * SCATTER_COLS)
            o_ref[b, :, cols] += _tn(pick, out_b[:, cols])

    @pl.when(e == pl.num_programs(1) - 1)
    def _():
        o_ref[...] = x1_ref[...] + g2_ref[...] * o_ref[...]


def _ffn_kernel(rank_ref, aff_ref, h_ref, x1_ref, g2_ref, wg_ref, wu_ref, wd_ref, o_ref, *, cap, bb):
    picks, gates, xs = [], [], []
    for b in range(bb):
        pick, gate = _pick_and_gate(rank_ref[b, 0], aff_ref[b, 0], cap)
        picks.append(pick)
        gates.append(gate)
        xs.append(_mm(pick, h_ref[b].astype(BF16)).astype(BF16))
    _ffn_experts(jnp.concatenate(xs, axis=0), picks, gates, x1_ref, g2_ref, wg_ref, wu_ref, wd_ref, o_ref, cap)


def _ffn(rank, aff_t, h2, x1, g2mod, wg, wu, wd, layer, cap, bb):
    bsz, n, d = h2.shape
    n_exp = wg.shape[1]
    once = dict(pipeline_mode=pl.Buffered(1))
    per_expert = lambda width: pl.BlockSpec((bb, 1, 1, width), lambda i, e: (i, e, 0, 0))
    common = [pl.BlockSpec((bb, n, d), lambda i, e: (i, 0, 0), **once),
              pl.BlockSpec((bb, 1, d), lambda i, e: (i, 0, 0)),
              pl.BlockSpec((1, 1, d, d), lambda i, e: (layer, e, 0, 0)),
              pl.BlockSpec((1, 1, d, d), lambda i, e: (layer, e, 0, 0)),
              pl.BlockSpec((1, 1, d, d), lambda i, e: (layer, e, 0, 0))]
    out_spec = pl.BlockSpec((bb, n, d), lambda i, e: (i, 0, 0))
    out_shape = jax.ShapeDtypeStruct((bsz, n, d), F32)
    rank4, aff4 = rank.reshape(bsz, n_exp, 1, n), aff_t.reshape(bsz, n_exp, 1, n)
    return pl.pallas_call(
        functools.partial(_ffn_kernel, cap=cap, bb=bb),
        grid=(bsz // bb, n_exp),
        in_specs=[per_expert(n), per_expert(n), pl.BlockSpec((bb, n, d), lambda i, e: (i, 0, 0), **once)] + common,
        out_specs=out_spec, out_shape=out_shape,
        compiler_params=_params(("arbitrary", "arbitrary")),
        name="expert_ffn",
    )(rank4, aff4, h2, x1, g2mod, wg, wu, wd)


def _rope_tables(ctx_len, seq):
    half = MLA_ROPE // 2
    inv = ROPE_THETA ** (-jnp.arange(0, half, 2, dtype=F32) / half)
    rows = seq // GRID_W
    row_pos = jnp.repeat(jnp.arange(rows), GRID_W).astype(F32)
    col_pos = jnp.broadcast_to(jnp.arange(GRID_W), (rows, GRID_W)).reshape(-1).astype(F32)

    def cs(pos):
        ang = pos[:, None] * inv[None, :]
        ang = jnp.concatenate([ang, ang], axis=-1)
        return jnp.cos(ang), jnp.sin(ang)

    cos_r, sin_r = cs(row_pos)
    cos_c, sin_c = cs(col_pos)
    zeros = lambda n: jnp.zeros((seq, n), F32)
    ones = lambda n: jnp.ones((seq, n), F32)
    tail = HEAD_PAD - MLA_DQK
    cos = jnp.concatenate([ones(MLA_NOPE), cos_r, cos_c, ones(tail)], axis=-1)
    sin = jnp.concatenate([zeros(MLA_NOPE), sin_r, sin_c, zeros(tail)], axis=-1)
    ident = lambda a, fill: jnp.concatenate([jnp.full((ctx_len, HEAD_PAD), fill, F32), a], axis=0)
    return ident(cos, 1.0), ident(sin, 0.0)


def kernel(x, c, ctx, c_ctx, ada_w, ada_b, norm1_g, norm2_g, w_in, b_in, ml_conv_w, ml_conv_b, ml_norm_g, hg_lb_logits, hg_norm_g, mla_q_norm_g, mla_w_uq, mla_kv_norm_g, mla_w_ukv, mla_q_qk_g, mla_k_qk_g, w_out, router_w, ex_w_gate, ex_w_up, ex_w_down):
    bsz, seq, d = x.shape
    ctx_len = ctx.shape[1]
    depth = ada_w.shape[0]
    t = ctx_len + seq
    tm = TOKEN_TILE if ctx_len % TOKEN_TILE == 0 else TOKEN_TILE // 2
    assert ctx_len % tm == 0 and seq % tm == 0 and ctx_len % ML_CHUNK == 0 and seq % ML_CHUNK == 0 and seq % GRID_W == 0
    nct = ctx_len // tm

    rows = -(-(bsz + 1) // 8) * 8
    cc = jnp.concatenate([c, c_ctx[None, :], jnp.zeros((rows - bsz - 1, d), F32)], axis=0)
    mod = _modulation(cc, ada_w, ada_b)
    cos, sin = _rope_tables(ctx_len, seq)
    bb_ctx = max(bb for bb in (8, 4, 2, 1) if bsz % bb == 0)

    wg, wu, wd = (w.astype(BF16) for w in (ex_w_gate, ex_w_up, ex_w_down))
    lat = x
    for layer in range(depth):
        need_ctx = layer < depth - 1
        m = mod[layer]
        modtok = jnp.stack([jnp.broadcast_to(m[bsz], (bsz, 6 * d)), m[:bsz]], axis=1).reshape(bsz, 2, 6, d)
        mla_operands, mla_specs = _mla_operands(mla_q_norm_g[layer], mla_kv_norm_g[layer], mla_w_uq[layer],
                                                mla_w_ukv[layer], mla_q_qk_g[layer], mla_k_qk_g[layer], cos, sin, tm)
        oml, ohqv, ohf, ogate, omisc, ogt, q, k, v = _in_proj(ctx, lat, modtok, norm1_g[layer], w_in[layer], b_in[layer],
                                                              mla_operands, mla_specs, tm, nct)
        gates_t = ogt.reshape(bsz, 16, t // ML_CHUNK, ML_CHUNK).transpose(0, 2, 1, 3)
        ml = _mlstm(oml, omisc, gates_t, ml_conv_w[layer], ml_conv_b[layer], ctx_len, tm)
        hg = _hgrn(ohqv, ohf, hg_lb_logits, ctx_len, layer)
        at_j0 = 0 if need_ctx else nct
        at = _attention(q, k, v, ctx_len, tm, at_j0)
        g2mod = modtok[:, :, 5:6, :]

        def post(xs, j0, sel, bb):
            x1, h2, aff = _merge(ml, hg, at, ogate, xs, modtok, ml_norm_g[layer], hg_norm_g[layer], w_out[layer],
                                 norm2_g[layer], router_w[layer], tm, j0, sel, at_j0)
            cap = EC_CAPACITY * xs.shape[1] // N_EXPERTS
            rank = _route(aff, cap)
            return _ffn(rank, aff, h2, x1, g2mod[:, sel], wg, wu, wd, layer, cap, bb)

        new_lat = post(lat, nct, 1, 1)
        if need_ctx:
            ctx = post(ctx, 0, 0, bb_ctx)
        lat = new_lat
    return lat
```

```python
import functools

import numpy as np
import jax
import jax.numpy as jnp
from jax import lax
from jax.experimental import pallas as pl
from jax.experimental.pallas import tpu as pltpu

F32 = jnp.float32
BF16 = jnp.bfloat16

GRID_W = 64
ML_HEADS = 4
ML_DH = 64
ML_W = 256
HG_HEADS = 4
HG_W = 256
MLA_HEADS = 8
MLA_NOPE = 64
MLA_ROPE = 32
MLA_DQK = 96
MLA_DV = 64
MLA_Q_RANK = 256
MLA_KV_RANK = 128
N_EXPERTS = 16
EC_CAPACITY = 2
CHUNK = 64
ML_CHUNK = 128
ROPE_THETA = 10000.0
EPS = 1e-6
HEAD_PAD = 128
LANES = 128
N_LEVELS = 6
N_MM_LEVELS = 3
SLAB = HG_HEADS * CHUNK
ML_SLAB = ML_HEADS * ML_CHUNK
VMEM_LIMIT = 52 * 1024 * 1024

_O_QK, _O_V, _O_OG, _O_GATES, _O_HQ, _O_HI, _O_HGATE, _O_HF, _O_CQ, _O_CKV, _O_KR, _O_END = (
    0, 512, 768, 1024, 1040, 1296, 1552, 1808, 2320, 2576, 2704, 2736)
_W_ML = 2 * ML_W + ML_W
_W_HQV = 2 * HG_W
_W_HF = 2 * HG_W
_W_GATE = ML_W + HG_W
_W_MLA = MLA_Q_RANK + MLA_KV_RANK + LANES
_C_ML, _C_HQV, _C_HF, _C_GATE, _C_MLA, _C_END = np.cumsum([0, _W_ML, _W_HQV, _W_HF, _W_GATE, _W_MLA]).tolist()
MISC_BLOCK = (MLA_Q_RANK + MLA_KV_RANK) // LANES
MOD_TILE = 1024
TOKEN_TILE = 256
SCATTER_COLS = 256


def _nt(a, b, **kw):
    return lax.dot_general(a, b, (((1,), (1,)), ((), ())), preferred_element_type=F32, **kw)


def _tn(a, b, **kw):
    return lax.dot_general(a, b, (((0,), (0,)), ((), ())), preferred_element_type=F32, **kw)


def _mm(a, b, **kw):
    return jnp.dot(a, b, preferred_element_type=F32, **kw)


def _split(x):
    hi = x.astype(BF16)
    return hi, (x - hi.astype(F32)).astype(BF16)


def _silu(x):
    return x * jax.nn.sigmoid(x)


def _log_sigmoid(x):
    return jnp.minimum(x, 0.0) - jnp.log1p(jnp.exp(-jnp.abs(x)))


def _rms_rows(x, g):
    return x * lax.rsqrt(jnp.mean(x * x, axis=-1, keepdims=True) + EPS) * g


def _params(sem=None):
    return pltpu.CompilerParams(dimension_semantics=sem, vmem_limit_bytes=VMEM_LIMIT)


def _const_spec(a):
    return pl.BlockSpec(a.shape, lambda *_: (0,) * a.ndim)


def _mod_kernel(c_ref, w_ref, b_ref, o_ref):
    s = _silu(c_ref[...]).astype(BF16)
    o_ref[0] = _mm(s, w_ref[0].astype(BF16)) + b_ref[0]


def _modulation(cc, ada_w, ada_b):
    n_layers, d, n6 = ada_w.shape
    rows = cc.shape[0]
    tn = MOD_TILE
    return pl.pallas_call(
        _mod_kernel,
        grid=(n_layers, n6 // tn),
        in_specs=[pl.BlockSpec((rows, d), lambda l, j: (0, 0)),
                  pl.BlockSpec((1, d, tn), lambda l, j: (l, 0, j)),
                  pl.BlockSpec((1, 1, tn), lambda l, j: (l, 0, j))],
        out_specs=pl.BlockSpec((1, rows, tn), lambda l, j: (l, 0, j)),
        out_shape=jax.ShapeDtypeStruct((n_layers, rows, n6), F32),
        compiler_params=_params(("arbitrary", "arbitrary")),
        name="modulation",
    )(cc, ada_w, ada_b.reshape(n_layers, 1, n6))


def _in_kernel(c_ref, l_ref, mod_ref, g_ref, w_ref, b_ref, wgt_ref, bgt_ref, *rest, nct):
    mla_refs, (oml_ref, ohqv_ref, ohf_ref, ogate_ref, omisc_ref, ogt_ref, q_ref, k_ref, v_ref) = rest[:12], rest[12:]
    x = jnp.where(pl.program_id(1) < nct, c_ref[0], l_ref[0])
    sh = mod_ref[0, 0, 0:1, :]
    sc = mod_ref[0, 0, 1:2, :]
    h = (_rms_rows(x, g_ref[...]) * (1.0 + sc) + sh).astype(BF16)
    proj = lambda lo, hi: _mm(h, w_ref[:, lo:hi]) + b_ref[:, lo:hi]
    mla = proj(_C_MLA, _C_END)
    omisc_ref[0] = mla[:, MISC_BLOCK * LANES:]
    _mla_project(mla, *mla_refs, q_ref, k_ref, v_ref)
    oml_ref[0] = proj(_C_ML, _C_HQV).astype(BF16)
    ohqv_ref[0] = proj(_C_HQV, _C_HF).astype(BF16)
    ohf_ref[0] = proj(_C_HF, _C_GATE)
    ogate_ref[0] = proj(_C_GATE, _C_MLA)
    ogt_ref[0] = _nt(wgt_ref[...], h) + bgt_ref[...]


def _in_proj(ctx_x, lat_x, modtok, g1, w_in, b_in, mla_operands, mla_specs, tm, nct):
    bsz, seq, d = lat_x.shape
    t = ctx_x.shape[1] + seq
    src, sign = _rotate_half_columns(MLA_ROPE)
    zeros = lambda n: jnp.zeros((d, n), F32)
    w = jnp.concatenate([
        w_in[:, _O_QK:_O_OG], w_in[:, _O_HQ:_O_HGATE], w_in[:, _O_HF:_O_CQ],
        w_in[:, _O_OG:_O_GATES], w_in[:, _O_HGATE:_O_HF], w_in[:, _O_CQ:_O_KR],
        w_in[:, _O_GATES:_O_HQ], zeros(48), w_in[:, _O_KR:_O_END],
        w_in[:, _O_KR:_O_END][:, src] * sign], axis=1).astype(BF16)
    zb = lambda n: jnp.zeros((n,), F32)
    b = jnp.concatenate([
        b_in[_O_QK:_O_OG], b_in[_O_HQ:_O_HGATE], b_in[_O_HF:_O_CQ],
        b_in[_O_OG:_O_GATES], b_in[_O_HGATE:_O_HF], b_in[_O_CQ:_O_KR],
        b_in[_O_GATES:_O_HQ], zb(48), b_in[_O_KR:_O_END], b_in[_O_KR:_O_END][src] * sign])[None, :]
    wgt = w_in[:, _O_GATES:_O_HQ].T.astype(BF16)
    bgt = b_in[_O_GATES:_O_HQ][:, None]
    nw = w.shape[1]
    tok = lambda width: pl.BlockSpec((1, tm, width), lambda i, j: (i, j, 0))
    full = lambda shape: pl.BlockSpec(shape, lambda i, j: (0,) * len(shape))
    return pl.pallas_call(
        functools.partial(_in_kernel, nct=nct),
        grid=(bsz, t // tm),
        in_specs=[pl.BlockSpec((1, tm, d), lambda i, j: (i, jnp.minimum(j, nct - 1), 0)),
                  pl.BlockSpec((1, tm, d), lambda i, j: (i, jnp.maximum(j - nct, 0), 0)),
                  pl.BlockSpec((1, 1, 6, d), lambda i, j: (i, jnp.where(j >= nct, 1, 0), 0, 0)),
                  full((1, d)), full((d, nw)), full((1, nw)), full((16, d)), full((16, 1))] + mla_specs,
        out_specs=[tok(_W_ML), tok(_W_HQV), tok(_W_HF), tok(_W_GATE), tok(LANES),
                   pl.BlockSpec((1, 16, tm), lambda i, j: (i, 0, j)),
                   pl.BlockSpec((1, MLA_HEADS, tm, HEAD_PAD), lambda i, j: (i, 0, j, 0)),
                   pl.BlockSpec((1, MLA_HEADS, tm, HEAD_PAD), lambda i, j: (i, 0, j, 0)),
                   pl.BlockSpec((1, MLA_HEADS, tm, MLA_DV), lambda i, j: (i, 0, j, 0))],
        out_shape=[jax.ShapeDtypeStruct((bsz, t, _W_ML), BF16),
                   jax.ShapeDtypeStruct((bsz, t, _W_HQV), BF16),
                   jax.ShapeDtypeStruct((bsz, t, _W_HF), F32),
                   jax.ShapeDtypeStruct((bsz, t, _W_GATE), F32),
                   jax.ShapeDtypeStruct((bsz, t, LANES), F32),
                   jax.ShapeDtypeStruct((bsz, 16, t), F32),
                   jax.ShapeDtypeStruct((bsz, MLA_HEADS, t, HEAD_PAD), BF16),
                   jax.ShapeDtypeStruct((bsz, MLA_HEADS, t, HEAD_PAD), BF16),
                   jax.ShapeDtypeStruct((bsz, MLA_HEADS, t, MLA_DV), BF16)],
        compiler_params=_params(("arbitrary", "arbitrary")),
        name="in_proj",
    )(ctx_x, lat_x, modtok, g1[None, :], w, b, wgt, bgt, *mla_operands)


def _dir1_chunk(i, n_ctx_chunks, n_chunks):
    return jnp.where(i < n_ctx_chunks, n_ctx_chunks - 1 - i, n_chunks - 1 - (i - n_ctx_chunks))


def _head_tables(chunk):
    feat_head = np.arange(ML_W) // ML_DH
    stack = (np.arange(ML_HEADS * chunk)[:, None] // chunk == feat_head[None, :]).astype(np.float32)
    square = (feat_head[:, None] == feat_head[None, :]).astype(np.float32)
    return jnp.asarray(stack, BF16), jnp.asarray(square, F32)


def _stack_heads(x_bf, same_bf):
    return jnp.concatenate([x_bf] * ML_HEADS, axis=0) * same_bf


class _Chain:
    def __init__(self, **kw):
        self.__dict__.update(kw)


def _samples_per_step(bsz):
    return 2 if bsz % 2 == 0 else 1


def _mm_split(a, b):
    hi, lo = _split(a)
    return _mm(hi, b) + _mm(lo, b)


def _mlstm_tables():
    t = np.arange(ML_CHUNK)
    lower = (t[:, None] >= t[None, :]).astype(np.float32)
    tri = np.stack([lower, lower.T])
    neg = np.stack([np.tile(np.where(m > 0, 0.0, -np.inf), (1, ML_HEADS)) for m in (lower, lower.T)]).astype(np.float32)
    spread = np.zeros((2, LANES, ML_W), np.float32)
    spread_s = np.zeros((2, LANES, ML_SLAB), np.float32)
    for d in range(2):
        for h in range(ML_HEADS):
            spread[d, d * 8 + 4 + h, h * ML_DH:(h + 1) * ML_DH] = 1.0
            spread_s[d, d * 8 + 4 + h, h * ML_CHUNK:(h + 1) * ML_CHUNK] = 1.0
    as_bf = lambda a: jnp.asarray(a, BF16)
    return (as_bf(tri), jnp.asarray(neg), as_bf(spread), as_bf(spread.transpose(0, 2, 1)),
            as_bf(spread_s), as_bf(spread_s.transpose(0, 2, 1)))


def _mlstm_kernel(ml_ref, gc_ref, gt_ref, cw_ref, cb_ref, tri_ref, neg_ref, spread_ref, gather_ref, spread_s_ref,
                  gather_s_ref, sameb_ref, samef_ref, o_ref, qk_s, c_s, n_s, m_s, *, t, ctx_len, tm):
    n_chunks = t // ML_CHUNK
    n_ctx_chunks = ctx_len // ML_CHUNK
    n_samples = ml_ref.shape[0]
    cw = cw_ref[...]
    cb = cb_ref[...]
    rid = lax.broadcasted_iota(jnp.int32, (tm, 2 * ML_W), 0)
    for b in range(n_samples):
        for j in range(t // tm):
            r0 = j * tm
            cur = ml_ref[b, r0:r0 + tm, 0:2 * ML_W].astype(F32)
            up = pltpu.roll(cur, 1, 0)
            if r0 in (0, ctx_len):
                up = jnp.where(rid == 0, 0.0, up)
            else:
                up = jnp.where(rid == 0, ml_ref[b, r0 - 16:r0, 0:2 * ML_W].astype(F32)[15:16], up)
            dn = pltpu.roll(cur, tm - 1, 0)
            if r0 + tm in (ctx_len, t):
                dn = jnp.where(rid == tm - 1, 0.0, dn)
            else:
                dn = jnp.where(rid == tm - 1, ml_ref[b, r0 + tm:r0 + tm + 16, 0:2 * ML_W].astype(F32)[0:1], dn)
            y = cw[0:1, :] * up + cw[1:2, :] * cur + cw[2:3, :] * dn + cb
            qk_s[b, r0:r0 + tm, :] = _silu(y)

    o_ref[...] = jnp.zeros_like(o_ref)
    c_s[...] = jnp.zeros_like(c_s)
    n_s[...] = jnp.zeros_like(n_s)
    m_s[...] = jnp.zeros_like(m_s)
    lane_id = lax.broadcasted_iota(jnp.int32, (ML_CHUNK, LANES), 1)
    gate_lane = lane_id < 16
    forget_lane = [(lane_id >= d * 8 + 4) & (lane_id < d * 8 + 8) for d in range(2)]
    tok = lax.broadcasted_iota(jnp.int32, (ML_CHUNK, LANES), 0)

    def load(ch):
        d, b = ch.d, ch.b
        ch.rows = pl.ds(pl.multiple_of(ch.c * ML_CHUNK, ML_CHUNK), ML_CHUNK)
        qk = qk_s[b, ch.rows, :]
        ch.qb = qk[:, 0:ML_W].astype(BF16)
        ch.k = qk[:, ML_W:2 * ML_W] * (ML_DH ** -0.5)
        ch.vb = ml_ref[b, ch.rows, 2 * ML_W:3 * ML_W]
        ch.gc = jnp.where(gate_lane, gc_ref[b, ch.rows, :], 0.0)
        ch.gt = gt_ref[b, ch.c]
        hi, lo = _split(_log_sigmoid(jnp.where(forget_lane[d], ch.gc, 0.0)))
        ch.b_c = _mm(tri_ref[d], hi) + _mm(tri_ref[d], lo)
        ch.b_r = _mm_split(_log_sigmoid(ch.gt), tri_ref[1 - d])
        ch.s_raw = _nt(ch.qb, _stack_heads(ch.k.astype(BF16), sameb_ref[...]))
        ch.qn = _mm((qk[:, 0:ML_W] * n_s[b, d][0:1, :]).astype(BF16), gather_ref[d])
        ch.c_mat = c_s[b, d]
        ch.qc = _mm(ch.qb, ch.c_mat.astype(BF16))

    def stabilise(ch):
        d = ch.d
        ch.u = jnp.where(forget_lane[d], pltpu.roll(ch.gc, 4, 1) - ch.b_c, 0.0)
        cmax = ch.u
        for step in [1 << i for i in range(ML_CHUNK.bit_length() - 1)]:
            if d == 0:
                moved = jnp.where(tok >= step, pltpu.roll(cmax, step, 0), -jnp.inf)
            else:
                moved = jnp.where(tok < ML_CHUNK - step, pltpu.roll(cmax, ML_CHUNK - step, 0), -jnp.inf)
            cmax = jnp.maximum(cmax, moved)
        ch.m = m_s[ch.b, d][0:1, :]
        inter = ch.b_c + ch.m
        ch.m_t = jnp.maximum(inter, ch.b_c + cmax)
        ch.w_st = jnp.exp(inter - ch.m_t)
        last = ML_CHUNK - 1 if d == 0 else 0
        ch.bend = ch.b_c[last:last + 1, :]
        ch.m_new = jnp.maximum(ch.bend + ch.m, ch.bend + cmax[last:last + 1, :])
        fcols = [d * 8 + 4 + h for h in range(ML_HEADS)]
        icols = [d * 8 + h for h in range(ML_HEADS)]
        ch.u_row = jnp.concatenate([ch.gt[i:i + 1, :] - ch.b_r[f:f + 1, :] for i, f in zip(icols, fcols)], axis=1)
        ch.bm_wide = _mm_split(ch.b_c - ch.m_t, spread_s_ref[d])
        ch.wk_wide = _mm(jnp.exp(ch.bend + ch.u - ch.m_new).astype(BF16), spread_ref[d])
        ch.w_old = jnp.broadcast_to(jnp.exp(ch.bend + ch.m - ch.m_new), (8, LANES))
        ch.wold_wide = _mm_split(ch.w_old, spread_ref[d])

    def weigh(ch):
        d = ch.d
        s = ch.s_raw * jnp.exp(ch.bm_wide + ch.u_row + neg_ref[d])
        sb = s.astype(BF16)
        ch.intra = _mm(sb, _stack_heads(ch.vb, sameb_ref[...]))
        ch.rowsum = _mm(sb, gather_s_ref[d])
        kw = ch.k * ch.wk_wide
        ch.dc = _tn(kw.astype(BF16), ch.vb)
        ch.dn = jnp.sum(kw, axis=0, keepdims=True)

    def normalise(ch):
        d = ch.d
        den = ch.rowsum + ch.w_st * ch.qn
        r = jnp.where(forget_lane[d], 1.0 / jnp.maximum(jnp.abs(den), jnp.exp(-ch.m_t)), 0.0)
        ch.r_wide = _mm(r.astype(BF16), spread_ref[d])
        ch.wr_wide = _mm((ch.w_st * r).astype(BF16), spread_ref[d])

    def store(ch):
        b, d = ch.b, ch.d
        o_ref[b, ch.rows, :] += ch.intra * ch.r_wide + ch.qc * ch.wr_wide
        w_old = ch.wold_wide[0:1, :]
        c_s[b, d] = w_old * ch.c_mat + ch.dc * samef_ref[...]
        n_s[b, d] = jnp.broadcast_to(w_old * n_s[b, d][0:1, :] + ch.dn, (8, ML_W))
        m_s[b, d] = jnp.broadcast_to(ch.m_new, (8, LANES))

    def body(i, carry):
        chunk = (i, _dir1_chunk(i, n_ctx_chunks, n_chunks))
        chains = [_Chain(b=b, d=d, c=chunk[d]) for b in range(n_samples) for d in range(2)]
        for stage in (load, stabilise, weigh, normalise, store):
            for ch in chains:
                stage(ch)
        return carry

    lax.fori_loop(0, n_chunks, body, 0)


def _mlstm(oml, omisc, gates_t, conv_w, conv_b, ctx_len, tm):
    bsz, t, _ = oml.shape
    n_chunks = t // ML_CHUNK
    kern = functools.partial(_mlstm_kernel, t=t, ctx_len=ctx_len, tm=tm)
    consts = _mlstm_tables() + _head_tables(ML_CHUNK)
    sps = _samples_per_step(bsz)
    return pl.pallas_call(
        kern,
        grid=(bsz // sps,),
        in_specs=[pl.BlockSpec((sps, t, _W_ML), lambda i: (i, 0, 0)),
                  pl.BlockSpec((sps, t, LANES), lambda i: (i, 0, 0)),
                  pl.BlockSpec((sps, n_chunks, 16, ML_CHUNK), lambda i: (i, 0, 0, 0)),
                  pl.BlockSpec((3, 2 * ML_W), lambda i: (0, 0)),
                  pl.BlockSpec((1, 2 * ML_W), lambda i: (0, 0))] + [_const_spec(a) for a in consts],
        out_specs=pl.BlockSpec((sps, t, ML_W), lambda i: (i, 0, 0)),
        out_shape=jax.ShapeDtypeStruct((bsz, t, ML_W), F32),
        scratch_shapes=[pltpu.VMEM((sps, t, 2 * ML_W), F32),
                        pltpu.VMEM((sps, 2, ML_W, ML_W), F32),
                        pltpu.VMEM((sps, 2, 8, ML_W), F32),
                        pltpu.VMEM((sps, 2, 8, LANES), F32)],
        compiler_params=_params(("arbitrary",)),
        name="mlstm",
    )(oml, omisc, gates_t, conv_w, conv_b[None, :], *consts)


def _hgrn_tables():
    sums = np.zeros((2, (N_MM_LEVELS + 1) * CHUNK, CHUNK), np.float32)
    pairs = np.zeros((2, N_LEVELS + 1, CHUNK, CHUNK), np.float32)
    sign = np.zeros((2, N_LEVELS - N_MM_LEVELS, CHUNK, 1), np.float32)
    for d in range(2):
        p = (lambda a: a) if d == 0 else (lambda a: CHUNK - 1 - a)
        for t in range(CHUNK):
            pairs[d, 0, p(t), p(t)] = 1.0
            for s in range(t + 1):
                sums[d, p(t), p(s)] = 1.0
        for l in range(N_LEVELS):
            w = 1 << l
            for t in range(CHUNK):
                ref = t - (t % (2 * w)) + w - 1
                if l < N_MM_LEVELS:
                    lo, hi = (ref + 1, t) if t > ref else (t + 1, ref)
                    for s in range(lo, hi + 1):
                        sums[d, (l + 1) * CHUNK + p(t), p(s)] = 1.0
                else:
                    sign[d, l - N_MM_LEVELS, p(t), 0] = 1.0 if t > ref else -1.0
                if t > ref:
                    for s in range(ref - w + 1, ref + 1):
                        pairs[d, l + 1, p(t), p(s)] = 1.0
    pairs = np.tile(pairs, (1, 1, 1, HG_HEADS))
    sign = np.broadcast_to(sign, sign.shape[:3] + (HG_W,))
    return jnp.asarray(sums, BF16), jnp.asarray(sign), jnp.asarray(pairs)


def _hgrn_ref_rows(cum, d, l):
    w = 1 << l
    parts = []
    for base in range(0, CHUNK, 2 * w):
        r = base + w - 1 if d == 0 else base + w
        parts.append(jnp.broadcast_to(cum[r:r + 1, :], (2 * w, cum.shape[1])))
    return parts[0] if len(parts) == 1 else jnp.concatenate(parts, axis=0)


def _hgrn_kernel(hqv_ref, hf_ref, lbl_ref, sum_ref, sign_ref, pair_ref, sameb_ref, samef_ref, o_ref, st_s,
                 *, t, ctx_len, layer):
    n_chunks = t // CHUNK
    n_ctx_chunks = ctx_len // CHUNK
    w_all = HG_W

    if layer > 0:
        logits = lbl_ref[...]
        n_layers = logits.shape[0]
        mx = logits[0]
        for l in range(1, n_layers):
            mx = jnp.maximum(mx, logits[l])
        ex = [jnp.exp(logits[l] - mx) for l in range(n_layers)]
        tot = ex[0]
        for l in range(1, n_layers):
            tot = tot + ex[l]
        low_all = ex[1] / tot
        for l in range(2, layer + 1):
            low_all = low_all + ex[l] / tot
        log_low = jnp.log(low_all)
        log_1m_low = jnp.log1p(-low_all)

    o_ref[...] = jnp.zeros_like(o_ref)
    st_s[...] = jnp.zeros_like(st_s)

    def gates(ch):
        b, d = ch.b, ch.d
        ch.rows = pl.ds(pl.multiple_of(ch.c * CHUNK, CHUNK), CHUNK)
        ch.q = _silu(hqv_ref[b, ch.rows, 0:w_all].astype(F32))
        ch.vb = hqv_ref[b, ch.rows, w_all:2 * w_all]
        pre = hf_ref[b, ch.rows, d * w_all:(d + 1) * w_all]
        ls = _log_sigmoid(pre)
        if layer == 0:
            log_f = ls
            ch.key = jnp.exp(ls - pre)
        else:
            lo_, hi_ = log_low[d:d + 1, :], log_1m_low[d:d + 1, :] + ls
            log_f = jnp.maximum(lo_, hi_) + jnp.log1p(jnp.exp(-jnp.abs(lo_ - hi_)))
            ch.key = (1.0 - low_all[d:d + 1, :]) * jnp.exp(ls - pre)
        hi, lo = _split(log_f)
        ch.xb = _mm(sum_ref[d], hi) + _mm(sum_ref[d], lo)
        ch.st = st_s[b, d]
        ch.amat = _nt(ch.q.astype(BF16), _stack_heads(ch.key.astype(BF16), sameb_ref[...])) * pair_ref[d, 0]

    def levels(ch):
        d = ch.d
        cum = ch.xb[0:CHUNK]
        for l in range(N_LEVELS):
            if l < N_MM_LEVELS:
                expo = ch.xb[(l + 1) * CHUNK:(l + 2) * CHUNK]
            else:
                expo = (cum - _hgrn_ref_rows(cum, d, l)) * sign_ref[d, l - N_MM_LEVELS]
            fac = jnp.exp(expo)
            qt = (ch.q * fac).astype(BF16)
            kt = (ch.key * fac).astype(BF16)
            ch.amat = ch.amat + _nt(qt, _stack_heads(kt, sameb_ref[...])) * pair_ref[d, l + 1]
        last = CHUNK - 1 if d == 0 else 0
        bend = cum[last:last + 1, :]
        ch.inter = _nt((ch.q * jnp.exp(cum)).astype(BF16), ch.st.astype(BF16))
        ch.dst = _tn(ch.vb, (ch.key * jnp.exp(bend - cum)).astype(BF16))
        ch.decay = jnp.exp(bend)

    def readout(ch):
        ch.intra = _mm(ch.amat.astype(BF16), _stack_heads(ch.vb, sameb_ref[...]))

    def store(ch):
        o_ref[ch.b, ch.rows, :] += ch.intra + ch.inter
        st_s[ch.b, ch.d] = ch.decay * ch.st + ch.dst * samef_ref[...]

    def body(i, carry):
        chunk = (i, _dir1_chunk(i, n_ctx_chunks, n_chunks))
        chains = [_Chain(b=b, d=d, c=chunk[d]) for b in range(hf_ref.shape[0]) for d in range(2)]
        for stage in (gates, levels, readout, store):
            for ch in chains:
                stage(ch)
        return carry

    lax.fori_loop(0, n_chunks, body, 0)


def _hgrn(ohqv, ohf, lb_logits, ctx_len, layer):
    bsz, t, _ = ohf.shape
    kern = functools.partial(_hgrn_kernel, t=t, ctx_len=ctx_len, layer=layer)
    consts = (lb_logits,) + _hgrn_tables() + _head_tables(CHUNK)
    sps = _samples_per_step(bsz)
    return pl.pallas_call(
        kern,
        grid=(bsz // sps,),
        in_specs=[pl.BlockSpec((sps, t, 2 * HG_W), lambda i: (i, 0, 0)),
                  pl.BlockSpec((sps, t, 2 * HG_W), lambda i: (i, 0, 0))] + [_const_spec(a) for a in consts],
        out_specs=pl.BlockSpec((sps, t, HG_W), lambda i: (i, 0, 0)),
        out_shape=jax.ShapeDtypeStruct((bsz, t, HG_W), F32),
        scratch_shapes=[pltpu.VMEM((sps, 2, HG_W, HG_W), F32)],
        compiler_params=_params(("arbitrary",)),
        name="hgrn",
    )(ohqv, ohf, *consts)


def _mla_project(x, gq_ref, gkv_ref, wq_ref, wqr_ref, wk_ref, wv_ref, gqq_ref, gqr_ref, gkk_ref, gkr_ref,
                 cos_ref, sin_ref, q_ref, k_ref, v_ref):
    cq = _rms_rows(x[:, 0:MLA_Q_RANK], gq_ref[...]).astype(BF16)
    ckv = _rms_rows(x[:, MLA_Q_RANK:MLA_Q_RANK + MLA_KV_RANK], gkv_ref[...]).astype(BF16)
    misc = x[:, MLA_Q_RANK + MLA_KV_RANK:]
    lane = lax.broadcasted_iota(jnp.int32, misc.shape, 1)
    rope_lane = (lane >= MLA_NOPE) & (lane < MLA_DQK)
    k_rope = jnp.where(rope_lane, misc, 0.0)
    k_rope_rot = jnp.where(rope_lane, pltpu.roll(misc, HEAD_PAD - MLA_ROPE, 1), 0.0)
    q_raw = _mm(cq, wq_ref[...])
    q_rot = _mm(cq, wqr_ref[...])
    k_raw = _mm(ckv, wk_ref[...])
    v_all = _mm(ckv, wv_ref[...]).astype(BF16)
    cos, sin = cos_ref[...], sin_ref[...]

    def norm_rope(xh, xr, g, gr):
        ms = jnp.sum(xh * xh, axis=-1, keepdims=True) * (1.0 / MLA_DQK)
        return (xh * (g * cos) + xr * (gr * sin)) * lax.rsqrt(ms + EPS)

    for h in range(MLA_HEADS):
        sl = slice(h * HEAD_PAD, (h + 1) * HEAD_PAD)
        q = norm_rope(q_raw[:, sl], q_rot[:, sl], gqq_ref[...], gqr_ref[...])
        q_ref[0, h] = (q * (MLA_DQK ** -0.5)).astype(BF16)
        k_ref[0, h] = norm_rope(k_raw[:, sl] + k_rope, k_rope_rot, gkk_ref[...], gkr_ref[...]).astype(BF16)
        v_ref[0, h] = v_all[:, h * MLA_DV:(h + 1) * MLA_DV]


def _rotate_half_columns(n):
    j = np.arange(n)
    first = j % 16 < 8
    return np.where(first, j + 8, j - 8), np.where(first, -1.0, 1.0).astype(np.float32)


def _mla_operands(gq, gkv, w_uq, w_ukv, g_qq, g_kk, cos, sin, tm):
    hw = MLA_HEADS * HEAD_PAD
    src, sign = _rotate_half_columns(MLA_ROPE)
    src_pad = np.arange(HEAD_PAD)
    src_pad[MLA_NOPE:MLA_DQK] = MLA_NOPE + src
    sign_pad = np.zeros((HEAD_PAD,), np.float32)
    sign_pad[MLA_NOPE:MLA_DQK] = sign
    wq = w_uq.reshape(MLA_Q_RANK, MLA_HEADS, MLA_DQK)
    wq = jnp.pad(wq, ((0, 0), (0, 0), (0, HEAD_PAD - MLA_DQK)))
    wq_rot = (wq[:, :, src_pad] * sign_pad).reshape(MLA_Q_RANK, hw).astype(BF16)
    wq = wq.reshape(MLA_Q_RANK, hw).astype(BF16)
    wkv = w_ukv.reshape(MLA_KV_RANK, MLA_HEADS, MLA_NOPE + MLA_DV)
    wk = jnp.pad(wkv[:, :, :MLA_NOPE], ((0, 0), (0, 0), (0, HEAD_PAD - MLA_NOPE))).reshape(MLA_KV_RANK, hw).astype(BF16)
    wv = wkv[:, :, MLA_NOPE:].reshape(MLA_KV_RANK, MLA_HEADS * MLA_DV).astype(BF16)
    padg = lambda g: jnp.pad(g, (0, HEAD_PAD - MLA_DQK))
    gains = [padg(g_qq)[None, :], (padg(g_qq)[src_pad] * jnp.abs(sign_pad))[None, :],
             padg(g_kk)[None, :], (padg(g_kk)[src_pad] * jnp.abs(sign_pad))[None, :]]
    operands = [gq[None, :], gkv[None, :], wq, wq_rot, wk, wv] + gains + [cos, sin]
    rope_spec = pl.BlockSpec((tm, HEAD_PAD), lambda i, j: (j, 0))
    specs = [pl.BlockSpec(a.shape, lambda i, j: (0, 0)) for a in operands[:-2]] + [rope_spec, rope_spec]
    return operands, specs


def _attn_kernel(q_ref, k_ref, v_ref, o_ref, *, ctx_len, nct, j0):
    is_ctx = pl.program_id(1) + j0 < nct

    def attend(n_keys):
        scores = lambda h: _nt(q_ref[0, h], k_ref[0, h, 0:n_keys, :])
        s_next = scores(0)
        for h in range(MLA_HEADS):
            s, s_next = s_next, (scores(h + 1) if h + 1 < MLA_HEADS else None)
            p = jnp.exp(s - jnp.max(s, axis=-1, keepdims=True))
            o = _mm(p.astype(BF16), v_ref[0, h, 0:n_keys, :]) / jnp.sum(p, axis=-1, keepdims=True)
            o_ref[0, :, h * MLA_DV:(h + 1) * MLA_DV] = o

    @pl.when(is_ctx)
    def _():
        attend(ctx_len)

    @pl.when(jnp.logical_not(is_ctx))
    def _():
        attend(k_ref.shape[2])


def _attention(q, k, v, ctx_len, tm, j0):
    bsz, _, t, _ = q.shape
    nct = ctx_len // tm
    nq = t // tm - j0
    kern = functools.partial(_attn_kernel, ctx_len=ctx_len, nct=nct, j0=j0)
    return pl.pallas_call(
        kern,
        grid=(bsz, nq),
        in_specs=[pl.BlockSpec((1, MLA_HEADS, tm, HEAD_PAD), lambda i, j: (i, 0, j + j0, 0)),
                  pl.BlockSpec((1, MLA_HEADS, t, HEAD_PAD), lambda i, j: (i, 0, 0, 0)),
                  pl.BlockSpec((1, MLA_HEADS, t, MLA_DV), lambda i, j: (i, 0, 0, 0))],
        out_specs=pl.BlockSpec((1, tm, MLA_HEADS * MLA_DV), lambda i, j: (i, j, 0)),
        out_shape=jax.ShapeDtypeStruct((bsz, nq * tm, MLA_HEADS * MLA_DV), F32),
        compiler_params=_params(("arbitrary", "arbitrary")),
        name="attention",
    )(q, k, v)


def _merge_kernel(ml_ref, hg_ref, at_ref, gate_ref, x_ref, mod_ref, gml_ref, ghg_ref, wo_ref,
                  g2_ref, rwt_ref, same_ref, x1_ref, h2_ref, aff_ref):
    head_mean = same_ref[...] * (1.0 / ML_DH)
    w_hi, w_lo = _split(rwt_ref[...])
    half = x_ref.shape[1] // 2

    def mean_squares(ch):
        ch.ml, ch.hg = ml_ref[0, ch.rows, :], hg_ref[0, ch.rows, :]
        hi, lo = _split(ch.ml * ch.ml)
        ch.ms_ml = _mm(hi, head_mean) + _mm(lo, head_mean)
        hi, lo = _split(ch.hg * ch.hg)
        ch.ms_hg = _mm(hi, head_mean) + _mm(lo, head_mean)

    def project(ch):
        gates = gate_ref[0, ch.rows, :]
        y_ml = ch.ml * lax.rsqrt(ch.ms_ml + EPS) * gml_ref[...] * jax.nn.sigmoid(gates[:, 0:ML_W])
        y_hg = ch.hg * lax.rsqrt(ch.ms_hg + EPS) * ghg_ref[...] * _silu(gates[:, ML_W:ML_W + HG_W])
        ch.y = (_mm(y_ml.astype(BF16), wo_ref[0:ML_W, :])
                + _mm(y_hg.astype(BF16), wo_ref[ML_W:ML_W + HG_W, :])
                + _mm(at_ref[0, ch.rows, :].astype(BF16), wo_ref[ML_W + HG_W:, :]))

    def route(ch):
        x1 = x_ref[0, ch.rows, :] + mod_ref[0, 0, 2:3, :] * ch.y
        x1_ref[0, ch.rows, :] = x1
        h2 = _rms_rows(x1, g2_ref[...]) * (1.0 + mod_ref[0, 0, 4:5, :]) + mod_ref[0, 0, 3:4, :]
        h2_ref[0, ch.rows, :] = h2.astype(BF16)
        h_hi, h_lo = _split(h2)
        ch.logits = _nt(w_hi, h_hi) + _nt(w_hi, h_lo) + _nt(w_lo, h_hi)

    def affinity(ch):
        e = jnp.exp(ch.logits - jnp.max(ch.logits, axis=0, keepdims=True))
        aff_ref[0, :, ch.rows] = e / jnp.sum(e, axis=0, keepdims=True)

    chains = [_Chain(rows=slice(i * half, (i + 1) * half)) for i in range(2)]
    for stage in (mean_squares, project, route, affinity):
        for ch in chains:
            stage(ch)


def _merge(ml, hg, at, ogate, xs, modtok, g_ml, g_hg, w_out, g2, router_w, tm, j0, sel, at_j0):
    bsz, n, d = xs.shape
    n_tiles = n // tm
    tok = lambda width, off: pl.BlockSpec((1, tm, width), lambda i, j: (i, j + off, 0))
    full = lambda shape: pl.BlockSpec(shape, lambda i, j: (0,) * len(shape))
    same_b = _head_tables(CHUNK)[1].astype(BF16)
    return pl.pallas_call(
        _merge_kernel,
        grid=(bsz, n_tiles),
        in_specs=[tok(ML_W, j0), tok(HG_W, j0), tok(MLA_HEADS * MLA_DV, j0 - at_j0), tok(_W_GATE, j0), tok(d, 0),
                  pl.BlockSpec((1, 1, 6, d), lambda i, j: (i, sel, 0, 0)),
                  full((1, ML_W)), full((1, HG_W)), full((d, d)), full((1, d)), full((N_EXPERTS, d)),
                  _const_spec(same_b)],
        out_specs=[tok(d, 0), tok(d, 0), pl.BlockSpec((1, N_EXPERTS, tm), lambda i, j: (i, 0, j))],
        out_shape=[jax.ShapeDtypeStruct((bsz, n, d), F32),
                   jax.ShapeDtypeStruct((bsz, n, d), BF16),
                   jax.ShapeDtypeStruct((bsz, N_EXPERTS, n), F32)],
        compiler_params=_params(("arbitrary", "arbitrary")),
        name="merge",
    )(ml, hg, at, ogate, xs, modtok, g_ml[None, :], g_hg[None, :], w_out.astype(BF16), g2[None, :],
      router_w.T, same_b)


def _route_kernel(aff_ref, rank_ref, *, cap, n):
    aff = aff_ref[0]

    def search(i, thr):
        cand = thr | (jnp.int32(1) << (30 - i))
        cnt = jnp.sum(jnp.where(aff >= pltpu.bitcast(cand, F32), 1, 0), axis=-1, keepdims=True)
        return jnp.where(cnt >= cap, cand, thr)

    thr = lax.fori_loop(0, 31, search, jnp.zeros((N_EXPERTS, 1), jnp.int32))
    above = aff >= pltpu.bitcast(thr + 1, F32)
    tied = jnp.logical_and(aff >= pltpu.bitcast(thr, F32), jnp.logical_not(above))
    need = cap - jnp.sum(jnp.where(above, 1, 0), axis=-1, keepdims=True)

    tl = min(n, TOKEN_TILE)
    r = lax.broadcasted_iota(jnp.int32, (tl, tl), 0)
    c = lax.broadcasted_iota(jnp.int32, (tl, tl), 1)
    before = jnp.where(r < c, 1.0, 0.0).astype(BF16)

    def excl_cumsum(mask):
        parts, carry = [], jnp.zeros((N_EXPERTS, 1), F32)
        for j in range(n // tl):
            m = jnp.where(mask[:, j * tl:(j + 1) * tl], 1.0, 0.0)
            parts.append(_mm(m.astype(BF16), before) + carry)
            carry = carry + jnp.sum(m, axis=-1, keepdims=True)
        return jnp.concatenate(parts, axis=-1).astype(jnp.int32)

    keep = jnp.logical_or(above, jnp.logical_and(tied, excl_cumsum(tied) < need))
    rank_ref[0] = jnp.where(keep, excl_cumsum(keep), -1)


def _route(aff_t, cap):
    bsz, e, n = aff_t.shape
    kern = functools.partial(_route_kernel, cap=cap, n=n)
    return pl.pallas_call(
        kern,
        grid=(bsz,),
        in_specs=[pl.BlockSpec((1, e, n), lambda i: (i, 0, 0))],
        out_specs=pl.BlockSpec((1, e, n), lambda i: (i, 0, 0)),
        out_shape=jax.ShapeDtypeStruct((bsz, e, n), jnp.int32),
        compiler_params=_params(("arbitrary",)),
        name="route",
    )(aff_t)


def _pick_and_gate(rank_row, aff_row, cap):
    slot = lax.broadcasted_iota(jnp.int32, (cap, rank_row.shape[1]), 0)
    chosen = rank_row == slot
    gate = jnp.sum(jnp.where(chosen, aff_row, 0.0), axis=-1, keepdims=True)
    return jnp.where(chosen, 1.0, 0.0).astype(BF16), gate


def _ffn_experts(xs, picks, gates, x1_ref, g2_ref, wg_ref, wu_ref, wd_ref, o_ref, cap):
    e = pl.program_id(1)
    hid = (_silu(_mm(xs, wg_ref[0, 0])) * _mm(xs, wu_ref[0, 0])).astype(BF16)
    out = _mm(hid, wd_ref[0, 0])

    @pl.when(e == 0)
    def _():
        o_ref[...] = jnp.zeros_like(o_ref)

    d = out.shape[1]
    for b, (pick, gate) in enumerate(zip(picks, gates)):
        out_b = (out[b * cap:(b + 1) * cap] * gate).astype(BF16)
        for j in range(d // SCATTER_COLS):
            cols = slice(j * SCATTER_COLS, (j + 1) * SCATTER_COLS)
            o_ref[b, :, cols] += _tn(pick, out_b[:, cols])

    @pl.when(e == pl.num_programs(1) - 1)
    def _():
        o_ref[...] = x1_ref[...] + g2_ref[...] * o_ref[...]


def _ffn_kernel(rank_ref, aff_ref, h_ref, x1_ref, g2_ref, wg_ref, wu_ref, wd_ref, o_ref, *, cap, bb):
    picks, gates, xs = [], [], []
    for b in range(bb):
        pick, gate = _pick_and_gate(rank_ref[b, 0], aff_ref[b, 0], cap)
        picks.append(pick)
        gates.append(gate)
        xs.append(_mm(pick, h_ref[b].astype(BF16)).astype(BF16))
    _ffn_experts(jnp.concatenate(xs, axis=0), picks, gates, x1_ref, g2_ref, wg_ref, wu_ref, wd_ref, o_ref, cap)


def _ffn(rank, aff_t, h2, x1, g2mod, wg, wu, wd, layer, cap, bb):
    bsz, n, d = h2.shape
    n_exp = wg.shape[1]
    once = dict(pipeline_mode=pl.Buffered(1))
    per_expert = lambda width: pl.BlockSpec((bb, 1, 1, width), lambda i, e: (i, e, 0, 0))
    common = [pl.BlockSpec((bb, n, d), lambda i, e: (i, 0, 0), **once),
              pl.BlockSpec((bb, 1, d), lambda i, e: (i, 0, 0)),
              pl.BlockSpec((1, 1, d, d), lambda i, e: (layer, e, 0, 0)),
              pl.BlockSpec((1, 1, d, d), lambda i, e: (layer, e, 0, 0)),
              pl.BlockSpec((1, 1, d, d), lambda i, e: (layer, e, 0, 0))]
    out_spec = pl.BlockSpec((bb, n, d), lambda i, e: (i, 0, 0))
    out_shape = jax.ShapeDtypeStruct((bsz, n, d), F32)
    rank4, aff4 = rank.reshape(bsz, n_exp, 1, n), aff_t.reshape(bsz, n_exp, 1, n)
    return pl.pallas_call(
        functools.partial(_ffn_kernel, cap=cap, bb=bb),
        grid=(bsz // bb, n_exp),
        in_specs=[per_expert(n), per_expert(n), pl.BlockSpec((bb, n, d), lambda i, e: (i, 0, 0), **once)] + common,
        out_specs=out_spec, out_shape=out_shape,
        compiler_params=_params(("arbitrary", "arbitrary")),
        name="expert_ffn",
    )(rank4, aff4, h2, x1, g2mod, wg, wu, wd)


def _rope_tables(ctx_len, seq):
    half = MLA_ROPE // 2
    inv = ROPE_THETA ** (-jnp.arange(0, half, 2, dtype=F32) / half)
    rows = seq // GRID_W
    row_pos = jnp.repeat(jnp.arange(rows), GRID_W).astype(F32)
    col_pos = jnp.broadcast_to(jnp.arange(GRID_W), (rows, GRID_W)).reshape(-1).astype(F32)

    def cs(pos):
        ang = pos[:, None] * inv[None, :]
        ang = jnp.concatenate([ang, ang], axis=-1)
        return jnp.cos(ang), jnp.sin(ang)

    cos_r, sin_r = cs(row_pos)
    cos_c, sin_c = cs(col_pos)
    zeros = lambda n: jnp.zeros((seq, n), F32)
    ones = lambda n: jnp.ones((seq, n), F32)
    tail = HEAD_PAD - MLA_DQK
    cos = jnp.concatenate([ones(MLA_NOPE), cos_r, cos_c, ones(tail)], axis=-1)
    sin = jnp.concatenate([zeros(MLA_NOPE), sin_r, sin_c, zeros(tail)], axis=-1)
    ident = lambda a, fill: jnp.concatenate([jnp.full((ctx_len, HEAD_PAD), fill, F32), a], axis=0)
    return ident(cos, 1.0), ident(sin, 0.0)


def kernel(x, c, ctx, c_ctx, ada_w, ada_b, norm1_g, norm2_g, w_in, b_in, ml_conv_w, ml_conv_b, ml_norm_g, hg_lb_logits, hg_norm_g, mla_q_norm_g, mla_w_uq, mla_kv_norm_g, mla_w_ukv, mla_q_qk_g, mla_k_qk_g, w_out, router_w, ex_w_gate, ex_w_up, ex_w_down):
    bsz, seq, d = x.shape
    ctx_len = ctx.shape[1]
    depth = ada_w.shape[0]
    t = ctx_len + seq
    tm = TOKEN_TILE if ctx_len % TOKEN_TILE == 0 else TOKEN_TILE // 2
    assert ctx_len % tm == 0 and seq % tm == 0 and ctx_len % ML_CHUNK == 0 and seq % ML_CHUNK == 0 and seq % GRID_W == 0
    nct = ctx_len // tm

    rows = -(-(bsz + 1) // 8) * 8
    cc = jnp.concatenate([c, c_ctx[None, :], jnp.zeros((rows - bsz - 1, d), F32)], axis=0)
    mod = _modulation(cc, ada_w, ada_b)
    cos, sin = _rope_tables(ctx_len, seq)
    bb_ctx = max(bb for bb in (8, 4, 2, 1) if bsz % bb == 0)

    wg, wu, wd = (w.astype(BF16) for w in (ex_w_gate, ex_w_up, ex_w_down))
    lat = x
    for layer in range(depth):
        need_ctx = layer < depth - 1
        m = mod[layer]
        modtok = jnp.stack([jnp.broadcast_to(m[bsz], (bsz, 6 * d)), m[:bsz]], axis=1).reshape(bsz, 2, 6, d)
        mla_operands, mla_specs = _mla_operands(mla_q_norm_g[layer], mla_kv_norm_g[layer], mla_w_uq[layer],
                                                mla_w_ukv[layer], mla_q_qk_g[layer], mla_k_qk_g[layer], cos, sin, tm)
        oml, ohqv, ohf, ogate, omisc, ogt, q, k, v = _in_proj(ctx, lat, modtok, norm1_g[layer], w_in[layer], b_in[layer],
                                                              mla_operands, mla_specs, tm, nct)
        gates_t = ogt.reshape(bsz, 16, t // ML_CHUNK, ML_CHUNK).transpose(0, 2, 1, 3)
        ml = _mlstm(oml, omisc, gates_t, ml_conv_w[layer], ml_conv_b[layer], ctx_len, tm)
        hg = _hgrn(ohqv, ohf, hg_lb_logits, ctx_len, layer)
        at_j0 = 0 if need_ctx else nct
        at = _attention(q, k, v, ctx_len, tm, at_j0)
        g2mod = modtok[:, :, 5:6, :]

        def post(xs, j0, sel, bb):
            x1, h2, aff = _merge(ml, hg, at, ogate, xs, modtok, ml_norm_g[layer], hg_norm_g[layer], w_out[layer],
                                 norm2_g[layer], router_w[layer], tm, j0, sel, at_j0)
            cap = EC_CAPACITY * xs.shape[1] // N_EXPERTS
            rank = _route(aff, cap)
            return _ffn(rank, aff, h2, x1, g2mod[:, sel], wg, wu, wd, layer, cap, bb)

        new_lat = post(lat, nct, 1, 1)
        if need_ctx:
            ctx = post(ctx, 0, 0, bb_ctx)
        lat = new_lat
    return lat
```
